```python
import math
import jax
import jax.numpy as jnp
from jax import lax
import numpy as np

D_MODEL = 2048
BATCH = 2
SEQ = 16384
DEPTH = 2

HEAD_DIM = 64
ATTN_BLOCK = 128
NORM_EPS = 1e-5

SWA_Q_HEADS = D_MODEL // 128
SWA_KV_HEADS = max(1, SWA_Q_HEADS // 8)
SWA_GROUP = SWA_Q_HEADS // SWA_KV_HEADS
SWA_WINDOW = 128
SWA_Q_W = SWA_Q_HEADS * HEAD_DIM
SWA_KV_W = SWA_KV_HEADS * HEAD_DIM

SSM_HEAD_DIM = 64
SSM_HEADS = D_MODEL // SSM_HEAD_DIM
SSM_D_INNER = SSM_HEADS * SSM_HEAD_DIM
SSM_GROUPS = 4
SSM_STATE = 128
SSM_BC_W = SSM_GROUPS * SSM_STATE
SSM_CONV = 4
SSM_CONV_DIM = SSM_D_INNER + 2 * SSM_BC_W
SSM_CHUNK = 128
DT_MIN = 1e-3
DT_MAX = 1e-1

AB_SPLITS = (SWA_Q_W,
             SWA_Q_W + SWA_KV_W,
             SWA_Q_W + 2 * SWA_KV_W,
             SWA_Q_W + 2 * SWA_KV_W + SSM_D_INNER,
             SWA_Q_W + 2 * SWA_KV_W + SSM_D_INNER + SSM_CONV_DIM)
AB_IN_W = AB_SPLITS[-1] + SSM_HEADS
AB_MIX_W = SWA_Q_W + SSM_D_INNER

DIL_HEADS = D_MODEL // HEAD_DIM
DIL_W = DIL_HEADS * HEAD_DIM
DIL_BRANCHES = ((128, 1), (512, 4), (2048, 16))
DIL_PAD = max(r for _, r in DIL_BRANCHES) * ATTN_BLOCK

D_FF = 4 * D_MODEL

N_EVEN = (DEPTH + 1) // 2
N_ODD = DEPTH // 2

kernel_name = 'hybrid_swa_ssd_dilated_block'


def rms_norm(x, gain):
    xf = x.astype(jnp.float32)
    y = xf * lax.rsqrt(jnp.mean(xf * xf, axis=-1, keepdims=True) + NORM_EPS)
    return (y * gain.astype(jnp.float32)).astype(x.dtype)


def banded_attention(q, k, v, max_dist, sink=None, return_lse=False):
    bsz, n, kh, g, hd = q.shape
    nb = n // ATTN_BLOCK
    qb = q.reshape(bsz, nb, ATTN_BLOCK, kh, g, hd)

    def with_prev(t):
        tb = t.reshape(bsz, nb, ATTN_BLOCK, kh, hd)
        prev = jnp.pad(tb, ((0, 0), (1, 0), (0, 0), (0, 0), (0, 0)))[:, :-1]
        return jnp.concatenate([prev, tb], axis=2)

    kw = with_prev(k)
    vw = with_prev(v)
    s = jnp.einsum('bnqhgd,bnkhd->bnhgqk', qb, kw,
                   preferred_element_type=jnp.float32) * (hd ** -0.5)
    qi = jnp.arange(ATTN_BLOCK)[:, None]
    kj = jnp.arange(2 * ATTN_BLOCK)[None, :]
    dist = ATTN_BLOCK + qi - kj
    in_band = (dist >= 0) & (dist <= max_dist)
    before_start = (jnp.arange(nb)[:, None, None] == 0) & (kj < ATTN_BLOCK)[None]
    valid = in_band[None] & ~before_start
    s = jnp.where(valid[None, :, None, None], s, -jnp.inf)
    m = jnp.max(s, axis=-1, keepdims=True)
    if sink is not None:
        sk = sink.astype(jnp.float32)[None, None, :, :, None, None]
        m = jnp.maximum(m, sk)
    p = jnp.exp(s - m)
    denom = jnp.sum(p, axis=-1, keepdims=True)
    if sink is not None:
        denom = denom + jnp.exp(sk - m)
    o = jnp.einsum('bnhgqk,bnkhd->bnqhgd', (p / denom).astype(v.dtype), vw)
    o = o.reshape(bsz, n, kh, g, hd)
    if not return_lse:
        return o
    lse = (m + jnp.log(denom))[..., 0]
    lse = jnp.moveaxis(lse, -1, 2).reshape(bsz, n, kh, g)
    return o, lse


def causal_depthwise_conv(x, w, b):
    c = x.shape[-1]
    y = lax.conv_general_dilated(x, w[:, None, :].astype(x.dtype), window_strides=(1,),
                                 padding=[(w.shape[0] - 1, 0)],
                                 dimension_numbers=('NWC', 'WIO', 'NWC'),
                                 feature_group_count=c)
    return y + b.astype(x.dtype)


def ssd_chunked(x, dt, a, bmat, cmat):
    bsz, s, h, p = x.shape
    g, nst = bmat.shape[2], bmat.shape[3]
    r = h // g
    nc = s // SSM_CHUNK
    f32 = jnp.float32
    xdt = (x.astype(f32) * dt[..., None]).reshape(bsz, nc, SSM_CHUNK, g, r, p)
    bc = bmat.astype(f32).reshape(bsz, nc, SSM_CHUNK, g, nst)
    cc = cmat.astype(f32).reshape(bsz, nc, SSM_CHUNK, g, nst)
    cum = jnp.cumsum((dt * a).reshape(bsz, nc, SSM_CHUNK, g, r), axis=2)
    cum = jnp.moveaxis(cum, 2, -1)
    causal = jnp.tril(jnp.ones((SSM_CHUNK, SSM_CHUNK), dtype=bool))
    seg = jnp.exp(jnp.where(causal, cum[..., :, None] - cum[..., None, :], -jnp.inf))
    cb = jnp.einsum('bctgn,bcsgn->bcgts', cc, bc)
    y_diag = jnp.einsum('bcgts,bcgrts,bcsgrp->bctgrp', cb, seg, xdt)
    states = jnp.einsum('bcsgn,bcgrs,bcsgrp->bcgrpn', bc, jnp.exp(cum[..., -1:] - cum), xdt)

    def carry_state(hst, inp):
        st, dec = inp
        return hst * dec[..., None, None] + st, hst

    h0 = jnp.zeros((bsz, g, r, p, nst), f32)
    _, h_in = lax.scan(carry_state, h0,
                       (jnp.moveaxis(states, 1, 0), jnp.moveaxis(jnp.exp(cum[..., -1]), 1, 0)))
    h_in = jnp.moveaxis(h_in, 0, 1)
    y_off = jnp.einsum('bctgn,bcgrt,bcgrpn->bctgrp', cc, jnp.exp(cum), h_in)
    return (y_diag + y_off).reshape(bsz, s, h, p)


def gated_group_rmsnorm(y, z, gain):
    gy = y.astype(jnp.float32) * jax.nn.silu(z.astype(jnp.float32))
    shp = gy.shape
    gy = gy.reshape(shp[:-1] + (SSM_GROUPS, shp[-1] // SSM_GROUPS))
    gy = gy * lax.rsqrt(jnp.mean(gy * gy, axis=-1, keepdims=True) + NORM_EPS)
    return gy.reshape(shp) * gain.astype(jnp.float32)


def swa_ssd_mixer(h, w_in, q_norm, k_norm, sinks, conv_w, conv_b, dt_bias, a_log,
                  d_skip, gate_norm, w_out):
    bsz, s, _ = h.shape
    proj = jnp.einsum('bsd,de->bse', h, w_in)
    q, k, v, z, xbc, dt_raw = jnp.split(proj, AB_SPLITS, axis=-1)
    q = rms_norm(q.reshape(bsz, s, SWA_KV_HEADS, SWA_GROUP, HEAD_DIM), q_norm)
    k = rms_norm(k.reshape(bsz, s, SWA_KV_HEADS, HEAD_DIM), k_norm)
    v = v.reshape(bsz, s, SWA_KV_HEADS, HEAD_DIM)
    attn = banded_attention(q, k, v, SWA_WINDOW - 1,
                            sink=sinks.reshape(SWA_KV_HEADS, SWA_GROUP))
    attn = attn.reshape(bsz, s, SWA_Q_W)
    xbc = jax.nn.silu(causal_depthwise_conv(xbc, conv_w, conv_b))
    xs, bm, cm = jnp.split(xbc, (SSM_D_INNER, SSM_D_INNER + SSM_BC_W), axis=-1)
    dt = jax.nn.softplus(dt_raw.astype(jnp.float32) + dt_bias.astype(jnp.float32))
    a = -jnp.exp(a_log.astype(jnp.float32))
    xh = xs.reshape(bsz, s, SSM_HEADS, SSM_HEAD_DIM)
    y = ssd_chunked(xh, dt, a,
                    bm.reshape(bsz, s, SSM_GROUPS, SSM_STATE),
                    cm.reshape(bsz, s, SSM_GROUPS, SSM_STATE))
    y = y + d_skip.astype(jnp.float32)[:, None] * xh.astype(jnp.float32)
    y = gated_group_rmsnorm(y.reshape(bsz, s, SSM_D_INNER), z, gate_norm).astype(h.dtype)
    mixed = jnp.concatenate([attn, y], axis=-1)
    return jnp.einsum('bse,ed->bsd', mixed, w_out)


def to_strided(t, dil):
    bsz, s, h, d = t.shape
    t = t.reshape(bsz, s // dil, dil, h, d).transpose(0, 2, 1, 3, 4)
    return t.reshape(bsz * dil, s // dil, h, d)


def from_strided(t, bsz, dil):
    n = t.shape[1]
    rest = t.shape[2:]
    t = jnp.swapaxes(t.reshape((bsz, dil, n) + rest), 1, 2)
    return t.reshape((bsz, n * dil) + rest)


def dilated_mixer(h, w_qkv, q_norm, k_norm, w_o):
    bsz, s, _ = h.shape
    q, k, v = jnp.split(jnp.einsum('bsd,de->bse', h, w_qkv), 3, axis=-1)
    q = rms_norm(q.reshape(bsz, s, DIL_HEADS, HEAD_DIM), q_norm)
    k = rms_norm(k.reshape(bsz, s, DIL_HEADS, HEAD_DIM), k_norm)
    v = v.reshape(bsz, s, DIL_HEADS, HEAD_DIM)
    s_pad = -(-s // DIL_PAD) * DIL_PAD
    pad = ((0, 0), (0, s_pad - s), (0, 0), (0, 0))
    q, k, v = jnp.pad(q, pad), jnp.pad(k, pad), jnp.pad(v, pad)
    outs = []
    lses = []
    for window, dil in DIL_BRANCHES:
        o, lse = banded_attention(to_strided(q, dil)[:, :, :, None, :], to_strided(k, dil),
                                  to_strided(v, dil), window // dil, return_lse=True)
        outs.append(from_strided(o[:, :, :, 0], bsz, dil))
        lses.append(from_strided(lse[..., 0], bsz, dil))
    weights = jax.nn.softmax(jnp.stack(lses, axis=0), axis=0)
    o = jnp.einsum('xbsh,xbshd->bshd', weights, jnp.stack(outs, axis=0).astype(jnp.float32))
    o = o[:, :s].reshape(bsz, s, DIL_W).astype(h.dtype)
    return jnp.einsum('bse,ed->bsd', o, w_o)


def squared_relu_mlp(h, w_up, w_down):
    u = jnp.einsum('bsd,df->bsf', h, w_up)
    return jnp.einsum('bsf,fd->bsd', jnp.square(jax.nn.relu(u)), w_down)


def setup_inputs(seed: int = 0) -> dict:
    key = jax.random.key(seed)
    ks = jax.random.split(key, 20)
    f32 = jnp.float32

    def dense(k, shape):
        return jax.random.normal(k, shape, f32) * (shape[-2] ** -0.5)

    def gain(k, shape):
        return 1.0 + 0.02 * jax.random.normal(k, shape, f32)

    dt = jnp.exp(jax.random.uniform(ks[11], (N_EVEN, SSM_HEADS), f32,
                                    math.log(DT_MIN), math.log(DT_MAX)))
    dt_bias = dt + jnp.log(-jnp.expm1(-dt))
    return {
        'x': jax.random.normal(ks[0], (BATCH, SEQ, D_MODEL), f32),
        'norm_mix': gain(ks[1], (DEPTH, D_MODEL)),
        'norm_ffn': gain(ks[2], (DEPTH, D_MODEL)),
        'w_up': dense(ks[3], (DEPTH, D_MODEL, D_FF)),
        'w_down': dense(ks[4], (DEPTH, D_FF, D_MODEL)),
        'ab_w_in': dense(ks[5], (N_EVEN, D_MODEL, AB_IN_W)),
        'ab_q_norm': gain(ks[6], (N_EVEN, HEAD_DIM)),
        'ab_k_norm': gain(ks[7], (N_EVEN, HEAD_DIM)),
        'ab_sinks': jax.random.normal(ks[8], (N_EVEN, SWA_Q_HEADS), f32),
        'ab_conv_w': jax.random.uniform(ks[9], (N_EVEN, SSM_CONV, SSM_CONV_DIM), f32, -0.5, 0.5),
        'ab_conv_b': jax.random.uniform(ks[10], (N_EVEN, SSM_CONV_DIM), f32, -0.1, 0.1),
        'ab_dt_bias': dt_bias,
        'ab_a_log': jnp.log(jax.random.uniform(ks[12], (N_EVEN, SSM_HEADS), f32, 1.0, 16.0)),
        'ab_d_skip': 1.0 + 0.1 * jax.random.normal(ks[13], (N_EVEN, SSM_HEADS), f32),
        'ab_gate_norm': gain(ks[14], (N_EVEN, SSM_D_INNER)),
        'ab_w_out': dense(ks[15], (N_EVEN, AB_MIX_W, D_MODEL)),
        'c_w_qkv': dense(ks[16], (N_ODD, D_MODEL, 3 * DIL_W)),
        'c_q_norm': gain(ks[17], (N_ODD, HEAD_DIM)),
        'c_k_norm': gain(ks[18], (N_ODD, HEAD_DIM)),
        'c_w_o': dense(ks[19], (N_ODD, DIL_W, D_MODEL)),
    }


def reference(x, norm_mix, norm_ffn, w_up, w_down, ab_w_in, ab_q_norm, ab_k_norm, ab_sinks,
              ab_conv_w, ab_conv_b, ab_dt_bias, ab_a_log, ab_d_skip, ab_gate_norm, ab_w_out,
              c_w_qkv, c_q_norm, c_k_norm, c_w_o):
    for layer in range(DEPTH):
        i = layer // 2
        h = rms_norm(x, norm_mix[layer])
        if layer % 2 == 0:
            x = x + swa_ssd_mixer(h, ab_w_in[i], ab_q_norm[i], ab_k_norm[i], ab_sinks[i],
                                  ab_conv_w[i], ab_conv_b[i], ab_dt_bias[i], ab_a_log[i],
                                  ab_d_skip[i], ab_gate_norm[i], ab_w_out[i])
        else:
            x = x + dilated_mixer(h, c_w_qkv[i], c_q_norm[i], c_k_norm[i], c_w_o[i])
        h = rms_norm(x, norm_ffn[layer])
        x = x + squared_relu_mlp(h, w_up[layer], w_down[layer])
    return x
```

```python
import functools

import jax
import jax.numpy as jnp
from jax import lax
from jax.experimental import pallas as pl
from jax.experimental.pallas import tpu as pltpu

F32 = jnp.float32
BF16 = jnp.bfloat16

NORM_EPS = 1e-5
HEAD_DIM = 64
ATTN_BLOCK = 128
SWA_KV_HEADS = 2
SWA_WINDOW = 128
SSM_GROUPS = 4
SSM_STATE = 128
SSM_CONV = 4
SSM_CHUNK = 128
DIL_BRANCHES = ((128, 1), (512, 4), (2048, 16))

LANES = 128
CONV_PAD = 8
VMEM_LIMIT_BYTES = 56 * 1024 * 1024


def _compiler_params(semantics):
    return pltpu.CompilerParams(dimension_semantics=semantics, vmem_limit_bytes=VMEM_LIMIT_BYTES)


def _rms_norm_rows(x, gain):
    ms = jnp.mean(x * x, axis=-1, keepdims=True)
    return x * lax.rsqrt(ms + NORM_EPS) * gain


def _norm_matmul_kernel(x_ref, g_ref, w_ref, o_ref, h_ref):
    @pl.when(pl.program_id(1) == 0)
    def _():
        h_ref[...] = _rms_norm_rows(x_ref[...], g_ref[...]).astype(BF16)

    o_ref[...] = jnp.dot(h_ref[...], w_ref[...], preferred_element_type=F32).astype(o_ref.dtype)


def _norm_matmul(x, gain, w, out_dtype, tm, tn, name):
    t, d = x.shape
    n = w.shape[1]
    tm = min(tm, t)
    return pl.pallas_call(
        _norm_matmul_kernel,
        grid=(t // tm, n // tn),
        in_specs=[
            pl.BlockSpec((tm, d), lambda i, j: (i, 0)),
            pl.BlockSpec((1, d), lambda i, j: (0, 0)),
            pl.BlockSpec((d, tn), lambda i, j: (0, j)),
        ],
        out_specs=pl.BlockSpec((tm, tn), lambda i, j: (i, j)),
        out_shape=jax.ShapeDtypeStruct((t, n), out_dtype),
        scratch_shapes=[pltpu.VMEM((tm, d), BF16)],
        compiler_params=_compiler_params(("parallel", "arbitrary")),
        name=name,
    )(x, gain.reshape(1, d).astype(F32), w)


def _pair_ones():
    r = lax.broadcasted_iota(jnp.int32, (2 * LANES, LANES), 0) % LANES
    c = lax.broadcasted_iota(jnp.int32, (2 * LANES, LANES), 1)
    return ((r // HEAD_DIM) == (c // HEAD_DIM)).astype(BF16)


def _head_rms(t, ones_bd):
    sq = t * t
    hi = sq.astype(BF16)
    lo = (sq - hi.astype(F32)).astype(BF16)
    ss = jnp.dot(jnp.concatenate([hi, lo], axis=1), ones_bd, preferred_element_type=F32)
    return t * lax.rsqrt(ss * (1.0 / HEAD_DIM) + NORM_EPS)


def _band_mask(first_block, max_dist):
    qi = lax.broadcasted_iota(jnp.int32, (ATTN_BLOCK, 2 * ATTN_BLOCK), 0)
    kj = lax.broadcasted_iota(jnp.int32, (ATTN_BLOCK, 2 * ATTN_BLOCK), 1)
    dist = ATTN_BLOCK + qi - kj
    in_band = (dist >= 0) & (dist <= max_dist)
    return in_band & ((kj >= ATTN_BLOCK) | jnp.logical_not(first_block))


def _attn_pair(qn, kn, v2, valid, lane_lo, sinks=None):
    q_heads = (jnp.where(lane_lo, qn, 0.0).astype(BF16), jnp.where(lane_lo, 0.0, qn).astype(BF16))
    ps, ls, lses = [], [], []
    for idx, qh in enumerate(q_heads):
        s = lax.dot_general(qh, kn, (((1,), (1,)), ((), ())), preferred_element_type=F32)
        s = jnp.where(valid, s, -jnp.inf)
        m = jnp.max(s, axis=-1, keepdims=True)
        if sinks is not None:
            m = jnp.maximum(m, sinks[idx])
        p = jnp.exp(s - m)
        l = jnp.sum(p, axis=-1, keepdims=True)
        if sinks is not None:
            l = l + jnp.exp(sinks[idx] - m)
        ps.append(p.astype(BF16))
        ls.append(l)
        lses.append(m + jnp.log(l))
    zero = jnp.zeros_like(v2)
    v_heads = jnp.concatenate([jnp.where(lane_lo, v2, zero), jnp.where(lane_lo, zero, v2)], axis=0)
    o = jnp.dot(jnp.concatenate(ps, axis=1), v_heads, preferred_element_type=F32)
    o = o * jnp.where(lane_lo, 1.0 / ls[0], 1.0 / ls[1])
    return o, lses[0], lses[1]


def _lane_lo():
    return lax.broadcasted_iota(jnp.int32, (1, LANES), 1) < HEAD_DIM


def _swa_kernel(sinks_ref, q_ref, kvp_ref, kvc_ref, gq_ref, gk_ref, o_ref):
    first = pl.program_id(1) == 0
    valid = _band_mask(first, SWA_WINDOW - 1)
    lane_lo = _lane_lo()
    ones_bd = _pair_ones()
    kv2 = jnp.concatenate([kvp_ref[...], kvc_ref[...]], axis=0)
    kn = _head_rms(kv2[:, :LANES].astype(F32), ones_bd) * gk_ref[...]
    v2 = kv2[:, LANES:].astype(F32)
    kr = pltpu.roll(kn, HEAD_DIM, axis=1)
    vr = pltpu.roll(v2, HEAD_DIM, axis=1)
    k_dup = (jnp.where(lane_lo, kn, kr).astype(BF16), jnp.where(lane_lo, kr, kn).astype(BF16))
    v_dup = (jnp.where(lane_lo, v2, vr).astype(BF16), jnp.where(lane_lo, vr, v2).astype(BF16))
    n_pairs = q_ref.shape[1] // LANES
    pairs_per_kv = n_pairs // SWA_KV_HEADS
    gq = gq_ref[...] * (HEAD_DIM ** -0.5)
    for j in range(n_pairs):
        kh = j // pairs_per_kv
        cols = slice(j * LANES, (j + 1) * LANES)
        qn = _head_rms(q_ref[:, cols].astype(F32), ones_bd) * gq
        o, _, _ = _attn_pair(qn, k_dup[kh], v_dup[kh], valid, lane_lo,
                             sinks=(sinks_ref[2 * j], sinks_ref[2 * j + 1]))
        o_ref[:, cols] = o.astype(o_ref.dtype)


def _swa_attention(proj, q_col_block, kv_col_block, q_norm, k_norm, sinks, batch, seq, q_width):
    t = proj.shape[0]
    nb = seq // ATTN_BLOCK
    gq = jnp.tile(q_norm.astype(F32), 2).reshape(1, LANES)
    gk = jnp.tile(k_norm.astype(F32), 2).reshape(1, LANES)
    return pl.pallas_call(
        _swa_kernel,
        grid=(batch, nb),
        in_specs=[
            pl.BlockSpec(memory_space=pltpu.SMEM),
            pl.BlockSpec((ATTN_BLOCK, q_width), lambda b, n: (b * nb + n, q_col_block)),
            pl.BlockSpec((ATTN_BLOCK, 2 * LANES), lambda b, n: (b * nb + jnp.maximum(n - 1, 0), kv_col_block)),
            pl.BlockSpec((ATTN_BLOCK, 2 * LANES), lambda b, n: (b * nb + n, kv_col_block)),
            pl.BlockSpec((1, LANES), lambda b, n: (0, 0)),
            pl.BlockSpec((1, LANES), lambda b, n: (0, 0)),
        ],
        out_specs=pl.BlockSpec((ATTN_BLOCK, q_width), lambda b, n: (b * nb + n, 0)),
        out_shape=jax.ShapeDtypeStruct((t, q_width), BF16),
        compiler_params=_compiler_params(("parallel", "arbitrary")),
        name="swa_attention",
    )(sinks.astype(F32), proj, proj, proj, gq, gk)


def _dilated_kernel(q_ref, kp_ref, kc_ref, vp_ref, vc_ref, gq_ref, gk_ref, o_ref, lse_ref, *, max_dist):
    first = pl.program_id(2) == 0
    valid = _band_mask(first, max_dist)
    lane = lax.broadcasted_iota(jnp.int32, (1, LANES), 1)
    lane_lo = lane < HEAD_DIM
    ones_bd = _pair_ones()
    gq = gq_ref[...] * (HEAD_DIM ** -0.5)
    gk = gk_ref[...]
    n_pairs = q_ref.shape[1] // LANES

    def body(j, lse_acc):
        cols = pl.ds(pl.multiple_of(j * LANES, LANES), LANES)
        qn = _head_rms(q_ref[:, cols].astype(F32), ones_bd) * gq
        k2 = jnp.concatenate([kp_ref[:, cols], kc_ref[:, cols]], axis=0).astype(F32)
        kn = (_head_rms(k2, ones_bd) * gk).astype(BF16)
        v2 = jnp.concatenate([vp_ref[:, cols], vc_ref[:, cols]], axis=0)
        o, lse_a, lse_b = _attn_pair(qn, kn, v2, valid, lane_lo)
        o_ref[:, cols] = o.astype(o_ref.dtype)
        return jnp.where(lane == 2 * j, lse_a, jnp.where(lane == 2 * j + 1, lse_b, lse_acc))

    lse_ref[...] = lax.fori_loop(0, n_pairs, body, jnp.zeros((ATTN_BLOCK, LANES), F32))


def _dilated_branch(qkv, q_norm, k_norm, batch, seq, width, window, dil):
    t = qkv.shape[0]
    sn = seq // dil
    nb = sn // ATTN_BLOCK
    view = qkv.reshape(t // dil, dil * 3 * width)
    gq = jnp.tile(q_norm.astype(F32), 2).reshape(1, LANES)
    gk = jnp.tile(k_norm.astype(F32), 2).reshape(1, LANES)

    def cur(which):
        return pl.BlockSpec((ATTN_BLOCK, width), lambda b, r, n: (b * nb + n, 3 * r + which))

    def prev(which):
        return pl.BlockSpec((ATTN_BLOCK, width), lambda b, r, n: (b * nb + jnp.maximum(n - 1, 0), 3 * r + which))

    o, lse = pl.pallas_call(
        functools.partial(_dilated_kernel, max_dist=window // dil),
        grid=(batch, dil, nb),
        in_specs=[cur(0), prev(1), cur(1), prev(2), cur(2),
                  pl.BlockSpec((1, LANES), lambda b, r, n: (0, 0)),
                  pl.BlockSpec((1, LANES), lambda b, r, n: (0, 0))],
        out_specs=[pl.BlockSpec((ATTN_BLOCK, width), lambda b, r, n: (b * nb + n, r)),
                   pl.BlockSpec((ATTN_BLOCK, LANES), lambda b, r, n: (b * nb + n, r))],
        out_shape=[jax.ShapeDtypeStruct((t // dil, dil * width), BF16),
                   jax.ShapeDtypeStruct((t // dil, dil * LANES), F32)],
        compiler_params=_compiler_params(("parallel", "parallel", "arbitrary")),
        name=f"dilated_attention_d{dil}",
    )(view, view, view, view, view, gq, gk)
    return o.reshape(t, width), lse.reshape(t, LANES)


def _softplus(x):
    return jnp.maximum(x, 0.0) + jnp.log1p(jnp.exp(-jnp.abs(x)))


def _silu(x):
    return x * (1.0 / (1.0 + jnp.exp(-x)))


def _expand_heads(v, head_mask, e3):
    n_heads = e3.shape[1] // HEAD_DIM
    v = jnp.where(head_mask, v, 0.0)
    hi = v.astype(BF16).astype(F32)
    r = v - hi
    mid = r.astype(BF16).astype(F32)
    lo = (r - mid).astype(BF16).astype(F32)
    cat = hi + pltpu.roll(mid, n_heads, axis=1) + pltpu.roll(lo, 2 * n_heads, axis=1)
    return jnp.dot(cat.astype(BF16), e3, preferred_element_type=F32)


def _ssd_kernel(xbc_ref, z_ref, dt_ref, cw_ref, cb_ref, dtb_ref, a_ref, dskip_ref, gn_ref, tril_ref, e3_ref,
                y_ref, buf_ref, state_ref):
    chunk = SSM_CHUNK
    d_inner = z_ref.shape[1]
    n_heads = d_inner // HEAD_DIM
    group_w = d_inner // SSM_GROUPS
    heads_per_group = n_heads // SSM_GROUPS

    @pl.when(pl.program_id(1) == 0)
    def _():
        buf_ref[0:CONV_PAD, :] = jnp.zeros((CONV_PAD, buf_ref.shape[1]), F32)
        state_ref[...] = jnp.zeros(state_ref.shape, F32)

    buf_ref[CONV_PAD:CONV_PAD + chunk, :] = xbc_ref[...].astype(F32)
    conv = cb_ref[...]
    for j in range(SSM_CONV):
        conv = conv + cw_ref[j:j + 1, :] * buf_ref[pl.ds(CONV_PAD - (SSM_CONV - 1) + j, chunk), :]
    buf_ref[0:CONV_PAD, :] = buf_ref[chunk:chunk + CONV_PAD, :]
    xbc = _silu(conv)
    xs = xbc[:, :d_inner]

    lane = lax.broadcasted_iota(jnp.int32, (1, LANES), 1)
    head_mask = lane < n_heads
    lane_lo = lane < HEAD_DIM
    e3 = e3_ref[...]

    dt = jnp.where(head_mask, _softplus(dt_ref[...] + dtb_ref[...]), 0.0)
    cum = jnp.dot(tril_ref[...], dt * a_ref[...], precision=lax.Precision.HIGHEST,
                  preferred_element_type=F32)
    cum_t = cum.T
    ecum_e = _expand_heads(jnp.exp(cum), head_mask, e3)
    dte_e = _expand_heads(jnp.exp(cum[chunk - 1:chunk, :] - cum), head_mask, e3)
    xdt = xs * _expand_heads(dt, head_mask, e3)
    xend = xdt * dte_e

    ti = lax.broadcasted_iota(jnp.int32, (chunk, chunk), 0)
    si = lax.broadcasted_iota(jnp.int32, (chunk, chunk), 1)
    causal = ti >= si

    for g in range(SSM_GROUPS):
        gcols = slice(g * group_w, (g + 1) * group_w)
        b_f32 = xbc[:, d_inner + g * SSM_STATE: d_inner + (g + 1) * SSM_STATE]
        c_off = d_inner + SSM_GROUPS * SSM_STATE
        c_bf = xbc[:, c_off + g * SSM_STATE: c_off + (g + 1) * SSM_STATE].astype(BF16)
        cb = lax.dot_general(c_bf, b_f32.astype(BF16), (((1,), (1,)), ((), ())), preferred_element_type=F32)
        st = state_ref[g]
        y_off = jnp.dot(c_bf, st.astype(BF16), preferred_element_type=F32) * ecum_e[:, gcols]
        state_ref[g] = st * ecum_e[chunk - 1:chunk, gcols] + jnp.dot(
            b_f32.T.astype(BF16), xend[:, gcols].astype(BF16), preferred_element_type=F32)

        y_pairs = []
        for pair in range(heads_per_group // 2):
            h0 = g * heads_per_group + 2 * pair
            ms = []
            for h in (h0, h0 + 1):
                diff = cum[:, h:h + 1] - cum_t[h:h + 1, :]
                seg = jnp.exp(jnp.where(causal, diff, -jnp.inf))
                ms.append((cb * seg).astype(BF16))
            xp = xdt[:, h0 * HEAD_DIM: h0 * HEAD_DIM + LANES]
            x_heads = jnp.concatenate([jnp.where(lane_lo, xp, 0.0), jnp.where(lane_lo, 0.0, xp)], axis=0)
            y_pairs.append(jnp.dot(jnp.concatenate(ms, axis=1), x_heads.astype(BF16),
                                   preferred_element_type=F32))
        y = jnp.concatenate(y_pairs, axis=1) + y_off + dskip_ref[:, gcols] * xs[:, gcols]
        gy = y * _silu(z_ref[:, gcols].astype(F32))
        ms_g = jnp.mean(gy * gy, axis=-1, keepdims=True)
        y_ref[:, gcols] = (gy * lax.rsqrt(ms_g + NORM_EPS) * gn_ref[:, gcols]).astype(y_ref.dtype)


def _ssd_mixer(proj, z_col_block, xbc_col_block, dt_raw, conv_w, conv_b, dt_bias, a_log, d_skip, gate_norm,
               batch, seq):
    t = proj.shape[0]
    d_inner = gate_norm.shape[0]
    n_heads = d_inner // HEAD_DIM
    conv_dim = conv_w.shape[1]
    nc = seq // SSM_CHUNK

    def pad_heads(v):
        return jnp.zeros((1, LANES), F32).at[0, :n_heads].set(v.astype(F32))

    tril = (lax.broadcasted_iota(jnp.int32, (SSM_CHUNK, SSM_CHUNK), 0)
            >= lax.broadcasted_iota(jnp.int32, (SSM_CHUNK, SSM_CHUNK), 1)).astype(F32)
    row = lax.broadcasted_iota(jnp.int32, (LANES, d_inner), 0)
    col = lax.broadcasted_iota(jnp.int32, (LANES, d_inner), 1)
    e3 = ((row < 3 * n_heads) & ((row % n_heads) == (col // HEAD_DIM))).astype(BF16)
    const = lambda shape: pl.BlockSpec(shape, lambda b, c: (0, 0))
    return pl.pallas_call(
        _ssd_kernel,
        grid=(batch, nc),
        in_specs=[
            pl.BlockSpec((SSM_CHUNK, conv_dim), lambda b, c: (b * nc + c, xbc_col_block)),
            pl.BlockSpec((SSM_CHUNK, d_inner), lambda b, c: (b * nc + c, z_col_block)),
            pl.BlockSpec((SSM_CHUNK, LANES), lambda b, c: (b * nc + c, 0)),
            const((SSM_CONV, conv_dim)), const((1, conv_dim)), const((1, LANES)), const((1, LANES)),
            const((1, d_inner)), const((1, d_inner)), const((SSM_CHUNK, SSM_CHUNK)), const((LANES, d_inner)),
        ],
        out_specs=pl.BlockSpec((SSM_CHUNK, d_inner), lambda b, c: (b * nc + c, 0)),
        out_shape=jax.ShapeDtypeStruct((t, d_inner), BF16),
        scratch_shapes=[pltpu.VMEM((SSM_CHUNK + CONV_PAD, conv_dim), F32),
                        pltpu.VMEM((SSM_GROUPS, SSM_STATE, d_inner // SSM_GROUPS), F32)],
        compiler_params=_compiler_params(("parallel", "arbitrary")),
        name="ssd_mixer",
    )(proj, proj, dt_raw, conv_w.astype(F32), conv_b.reshape(1, conv_dim).astype(F32),
      pad_heads(dt_bias), pad_heads(-jnp.exp(a_log.astype(F32))),
      jnp.repeat(d_skip.astype(F32), HEAD_DIM).reshape(1, d_inner), gate_norm.reshape(1, d_inner).astype(F32),
      tril, e3)


def _outproj_swa_ssd_kernel(x_ref, a_ref, y_ref, wa_ref, wy_ref, o_ref):
    o_ref[...] = (x_ref[...]
                  + jnp.dot(a_ref[...], wa_ref[...], preferred_element_type=F32)
                  + jnp.dot(y_ref[...], wy_ref[...], preferred_element_type=F32))


def _outproj_swa_ssd(x, attn, y, w_attn, w_y, tm, tn):
    t, d = x.shape
    tm = min(tm, t)
    return pl.pallas_call(
        _outproj_swa_ssd_kernel,
        grid=(t // tm, d // tn),
        in_specs=[
            pl.BlockSpec((tm, tn), lambda i, j: (i, j)),
            pl.BlockSpec((tm, attn.shape[1]), lambda i, j: (i, 0)),
            pl.BlockSpec((tm, y.shape[1]), lambda i, j: (i, 0)),
            pl.BlockSpec((attn.shape[1], tn), lambda i, j: (0, j)),
            pl.BlockSpec((y.shape[1], tn), lambda i, j: (0, j)),
        ],
        out_specs=pl.BlockSpec((tm, tn), lambda i, j: (i, j)),
        out_shape=jax.ShapeDtypeStruct((t, d), F32),
        compiler_params=_compiler_params(("parallel", "arbitrary")),
        name="outproj_swa_ssd",
    )(x, attn, y, w_attn, w_y)


def _outproj_dilated_kernel(x_ref, o1_ref, o2_ref, o3_ref, l1_ref, l2_ref, l3_ref, e2_ref, w_ref, out_ref, om_ref):
    @pl.when(pl.program_id(1) == 0)
    def _():
        n_heads = o1_ref.shape[1] // HEAD_DIM
        lane = lax.broadcasted_iota(jnp.int32, (1, LANES), 1)
        head_mask = lane < n_heads
        lses = (l1_ref[...], l2_ref[...], l3_ref[...])
        m = jnp.maximum(jnp.maximum(lses[0], lses[1]), lses[2])
        es = [jnp.exp(l - m) for l in lses]
        inv = 1.0 / (es[0] + es[1] + es[2])
        acc = None
        for e, o_ref in zip(es, (o1_ref, o2_ref, o3_ref)):
            w = jnp.where(head_mask, e * inv, 0.0)
            hi = w.astype(BF16).astype(F32)
            lo = (w - hi).astype(BF16).astype(F32)
            cat = (hi + pltpu.roll(lo, n_heads, axis=1)).astype(BF16)
            term = jnp.dot(cat, e2_ref[...], preferred_element_type=F32) * o_ref[...].astype(F32)
            acc = term if acc is None else acc + term
        om_ref[...] = acc.astype(BF16)

    out_ref[...] = x_ref[...] + jnp.dot(om_ref[...], w_ref[...], preferred_element_type=F32)


def _outproj_dilated(x, outs, lses, w_o, tm, tn):
    t, d = x.shape
    width = w_o.shape[0]
    n_heads = width // HEAD_DIM
    tm = min(tm, t)
    row = lax.broadcasted_iota(jnp.int32, (LANES, width), 0)
    col = lax.broadcasted_iota(jnp.int32, (LANES, width), 1)
    e2 = ((row < 2 * n_heads) & ((row % n_heads) == (col // HEAD_DIM))).astype(BF16)
    o_spec = pl.BlockSpec((tm, width), lambda i, j: (i, 0))
    l_spec = pl.BlockSpec((tm, LANES), lambda i, j: (i, 0))
    return pl.pallas_call(
        _outproj_dilated_kernel,
        grid=(t // tm, d // tn),
        in_specs=[pl.BlockSpec((tm, tn), lambda i, j: (i, j)), o_spec, o_spec, o_spec, l_spec, l_spec, l_spec,
                  pl.BlockSpec((LANES, width), lambda i, j: (0, 0)),
                  pl.BlockSpec((width, tn), lambda i, j: (0, j))],
        out_specs=pl.BlockSpec((tm, tn), lambda i, j: (i, j)),
        out_shape=jax.ShapeDtypeStruct((t, d), F32),
        scratch_shapes=[pltpu.VMEM((tm, width), BF16)],
        compiler_params=_compiler_params(("parallel", "arbitrary")),
        name="outproj_dilated",
    )(x, *outs, *lses, e2, w_o)


def _mlp_kernel(x_ref, g_ref, wu_ref, wd_ref, o_ref, h_ref, acc_ref):
    j = pl.program_id(1)

    @pl.when(j == 0)
    def _():
        h_ref[...] = _rms_norm_rows(x_ref[...], g_ref[...]).astype(BF16)

    u = jnp.dot(h_ref[...], wu_ref[...], preferred_element_type=F32)
    a = jnp.square(jnp.maximum(u, 0.0)).astype(BF16)
    contrib = jnp.dot(a, wd_ref[...], preferred_element_type=F32)

    @pl.when(j == 0)
    def _():
        acc_ref[...] = contrib

    @pl.when(j > 0)
    def _():
        acc_ref[...] += contrib

    @pl.when(j == pl.num_programs(1) - 1)
    def _():
        o_ref[...] = x_ref[...] + acc_ref[...]


def _mlp(x, gain, w_up, w_down, tm, tf):
    t, d = x.shape
    ff = w_up.shape[1]
    tm = min(tm, t)
    return pl.pallas_call(
        _mlp_kernel,
        grid=(t // tm, ff // tf),
        in_specs=[
            pl.BlockSpec((tm, d), lambda i, j: (i, 0)),
            pl.BlockSpec((1, d), lambda i, j: (0, 0)),
            pl.BlockSpec((d, tf), lambda i, j: (0, j)),
            pl.BlockSpec((tf, d), lambda i, j: (j, 0)),
        ],
        out_specs=pl.BlockSpec((tm, d), lambda i, j: (i, 0)),
        out_shape=jax.ShapeDtypeStruct((t, d), F32),
        scratch_shapes=[pltpu.VMEM((tm, d), BF16), pltpu.VMEM((tm, d), F32)],
        compiler_params=_compiler_params(("parallel", "arbitrary")),
        name="sqrelu_mlp",
    )(x, gain.reshape(1, d).astype(F32), w_up, w_down)


def _swa_ssd_layer(x, gain, w_in, q_norm, k_norm, sinks, conv_w, conv_b, dt_bias, a_log, d_skip, gate_norm,
                   w_out, batch, seq):
    d_inner = gate_norm.shape[0]
    n_ssm_heads = dt_bias.shape[0]
    q_w = sinks.shape[0] * HEAD_DIM
    kv_w = SWA_KV_HEADS * HEAD_DIM
    conv_dim = conv_w.shape[1]
    splits = (q_w, q_w + kv_w, q_w + 2 * kv_w, q_w + 2 * kv_w + d_inner, q_w + 2 * kv_w + d_inner + conv_dim)
    wq, wk, wv, wz, wxbc, wdt = jnp.split(w_in, splits, axis=1)
    w_main = jnp.concatenate([wz, wq, wxbc, wk, wv], axis=1).astype(BF16)
    assert d_inner % q_w == 0 and conv_dim == d_inner + q_w and (d_inner + q_w + conv_dim) % (2 * kv_w) == 0
    w_dt = jnp.pad(wdt, ((0, 0), (0, LANES - n_ssm_heads))).astype(BF16)

    proj = _norm_matmul(x, gain, w_main, BF16, tm=1024, tn=w_main.shape[1] // 5, name="norm_inproj")
    dt_raw = _norm_matmul(x, gain, w_dt, F32, tm=1024, tn=LANES, name="norm_dtproj")
    attn = _swa_attention(proj, d_inner // q_w, (d_inner + q_w + conv_dim) // (2 * kv_w), q_norm, k_norm, sinks,
                          batch, seq, q_w)
    y = _ssd_mixer(proj, 0, 1, dt_raw, conv_w, conv_b, dt_bias, a_log, d_skip, gate_norm, batch, seq)
    w_out = w_out.astype(BF16)
    return _outproj_swa_ssd(x, attn, y, w_out[:q_w], w_out[q_w:], tm=1024, tn=1024)


def _dilated_layer(x, gain, w_qkv, q_norm, k_norm, w_o, batch, seq):
    width = w_o.shape[0]
    qkv = _norm_matmul(x, gain, w_qkv.astype(BF16), BF16, tm=1024, tn=1536, name="norm_qkvproj")
    outs, lses = [], []
    for window, dil in DIL_BRANCHES:
        o, lse = _dilated_branch(qkv, q_norm, k_norm, batch, seq, width, window, dil)
        outs.append(o)
        lses.append(lse)
    return _outproj_dilated(x, outs, lses, w_o.astype(BF16), tm=512, tn=1024)


def kernel(x, norm_mix, norm_ffn, w_up, w_down, ab_w_in, ab_q_norm, ab_k_norm, ab_sinks, ab_conv_w, ab_conv_b,
           ab_dt_bias, ab_a_log, ab_d_skip, ab_gate_norm, ab_w_out, c_w_qkv, c_q_norm, c_k_norm, c_w_o):
    batch, seq, d_model = x.shape
    depth = norm_mix.shape[0]
    max_dil = max(d for _, d in DIL_BRANCHES)
    assert seq % (max_dil * ATTN_BLOCK) == 0 and seq % SSM_CHUNK == 0
    h = x.reshape(batch * seq, d_model).astype(F32)
    for layer in range(depth):
        i = layer // 2
        if layer % 2 == 0:
            h = _swa_ssd_layer(h, norm_mix[layer], ab_w_in[i], ab_q_norm[i], ab_k_norm[i], ab_sinks[i],
                               ab_conv_w[i], ab_conv_b[i], ab_dt_bias[i], ab_a_log[i], ab_d_skip[i],
                               ab_gate_norm[i], ab_w_out[i], batch, seq)
        else:
            h = _dilated_layer(h, norm_mix[layer], c_w_qkv[i], c_q_norm[i], c_k_norm[i], c_w_o[i], batch, seq)
        h = _mlp(h, norm_ffn[layer], w_up[layer].astype(BF16), w_down[layer].astype(BF16), tm=512, tf=1024)
    return h.reshape(batch, seq, d_model).astype(x.dtype)
```

```python
import functools

import jax
import jax.numpy as jnp
from jax import lax
from jax.experimental import pallas as pl
from jax.experimental.pallas import tpu as pltpu

F32 = jnp.float32
BF16 = jnp.bfloat16

NORM_EPS = 1e-5
HEAD_DIM = 64
ATTN_BLOCK = 128
SWA_KV_HEADS = 2
SWA_WINDOW = 128
SSM_GROUPS = 4
SSM_STATE = 128
SSM_CONV = 4
SSM_CHUNK = 128
DIL_BRANCHES = ((128, 1), (512, 4), (2048, 16))
RESIDUES = 16
SUPERBLOCK = RESIDUES * ATTN_BLOCK
STAGE_PITCH = 20

LANES = 128
MXU_WIDTH = 256
CONV_PAD = 8
VMEM_LIMIT_BYTES = 56 * 1024 * 1024


def _compiler_params(semantics):
    return pltpu.CompilerParams(dimension_semantics=semantics, vmem_limit_bytes=VMEM_LIMIT_BYTES)


def _rms_norm_rows(x, gain):
    ms = jnp.mean(x * x, axis=-1, keepdims=True)
    return x * lax.rsqrt(ms + NORM_EPS) * gain


def _norm_rows_into(h_ref, x_ref, g_ref, interleave, xs_ref):
    if interleave is None:
        h_ref[...] = _rms_norm_rows(x_ref[...], g_ref[...]).astype(BF16)
        return
    n_chunks = xs_ref.shape[0]
    rows = x_ref.shape[0] // interleave
    pitch = xs_ref.shape[1] // rows
    for c in range(n_chunks):
        for k in range(rows):
            xs_ref[c, k * pitch:k * pitch + interleave, :] = x_ref[k * interleave:(k + 1) * interleave,
                                                                   c * LANES:(c + 1) * LANES]
    for r in range(interleave):
        xr = jnp.concatenate([xs_ref[c, pl.ds(r, rows, stride=pitch), :] for c in range(n_chunks)], axis=1)
        h_ref[r * rows:(r + 1) * rows, :] = _rms_norm_rows(xr, g_ref[...]).astype(BF16)


def _norm_matmul_kernel(x_ref, g_ref, w_ref, o_ref, h_ref, xs_ref=None, *, interleave):
    @pl.when(pl.program_id(1) == 0)
    def _():
        _norm_rows_into(h_ref, x_ref, g_ref, interleave, xs_ref)

    acc = jnp.dot(h_ref[...], w_ref[...], preferred_element_type=F32)
    o_ref[...] = acc.astype(o_ref.dtype).reshape(o_ref.shape)


def _norm_matmul_headnorm_kernel(x_ref, g_ref, w_ref, hg_ref, hm_ref, ones_ref, *rest, interleave, n_extra,
                                 norm_tiles):
    rest = list(rest)
    o_ref = rest.pop(0)
    extra_ref = rest.pop(0) if n_extra else None
    h_ref = rest.pop(0)
    xs_ref = rest.pop(0) if interleave is not None else None
    j = pl.program_id(1)

    @pl.when(j == 0)
    def _():
        _norm_rows_into(h_ref, x_ref, g_ref, interleave, xs_ref)

    acc = jnp.dot(h_ref[...], w_ref[...], preferred_element_type=F32)
    n_main = acc.shape[1] - n_extra

    def head_norm_store():
        chunks = []
        for c in range(n_main // MXU_WIDTH):
            cols = slice(c * MXU_WIDTH, (c + 1) * MXU_WIDTH)
            a = acc[:, cols]
            ss = jnp.dot((a * a).astype(BF16), ones_ref[...], preferred_element_type=F32)
            r = lax.rsqrt(ss * (1.0 / HEAD_DIM) + NORM_EPS)
            chunks.append((a * jnp.where(hm_ref[:, cols] > 0.0, r, 1.0) * hg_ref[:, cols]).astype(o_ref.dtype))
        o_ref[...] = jnp.concatenate(chunks, axis=1).reshape(o_ref.shape)

    if norm_tiles is None:
        head_norm_store()
    else:
        pl.when(j < norm_tiles)(head_norm_store)

        @pl.when(j >= norm_tiles)
        def _():
            o_ref[...] = acc.astype(o_ref.dtype).reshape(o_ref.shape)
    if n_extra:
        extra_ref[...] = acc[:, n_main:]


def _proj_scratch(tm, d, interleave):
    scratch = [pltpu.VMEM((tm, d), BF16)]
    if interleave is not None:
        scratch.append(pltpu.VMEM((d // LANES, tm // interleave * STAGE_PITCH, LANES), F32))
    return scratch


def _row_specs(t, tm, tn, interleave):
    if interleave is None:
        return (lambda n: jax.ShapeDtypeStruct((t, n), BF16)), pl.BlockSpec((tm, tn), lambda i, j: (i, j))
    per_super = SUPERBLOCK // tm
    rows = tm // interleave
    shape = lambda n: jax.ShapeDtypeStruct((t // SUPERBLOCK, interleave, SUPERBLOCK // interleave, n), BF16)
    spec = pl.BlockSpec((1, interleave, rows, tn), lambda i, j: (i // per_super, 0, i % per_super, j))
    return shape, spec


def _norm_matmul(x, gain, w, tm, tn, name, interleave=None):
    t, d = x.shape
    n = w.shape[1]
    tm = min(tm, t)
    shape, out_spec = _row_specs(t, tm, tn, interleave)
    out = pl.pallas_call(
        functools.partial(_norm_matmul_kernel, interleave=interleave),
        grid=(t // tm, n // tn),
        in_specs=[
            pl.BlockSpec((tm, d), lambda i, j: (i, 0)),
            pl.BlockSpec((1, d), lambda i, j: (0, 0)),
            pl.BlockSpec((d, tn), lambda i, j: (0, j)),
        ],
        out_specs=out_spec,
        out_shape=shape(n),
        scratch_shapes=_proj_scratch(tm, d, interleave),
        compiler_params=_compiler_params(("parallel", "arbitrary")),
        name=name,
    )(x, gain.reshape(1, d).astype(F32), w)
    return out.reshape(t, n)


def _head_ones():
    r = lax.broadcasted_iota(jnp.int32, (MXU_WIDTH, MXU_WIDTH), 0)
    c = lax.broadcasted_iota(jnp.int32, (MXU_WIDTH, MXU_WIDTH), 1)
    return ((r // HEAD_DIM) == (c // HEAD_DIM)).astype(BF16)


def _norm_matmul_headnorm(x, gain, w, head_gain, head_mask, tm, tn, name, interleave=None, n_extra=0,
                          norm_tiles=None):
    t, d = x.shape
    n_main = w.shape[1] - n_extra
    tm = min(tm, t)
    assert n_extra == 0 or tn == n_main
    shape, out_spec = _row_specs(t, tm, tn, interleave)
    out_shapes = [shape(n_main)]
    out_specs = [out_spec]
    if n_extra:
        out_shapes.append(jax.ShapeDtypeStruct((t, n_extra), F32))
        out_specs.append(pl.BlockSpec((tm, n_extra), lambda i, j: (i, 0)))
    outs = pl.pallas_call(
        functools.partial(_norm_matmul_headnorm_kernel, interleave=interleave, n_extra=n_extra,
                          norm_tiles=norm_tiles),
        grid=(t // tm, n_main // tn),
        in_specs=[
            pl.BlockSpec((tm, d), lambda i, j: (i, 0)),
            pl.BlockSpec((1, d), lambda i, j: (0, 0)),
            pl.BlockSpec((d, tn + n_extra), lambda i, j: (0, j)),
            pl.BlockSpec((1, tn), lambda i, j: (0, j)),
            pl.BlockSpec((1, tn), lambda i, j: (0, j)),
            pl.BlockSpec((MXU_WIDTH, MXU_WIDTH), lambda i, j: (0, 0)),
        ],
        out_specs=out_specs,
        out_shape=out_shapes,
        scratch_shapes=_proj_scratch(tm, d, interleave),
        compiler_params=_compiler_params(("parallel", "arbitrary")),
        name=name,
    )(x, gain.reshape(1, d).astype(F32), w, head_gain.reshape(1, n_main).astype(F32),
      head_mask.reshape(1, n_main).astype(F32), _head_ones())
    if n_extra:
        return outs[0].reshape(t, n_main), outs[1]
    return outs[0].reshape(t, n_main)


def _lane_lo():
    return lax.broadcasted_iota(jnp.int32, (1, LANES), 1) < HEAD_DIM


def _attn_tiles(qs, ks, vs, masks, lane_lo, sinks=None, normalize=False):
    results = []
    for i, (q, k, v, mask) in enumerate(zip(qs, ks, vs, masks)):
        zq = jnp.zeros_like(q)
        q_heads = jnp.concatenate([jnp.where(lane_lo, q, zq), jnp.where(lane_lo, zq, q)], axis=0)
        s2 = lax.dot_general(q_heads, k, (((1,), (1,)), ((), ())), preferred_element_type=F32)
        ps, ms, ls = [], [], []
        for idx in range(2):
            s = jnp.where(mask, s2[idx * ATTN_BLOCK:(idx + 1) * ATTN_BLOCK], -jnp.inf)
            m = jnp.max(s, axis=-1, keepdims=True)
            if sinks is not None:
                m = jnp.maximum(m, sinks[i][idx])
            p = jnp.exp(s - m)
            l = jnp.sum(p, axis=-1, keepdims=True)
            if sinks is not None:
                l = l + jnp.exp(sinks[i][idx] - m)
            ps.append(p.astype(BF16))
            ms.append(m)
            ls.append(l)
        zv = jnp.zeros_like(v)
        v_heads = jnp.concatenate([jnp.where(lane_lo, v, zv), jnp.where(lane_lo, zv, v)], axis=0)
        o = jnp.dot(jnp.concatenate(ps, axis=1), v_heads, preferred_element_type=F32)
        if normalize:
            o = o * jnp.where(lane_lo, 1.0 / ls[0], 1.0 / ls[1])
        results.append((o, ms, ls))
    return results


def _dist_mask(q_pos, k_pos, max_dist, key_exists=None):
    dist = q_pos - k_pos
    mask = (dist >= 0) & (dist <= max_dist)
    if key_exists is not None:
        mask = mask & key_exists
    return mask


def _iota(shape, dim):
    return lax.broadcasted_iota(jnp.int32, shape, dim)


def _swa_kernel(sinks_ref, q_ref, kvp_ref, kvc_ref, o_ref):
    first = pl.program_id(1) == 0
    lane_lo = _lane_lo()
    kj = _iota((1, 2 * ATTN_BLOCK), 1)
    mask = _dist_mask(_iota((ATTN_BLOCK, 1), 0) + ATTN_BLOCK, kj, SWA_WINDOW - 1,
                      (kj >= ATTN_BLOCK) | jnp.logical_not(first))
    kv2 = jnp.concatenate([kvp_ref[...], kvc_ref[...]], axis=0).astype(F32)

    def dup_heads(x):
        xr = pltpu.roll(x, HEAD_DIM, axis=1)
        return jnp.where(lane_lo, x, xr).astype(BF16), jnp.where(lane_lo, xr, x).astype(BF16)

    k_dup = dup_heads(kv2[:, :LANES])
    v_dup = dup_heads(kv2[:, LANES:])
    n_pairs = q_ref.shape[1] // LANES
    pairs_per_kv = n_pairs // SWA_KV_HEADS
    qs, ks, vs, masks, sinks = [], [], [], [], []
    for j in range(n_pairs):
        kh = j // pairs_per_kv
        qs.append(q_ref[:, j * LANES:(j + 1) * LANES])
        ks.append(k_dup[kh])
        vs.append(v_dup[kh])
        masks.append(mask)
        sinks.append((sinks_ref[2 * j], sinks_ref[2 * j + 1]))
    for j, (o, _, _) in enumerate(_attn_tiles(qs, ks, vs, masks, lane_lo, sinks, normalize=True)):
        o_ref[:, j * LANES:(j + 1) * LANES] = o.astype(o_ref.dtype)


def _swa_attention(qkv, sinks, batch, seq, q_width):
    t = qkv.shape[0]
    nb = seq // ATTN_BLOCK
    kv_block = q_width // (2 * LANES)
    return pl.pallas_call(
        _swa_kernel,
        grid=(batch, nb),
        in_specs=[
            pl.BlockSpec(memory_space=pltpu.SMEM),
            pl.BlockSpec((ATTN_BLOCK, q_width), lambda b, n: (b * nb + n, 0)),
            pl.BlockSpec((ATTN_BLOCK, 2 * LANES), lambda b, n: (b * nb + jnp.maximum(n - 1, 0), kv_block)),
            pl.BlockSpec((ATTN_BLOCK, 2 * LANES), lambda b, n: (b * nb + n, kv_block)),
        ],
        out_specs=pl.BlockSpec((ATTN_BLOCK, q_width), lambda b, n: (b * nb + n, 0)),
        out_shape=jax.ShapeDtypeStruct((t, q_width), BF16),
        compiler_params=_compiler_params(("parallel", "arbitrary")),
        name="swa_attention",
    )(sinks.astype(F32), qkv, qkv, qkv)


def _write_stats(acc, lane, pair, ms, ls):
    odd = (lane % 2) == 1
    stat = jnp.where(lane < HEAD_DIM, jnp.where(odd, ms[1], ms[0]), jnp.where(odd, ls[1], ls[0]))
    return jnp.where((lane % HEAD_DIM) // 2 == pair, stat, acc)


def _dil16_kernel(q_ref, kp_ref, kc_ref, vp_ref, vc_ref, o_ref, lse_ref):
    first = pl.program_id(2) == 0
    n_res = q_ref.shape[1]
    n_pairs = q_ref.shape[3] // LANES
    lane = _iota((1, LANES), 1)
    lane_lo = lane < HEAD_DIM
    kj = _iota((1, 2 * ATTN_BLOCK), 1)
    mask = _dist_mask(_iota((ATTN_BLOCK, 1), 0) + ATTN_BLOCK, kj, ATTN_BLOCK,
                      (kj >= ATTN_BLOCK) | jnp.logical_not(first))

    def body(j, lse_accs):
        cols = pl.ds(pl.multiple_of(j * LANES, LANES), LANES)
        qs = [q_ref[0, r, :, cols] for r in range(n_res)]
        ks = [jnp.concatenate([kp_ref[0, r, :, cols], kc_ref[0, r, :, cols]], axis=0) for r in range(n_res)]
        vs = [jnp.concatenate([vp_ref[0, r, :, cols], vc_ref[0, r, :, cols]], axis=0) for r in range(n_res)]
        out = []
        for r, (o, ms, ls) in enumerate(_attn_tiles(qs, ks, vs, [mask] * n_res, lane_lo)):
            o_ref[0, r, :, cols] = o.astype(o_ref.dtype)
            out.append(_write_stats(lse_accs[r], lane, j, ms, ls))
        return tuple(out)

    zeros = tuple(jnp.zeros((ATTN_BLOCK, LANES), F32) for _ in range(n_res))
    for r, acc in enumerate(lax.fori_loop(0, n_pairs, body, zeros, unroll=4)):
        lse_ref[0, r] = acc


def _dil4_kernel(q_ref, kp_ref, kc_ref, vp_ref, vc_ref, o_ref, lse_ref):
    first = pl.program_id(2) == 0
    n_sub = q_ref.shape[1]
    rows = ATTN_BLOCK // n_sub
    n_tiles = q_ref.shape[3] // rows
    n_pairs = q_ref.shape[4] // LANES
    lane = _iota((1, LANES), 1)
    lane_lo = lane < HEAD_DIM
    qi = _iota((ATTN_BLOCK, 1), 0)
    kj = _iota((1, 2 * ATTN_BLOCK), 1)
    q_pos = n_sub * (rows + qi % rows) + qi // rows
    k_pos = n_sub * (kj % (2 * rows)) + kj // (2 * rows)
    mask = _dist_mask(q_pos, k_pos, ATTN_BLOCK)
    mask_first = mask & ((kj % (2 * rows) >= rows) | jnp.logical_not(first))

    def gather(cur_ref, prev_ref, tile, cols):
        if tile == 0:
            parts = [jnp.concatenate([prev_ref[0, a, 0, :, cols], cur_ref[0, a, 0, 0:rows, cols]], axis=0)
                     for a in range(n_sub)]
        else:
            parts = [cur_ref[0, a, 0, (tile - 1) * rows:(tile + 1) * rows, cols] for a in range(n_sub)]
        return jnp.concatenate(parts, axis=0)

    def body(j, lse_accs):
        cols = pl.ds(pl.multiple_of(j * LANES, LANES), LANES)
        qs = [jnp.concatenate([q_ref[0, a, 0, t * rows:(t + 1) * rows, cols] for a in range(n_sub)], axis=0)
              for t in range(n_tiles)]
        ks = [gather(kc_ref, kp_ref, t, cols) for t in range(n_tiles)]
        vs = [gather(vc_ref, vp_ref, t, cols) for t in range(n_tiles)]
        masks = [mask_first] + [mask] * (n_tiles - 1)
        out = []
        for t, (o, ms, ls) in enumerate(_attn_tiles(qs, ks, vs, masks, lane_lo)):
            ob = o.astype(o_ref.dtype)
            for a in range(n_sub):
                o_ref[0, a, 0, t * rows:(t + 1) * rows, cols] = ob[a * rows:(a + 1) * rows, :]
            out.append(_write_stats(lse_accs[t], lane, j, ms, ls))
        return tuple(out)

    zeros = tuple(jnp.zeros((ATTN_BLOCK, LANES), F32) for _ in range(n_tiles))
    for t, acc in enumerate(lax.fori_loop(0, n_pairs, body, zeros, unroll=4)):
        for a in range(n_sub):
            lse_ref[0, a, 0, t * rows:(t + 1) * rows, :] = acc[a * rows:(a + 1) * rows, :]


def _dil1_kernel(q_ref, kp_ref, kc_ref, vp_ref, vc_ref, o_ref, lse_ref):
    first = (pl.program_id(1) == 0) & (pl.program_id(2) == 0)
    n_res = q_ref.shape[1]
    sub = ATTN_BLOCK // n_res
    n_tiles = q_ref.shape[2] // sub
    n_pairs = q_ref.shape[3] // LANES
    prev_rows = kp_ref.shape[2]
    lane = _iota((1, LANES), 1)
    lane_lo = lane < HEAD_DIM
    qi = _iota((ATTN_BLOCK, 1), 0)
    kj = _iota((1, 2 * ATTN_BLOCK), 1)
    q_pos = n_res * (sub + qi % sub) + qi // sub
    k_pos = n_res * (kj % (2 * sub)) + kj // (2 * sub)
    mask = _dist_mask(q_pos, k_pos, ATTN_BLOCK)
    mask_first = mask & ((kj % (2 * sub) >= sub) | jnp.logical_not(first))

    def tile_rows(x, lo, hi):
        return x[:, lo:hi, :].reshape(n_res * (hi - lo), LANES)

    def body(j, lse_accs):
        cols = pl.ds(pl.multiple_of(j * LANES, LANES), LANES)
        q = q_ref[0, :, :, cols].astype(F32)
        qs = [tile_rows(q, t * sub, (t + 1) * sub).astype(BF16) for t in range(n_tiles)]

        def gather(cur_ref, prev_ref):
            cur = cur_ref[0, :, :, cols].astype(F32)
            prev = prev_ref[0, :, :, cols].astype(F32)
            tiles = []
            for t in range(n_tiles):
                if t == 0:
                    x = jnp.concatenate([prev[:, prev_rows - sub:, :], cur[:, 0:sub, :]], axis=1)
                else:
                    x = cur[:, (t - 1) * sub:(t + 1) * sub, :]
                tiles.append(x.reshape(n_res * 2 * sub, LANES).astype(BF16))
            return tiles

        ks = gather(kc_ref, kp_ref)
        vs = gather(vc_ref, vp_ref)
        masks = [mask_first] + [mask] * (n_tiles - 1)
        res = _attn_tiles(qs, ks, vs, masks, lane_lo)
        o_full = jnp.concatenate([o.reshape(n_res, sub, LANES) for o, _, _ in res], axis=1)
        o_ref[0, :, :, cols] = o_full.astype(o_ref.dtype)
        return tuple(_write_stats(lse_accs[t], lane, j, ms, ls) for t, (_, ms, ls) in enumerate(res))

    zeros = tuple(jnp.zeros((ATTN_BLOCK, LANES), F32) for _ in range(n_tiles))
    accs = lax.fori_loop(0, n_pairs, body, zeros, unroll=4)
    lse_ref[0] = jnp.concatenate([acc.reshape(n_res, sub, LANES) for acc in accs], axis=1)


def _dilated_branches(qkv, batch, seq):
    t, w = qkv.shape[0], qkv.shape[1] // 3
    nsb = seq // SUPERBLOCK
    n_super = t // SUPERBLOCK
    per_res = SUPERBLOCK // RESIDUES
    outs, lses = [], []

    def in_specs(block, cur_map, prev_block, prev_map):
        cur = lambda col: pl.BlockSpec(block, lambda *g: cur_map(*g) + (col,))
        prev = lambda col: pl.BlockSpec(prev_block, lambda *g: prev_map(*g) + (col,))
        return [cur(0), prev(1), cur(1), prev(2), cur(2)]

    r16 = 4
    shp = (n_super, RESIDUES, per_res)
    cur_map = lambda b, r, s: (b * nsb + s, r, 0)
    prev_map = lambda b, r, s: (b * nsb + jnp.maximum(s - 1, 0), r, 0)
    qkv4 = qkv.reshape(shp + (3 * w,))
    o16, l16 = pl.pallas_call(
        _dil16_kernel, grid=(batch, RESIDUES // r16, nsb),
        in_specs=in_specs((1, r16, per_res, w), cur_map, (1, r16, per_res, w), prev_map),
        out_specs=[pl.BlockSpec((1, r16, per_res, w), lambda *g: cur_map(*g) + (0,)),
                   pl.BlockSpec((1, r16, per_res, LANES), lambda *g: cur_map(*g) + (0,))],
        out_shape=[jax.ShapeDtypeStruct(shp + (w,), BF16), jax.ShapeDtypeStruct(shp + (LANES,), F32)],
        compiler_params=_compiler_params(("parallel", "parallel", "arbitrary")),
        name="dilated_attention_d16",
    )(qkv4, qkv4, qkv4, qkv4, qkv4)

    shp5 = (n_super, 4, 4, per_res)
    tail = per_res // 4
    cur_map5 = lambda b, r, s: (b * nsb + s, 0, r, 0)
    prev_map5 = lambda b, r, s: (b * nsb + jnp.maximum(s - 1, 0), 0, r, per_res // tail - 1)
    qkv5 = qkv.reshape(shp5 + (3 * w,))
    o4, l4 = pl.pallas_call(
        _dil4_kernel, grid=(batch, 4, nsb),
        in_specs=in_specs((1, 4, 1, per_res, w), cur_map5, (1, 4, 1, tail, w), prev_map5),
        out_specs=[pl.BlockSpec((1, 4, 1, per_res, w), lambda *g: cur_map5(*g) + (0,)),
                   pl.BlockSpec((1, 4, 1, per_res, LANES), lambda *g: cur_map5(*g) + (0,))],
        out_shape=[jax.ShapeDtypeStruct(shp5 + (w,), BF16), jax.ShapeDtypeStruct(shp5 + (LANES,), F32)],
        compiler_params=_compiler_params(("parallel", "parallel", "arbitrary")),
        name="dilated_attention_d4",
    )(qkv5, qkv5, qkv5, qkv5, qkv5)

    run, hist = 32, 16
    runs = per_res // run
    cur_map1 = lambda b, s, c: (b * nsb + s, 0, c)

    def prev_map1(b, s, c):
        g = (b * nsb + s) * (per_res // hist) + c * (run // hist) - 1
        g = jnp.maximum(g, 0)
        return (g // (per_res // hist), 0, g % (per_res // hist))

    o1, l1 = pl.pallas_call(
        _dil1_kernel, grid=(batch, nsb, runs),
        in_specs=in_specs((1, RESIDUES, run, w), cur_map1, (1, RESIDUES, hist, w), prev_map1),
        out_specs=[pl.BlockSpec((1, RESIDUES, run, w), lambda *g: cur_map1(*g) + (0,)),
                   pl.BlockSpec((1, RESIDUES, run, LANES), lambda *g: cur_map1(*g) + (0,))],
        out_shape=[jax.ShapeDtypeStruct(shp + (w,), BF16), jax.ShapeDtypeStruct(shp + (LANES,), F32)],
        compiler_params=_compiler_params(("parallel", "arbitrary", "arbitrary")),
        name="dilated_attention_d1",
    )(qkv4, qkv4, qkv4, qkv4, qkv4)

    for o, l in ((o1, l1), (o4, l4), (o16, l16)):
        outs.append(o.reshape(t, w))
        lses.append(l.reshape(t, LANES))
    return outs, lses


def _softplus(x):
    return jnp.maximum(x, 0.0) + jnp.log1p(jnp.exp(-jnp.abs(x)))


def _silu(x):
    return x * (1.0 / (1.0 + jnp.exp(-x)))


def _expand_heads(v, head_mask, e3):
    n_heads = e3.shape[1] // HEAD_DIM
    v = jnp.where(head_mask, v, 0.0)
    hi = v.astype(BF16).astype(F32)
    r = v - hi
    mid = r.astype(BF16).astype(F32)
    lo = (r - mid).astype(BF16).astype(F32)
    cat = hi + pltpu.roll(mid, n_heads, axis=1) + pltpu.roll(lo, 2 * n_heads, axis=1)
    return jnp.dot(cat.astype(BF16), e3, preferred_element_type=F32)


def _ssd_kernel(xs_ref, bc_ref, z_ref, dt_ref, cw_ref, cb_ref, dtb_ref, a_ref, dskip_ref, gn_ref, tril_ref, e3_ref,
                y_ref, buf_ref, state_ref):
    chunk = SSM_CHUNK
    d_inner = z_ref.shape[1]
    n_heads = d_inner // HEAD_DIM
    group_w = d_inner // SSM_GROUPS
    heads_per_group = n_heads // SSM_GROUPS

    @pl.when(pl.program_id(1) == 0)
    def _():
        buf_ref[0:CONV_PAD, :] = jnp.zeros((CONV_PAD, buf_ref.shape[1]), F32)
        state_ref[...] = jnp.zeros(state_ref.shape, F32)

    buf_ref[CONV_PAD:CONV_PAD + chunk, :d_inner] = xs_ref[...].astype(F32)
    buf_ref[CONV_PAD:CONV_PAD + chunk, d_inner:] = bc_ref[...].astype(F32)
    conv = cb_ref[...]
    for j in range(SSM_CONV):
        conv = conv + cw_ref[j:j + 1, :] * buf_ref[pl.ds(CONV_PAD - (SSM_CONV - 1) + j, chunk), :]
    buf_ref[0:CONV_PAD, :] = buf_ref[chunk:chunk + CONV_PAD, :]
    xbc = _silu(conv)
    xs = xbc[:, :d_inner]

    lane = _iota((1, LANES), 1)
    head_mask = lane < n_heads
    lane_lo = lane < HEAD_DIM
    e3 = e3_ref[...]

    dt = jnp.where(head_mask, _softplus(dt_ref[...] + dtb_ref[...]), 0.0)
    cum = jnp.dot(tril_ref[...], dt * a_ref[...], precision=lax.Precision.HIGHEST,
                  preferred_element_type=F32)
    cum_t = cum.T
    ecum_e = _expand_heads(jnp.exp(cum), head_mask, e3)
    dte_e = _expand_heads(jnp.exp(cum[chunk - 1:chunk, :] - cum), head_mask, e3)
    xdt = xs * _expand_heads(dt, head_mask, e3)
    xend = xdt * dte_e

    causal = _iota((chunk, chunk), 0) >= _iota((chunk, chunk), 1)

    for g in range(SSM_GROUPS):
        gcols = slice(g * group_w, (g + 1) * group_w)
        b_f32 = xbc[:, d_inner + g * SSM_STATE: d_inner + (g + 1) * SSM_STATE]
        c_off = d_inner + SSM_GROUPS * SSM_STATE
        c_bf = xbc[:, c_off + g * SSM_STATE: c_off + (g + 1) * SSM_STATE].astype(BF16)
        cb = lax.dot_general(c_bf, b_f32.astype(BF16), (((1,), (1,)), ((), ())), preferred_element_type=F32)
        st = state_ref[g]
        y_off = jnp.dot(c_bf, st.astype(BF16), preferred_element_type=F32) * ecum_e[:, gcols]
        state_ref[g] = st * ecum_e[chunk - 1:chunk, gcols] + jnp.dot(
            b_f32.T.astype(BF16), xend[:, gcols].astype(BF16), preferred_element_type=F32)

        y_pairs = []
        for pair in range(heads_per_group // 2):
            h0 = g * heads_per_group + 2 * pair
            ms = []
            for h in (h0, h0 + 1):
                diff = cum[:, h:h + 1] - cum_t[h:h + 1, :]
                seg = jnp.exp(jnp.where(causal, diff, -jnp.inf))
                ms.append((cb * seg).astype(BF16))
            xp = xdt[:, h0 * HEAD_DIM: h0 * HEAD_DIM + LANES]
            x_heads = jnp.concatenate([jnp.where(lane_lo, xp, 0.0), jnp.where(lane_lo, 0.0, xp)], axis=0)
            y_pairs.append(jnp.dot(jnp.concatenate(ms, axis=1), x_heads.astype(BF16),
                                   preferred_element_type=F32))
        y = jnp.concatenate(y_pairs, axis=1) + y_off + dskip_ref[:, gcols] * xs[:, gcols]
        gy = y * _silu(z_ref[:, gcols].astype(F32))
        ms_g = jnp.mean(gy * gy, axis=-1, keepdims=True)
        y_ref[:, gcols] = (gy * lax.rsqrt(ms_g + NORM_EPS) * gn_ref[:, gcols]).astype(y_ref.dtype)


def _ssd_mixer(proj, dt_raw, conv_w, conv_b, dt_bias, a_log, d_skip, gate_norm, batch, seq):
    t = proj.shape[0]
    d_inner = gate_norm.shape[0]
    n_heads = d_inner // HEAD_DIM
    conv_dim = conv_w.shape[1]
    bc_w = conv_dim - d_inner
    nc = seq // SSM_CHUNK
    assert d_inner % bc_w == 0

    def pad_heads(v):
        return jnp.zeros((1, LANES), F32).at[0, :n_heads].set(v.astype(F32))

    tril = (_iota((SSM_CHUNK, SSM_CHUNK), 0) >= _iota((SSM_CHUNK, SSM_CHUNK), 1)).astype(F32)
    row = _iota((LANES, d_inner), 0)
    col = _iota((LANES, d_inner), 1)
    e3 = ((row < 3 * n_heads) & ((row % n_heads) == (col // HEAD_DIM))).astype(BF16)
    const = lambda shape: pl.BlockSpec(shape, lambda b, c: (0, 0))
    return pl.pallas_call(
        _ssd_kernel,
        grid=(batch, nc),
        in_specs=[
            pl.BlockSpec((SSM_CHUNK, d_inner), lambda b, c: (b * nc + c, 1)),
            pl.BlockSpec((SSM_CHUNK, bc_w), lambda b, c: (b * nc + c, 2 * d_inner // bc_w)),
            pl.BlockSpec((SSM_CHUNK, d_inner), lambda b, c: (b * nc + c, 0)),
            pl.BlockSpec((SSM_CHUNK, LANES), lambda b, c: (b * nc + c, 0)),
            const((SSM_CONV, conv_dim)), const((1, conv_dim)), const((1, LANES)), const((1, LANES)),
            const((1, d_inner)), const((1, d_inner)), const((SSM_CHUNK, SSM_CHUNK)), const((LANES, d_inner)),
        ],
        out_specs=pl.BlockSpec((SSM_CHUNK, d_inner), lambda b, c: (b * nc + c, 0)),
        out_shape=jax.ShapeDtypeStruct((t, d_inner), BF16),
        scratch_shapes=[pltpu.VMEM((SSM_CHUNK + CONV_PAD, conv_dim), F32),
                        pltpu.VMEM((SSM_GROUPS, SSM_STATE, d_inner // SSM_GROUPS), F32)],
        compiler_params=_compiler_params(("parallel", "arbitrary")),
        name="ssd_mixer",
    )(proj, proj, proj, dt_raw, conv_w.astype(F32), conv_b.reshape(1, conv_dim).astype(F32),
      pad_heads(dt_bias), pad_heads(-jnp.exp(a_log.astype(F32))),
      jnp.repeat(d_skip.astype(F32), HEAD_DIM).reshape(1, d_inner), gate_norm.reshape(1, d_inner).astype(F32),
      tril, e3)


def _outproj_swa_ssd_kernel(x_ref, a_ref, y_ref, wa_ref, wy_ref, o_ref):
    o_ref[...] = (x_ref[...]
                  + jnp.dot(a_ref[...], wa_ref[...], preferred_element_type=F32)
                  + jnp.dot(y_ref[...], wy_ref[...], preferred_element_type=F32))


def _outproj_swa_ssd(x, attn, y, w_attn, w_y, tm, tn):
    t, d = x.shape
    tm = min(tm, t)
    return pl.pallas_call(
        _outproj_swa_ssd_kernel,
        grid=(t // tm, d // tn),
        in_specs=[
            pl.BlockSpec((tm, tn), lambda i, j: (i, j)),
            pl.BlockSpec((tm, attn.shape[1]), lambda i, j: (i, 0)),
            pl.BlockSpec((tm, y.shape[1]), lambda i, j: (i, 0)),
            pl.BlockSpec((attn.shape[1], tn), lambda i, j: (0, j)),
            pl.BlockSpec((y.shape[1], tn), lambda i, j: (0, j)),
        ],
        out_specs=pl.BlockSpec((tm, tn), lambda i, j: (i, j)),
        out_shape=jax.ShapeDtypeStruct((t, d), F32),
        compiler_params=_compiler_params(("parallel", "arbitrary")),
        name="outproj_swa_ssd",
    )(x, attn, y, w_attn, w_y)


def _outproj_dilated_kernel(x_ref, o1_ref, o2_ref, o3_ref, l1_ref, l2_ref, l3_ref, e2_ref, w_ref, out_ref,
                            om_ref, scr_ref):
    @pl.when(pl.program_id(1) == 0)
    def _():
        tm, width = om_ref.shape
        n_heads = width // HEAD_DIM
        n_res = o1_ref.shape[1]
        lane = _iota((1, LANES), 1)
        head_mask = lane < n_heads
        stats = [l_ref[...].reshape(tm, LANES) for l_ref in (l1_ref, l2_ref, l3_ref)]
        m = jnp.maximum(jnp.maximum(stats[0], stats[1]), stats[2])
        es = [jnp.exp(st - m) for st in stats]
        den = sum(e * pltpu.roll(st, HEAD_DIM, axis=1) for e, st in zip(es, stats))
        inv = 1.0 / jnp.where(head_mask, den, 1.0)
        acc = None
        for e, o_ref in zip(es, (o1_ref, o2_ref, o3_ref)):
            w = jnp.where(head_mask, e * inv, 0.0)
            hi = w.astype(BF16).astype(F32)
            lo = (w - hi).astype(BF16).astype(F32)
            cat = (hi + pltpu.roll(lo, n_heads, axis=1)).astype(BF16)
            term = jnp.dot(cat, e2_ref[...], preferred_element_type=F32) * o_ref[...].reshape(tm, width).astype(F32)
            acc = term if acc is None else acc + term
        rows = tm // n_res
        pitch = scr_ref.shape[1] // n_res
        for c in range(width // LANES):
            for r in range(n_res):
                scr_ref[c, r * pitch:r * pitch + rows, :] = acc[r * rows:(r + 1) * rows, c * LANES:(c + 1) * LANES]
        for n in range(rows):
            om_ref[n * n_res:(n + 1) * n_res, :] = jnp.concatenate(
                [scr_ref[c, pl.ds(n, n_res, stride=pitch), :] for c in range(width // LANES)], axis=1).astype(BF16)

    out_ref[...] = x_ref[...] + jnp.dot(om_ref[...], w_ref[...], preferred_element_type=F32)


def _outproj_dilated(x, outs, lses, w_o, tm, tn):
    t, d = x.shape
    width = w_o.shape[0]
    n_heads = width // HEAD_DIM
    tm = min(tm, t)
    rows = tm // RESIDUES
    per_super = SUPERBLOCK // tm
    row = _iota((LANES, width), 0)
    col = _iota((LANES, width), 1)
    e2 = ((row < 2 * n_heads) & ((row % n_heads) == (col // HEAD_DIM))).astype(BF16)
    shp = (t // SUPERBLOCK, RESIDUES, SUPERBLOCK // RESIDUES)
    o_spec = pl.BlockSpec((1, RESIDUES, rows, width), lambda i, j: (i // per_super, 0, i % per_super, 0))
    l_spec = pl.BlockSpec((1, RESIDUES, rows, LANES), lambda i, j: (i // per_super, 0, i % per_super, 0))
    return pl.pallas_call(
        _outproj_dilated_kernel,
        grid=(t // tm, d // tn),
        in_specs=[pl.BlockSpec((tm, tn), lambda i, j: (i, j)), o_spec, o_spec, o_spec, l_spec, l_spec, l_spec,
                  pl.BlockSpec((LANES, width), lambda i, j: (0, 0)),
                  pl.BlockSpec((width, tn), lambda i, j: (0, j))],
        out_specs=pl.BlockSpec((tm, tn), lambda i, j: (i, j)),
        out_shape=jax.ShapeDtypeStruct((t, d), F32),
        scratch_shapes=[pltpu.VMEM((tm, width), BF16),
                        pltpu.VMEM((width // LANES, RESIDUES * (rows + STAGE_PITCH - RESIDUES), LANES), F32)],
        compiler_params=_compiler_params(("parallel", "arbitrary")),
        name="outproj_dilated",
    )(x, *[o.reshape(shp + (width,)) for o in outs], *[l.reshape(shp + (LANES,)) for l in lses], e2, w_o)


def _mlp_kernel(x_ref, g_ref, wu_ref, wd_ref, o_ref, h_ref):
    @pl.when(pl.program_id(1) == 0)
    def _():
        h_ref[...] = _rms_norm_rows(x_ref[...], g_ref[...]).astype(BF16)
        o_ref[...] = x_ref[...]

    u = jnp.dot(h_ref[...], wu_ref[...], preferred_element_type=F32)
    a = jnp.square(jnp.maximum(u, 0.0)).astype(BF16)
    o_ref[...] += jnp.dot(a, wd_ref[...], preferred_element_type=F32)


def _mlp(x, gain, w_up, w_down, tm, tf):
    t, d = x.shape
    ff = w_up.shape[1]
    tm = min(tm, t)
    return pl.pallas_call(
        _mlp_kernel,
        grid=(t // tm, ff // tf),
        in_specs=[
            pl.BlockSpec((tm, d), lambda i, j: (i, 0)),
            pl.BlockSpec((1, d), lambda i, j: (0, 0)),
            pl.BlockSpec((d, tf), lambda i, j: (0, j)),
            pl.BlockSpec((tf, d), lambda i, j: (j, 0)),
        ],
        out_specs=pl.BlockSpec((tm, d), lambda i, j: (i, 0)),
        out_shape=jax.ShapeDtypeStruct((t, d), F32),
        scratch_shapes=[pltpu.VMEM((tm, d), BF16)],
        compiler_params=_compiler_params(("parallel", "arbitrary")),
        name="sqrelu_mlp",
    )(x, gain.reshape(1, d).astype(F32), w_up, w_down)


def _swa_ssd_layer(x, gain, w_in, q_norm, k_norm, sinks, conv_w, conv_b, dt_bias, a_log, d_skip, gate_norm,
                   w_out, batch, seq):
    d_inner = gate_norm.shape[0]
    n_ssm_heads = dt_bias.shape[0]
    n_q_heads = sinks.shape[0]
    q_w = n_q_heads * HEAD_DIM
    kv_w = SWA_KV_HEADS * HEAD_DIM
    conv_dim = conv_w.shape[1]
    splits = (q_w, q_w + kv_w, q_w + 2 * kv_w, q_w + 2 * kv_w + d_inner, q_w + 2 * kv_w + d_inner + conv_dim)
    wq, wk, wv, wz, wxbc, wdt = jnp.split(w_in, splits, axis=1)
    w_qkv_dt = jnp.concatenate([wq, wk, wv, jnp.pad(wdt, ((0, 0), (0, LANES - n_ssm_heads)))], axis=1).astype(BF16)
    w_ssm = jnp.concatenate([wz, wxbc], axis=1).astype(BF16)
    qkv_w = q_w + 2 * kv_w
    head_gain = jnp.concatenate([jnp.tile(q_norm.astype(F32) * HEAD_DIM ** -0.5, n_q_heads),
                                 jnp.tile(k_norm.astype(F32), SWA_KV_HEADS), jnp.ones((kv_w,), F32)])
    head_mask = jnp.concatenate([jnp.ones((q_w + kv_w,), F32), jnp.zeros((kv_w,), F32)])

    qkv, dt_raw = _norm_matmul_headnorm(x, gain, w_qkv_dt, head_gain, head_mask, tm=1024, tn=qkv_w,
                                        name="norm_qkv_dt_proj", n_extra=LANES)
    proj = _norm_matmul(x, gain, w_ssm, tm=1024, tn=w_ssm.shape[1] // 4, name="norm_ssm_proj")
    attn = _swa_attention(qkv, sinks, batch, seq, q_w)
    y = _ssd_mixer(proj, dt_raw, conv_w, conv_b, dt_bias, a_log, d_skip, gate_norm, batch, seq)
    w_out = w_out.astype(BF16)
    return _outproj_swa_ssd(x, attn, y, w_out[:q_w], w_out[q_w:], tm=1024, tn=1024)


def _dilated_layer(x, gain, w_qkv, q_norm, k_norm, w_o, batch, seq):
    width = w_o.shape[0]
    n_heads = width // HEAD_DIM
    w_qkv = w_qkv.astype(BF16)
    head_gain = jnp.concatenate([jnp.tile(q_norm.astype(F32) * HEAD_DIM ** -0.5, n_heads),
                                 jnp.tile(k_norm.astype(F32), n_heads), jnp.ones((width,), F32)])
    head_mask = jnp.concatenate([jnp.ones((2 * width,), F32), jnp.zeros((width,), F32)])
    tn = 1024
    qkv = _norm_matmul_headnorm(x, gain, w_qkv, head_gain, head_mask, tm=1024, tn=tn, name="norm_qkv_proj",
                                interleave=RESIDUES, norm_tiles=2 * width // tn)
    outs, lses = _dilated_branches(qkv, batch, seq)
    return _outproj_dilated(x, outs, lses, w_o.astype(BF16), tm=512, tn=1024)


def kernel(x, norm_mix, norm_ffn, w_up, w_down, ab_w_in, ab_q_norm, ab_k_norm, ab_sinks, ab_conv_w, ab_conv_b,
           ab_dt_bias, ab_a_log, ab_d_skip, ab_gate_norm, ab_w_out, c_w_qkv, c_q_norm, c_k_norm, c_w_o):
    batch, seq, d_model = x.shape
    depth = norm_mix.shape[0]
    assert seq % SUPERBLOCK == 0 and seq % SSM_CHUNK == 0
    h = x.reshape(batch * seq, d_model).astype(F32)
    for layer in range(depth):
        i = layer // 2
        if layer % 2 == 0:
            h = _swa_ssd_layer(h, norm_mix[layer], ab_w_in[i], ab_q_norm[i], ab_k_norm[i], ab_sinks[i],
                               ab_conv_w[i], ab_conv_b[i], ab_dt_bias[i], ab_a_log[i], ab_d_skip[i],
                               ab_gate_norm[i], ab_w_out[i], batch, seq)
        else:
            h = _dilated_layer(h, norm_mix[layer], c_w_qkv[i], c_q_norm[i], c_k_norm[i], c_w_o[i], batch, seq)
        h = _mlp(h, norm_ffn[layer], w_up[layer].astype(BF16), w_down[layer].astype(BF16), tm=512, tf=1024)
    return h.reshape(batch, seq, d_model).astype(x.dtype)
```

```python
import functools
import math

import jax
import jax.numpy as jnp
from jax import lax
from jax.experimental import pallas as pl
from jax.experimental.pallas import tpu as pltpu

F32 = jnp.float32
BF16 = jnp.bfloat16

NORM_EPS = 1e-5
HEAD_DIM = 64
ATTN_BLOCK = 128
SWA_KV_HEADS = 2
SWA_WINDOW = 128
SSM_GROUPS = 4
SSM_STATE = 128
SSM_CONV = 4
SSM_CHUNK = 128
DIL_BRANCHES = ((128, 1), (512, 4), (2048, 16))
RESIDUES = 16
SUPERBLOCK = RESIDUES * ATTN_BLOCK
STAGE_PITCH = 20

LOG2E = math.log2(math.e)
Q_SCALE = HEAD_DIM ** -0.5 * LOG2E

LANES = 128
MXU_WIDTH = 256
CONV_HIST = 16
VMEM_LIMIT_BYTES = 56 * 1024 * 1024


def _compiler_params(semantics):
    return pltpu.CompilerParams(dimension_semantics=semantics, vmem_limit_bytes=VMEM_LIMIT_BYTES)


def _rms_norm_rows(x, gain):
    ms = jnp.mean(x * x, axis=-1, keepdims=True)
    return x * lax.rsqrt(ms + NORM_EPS) * gain


def _norm_rows_into(h_ref, x_ref, g_ref, interleave, xs_ref):
    if interleave is None:
        h_ref[...] = _rms_norm_rows(x_ref[...], g_ref[...]).astype(BF16)
        return
    n_chunks = xs_ref.shape[0]
    rows = x_ref.shape[0] // interleave
    pitch = xs_ref.shape[1] // rows
    for c in range(n_chunks):
        for k in range(rows):
            xs_ref[c, k * pitch:k * pitch + interleave, :] = x_ref[k * interleave:(k + 1) * interleave,
                                                                   c * LANES:(c + 1) * LANES]
    for r in range(interleave):
        xr = jnp.concatenate([xs_ref[c, pl.ds(r, rows, stride=pitch), :] for c in range(n_chunks)], axis=1)
        h_ref[r * rows:(r + 1) * rows, :] = _rms_norm_rows(xr, g_ref[...]).astype(BF16)


def _norm_matmul_kernel(x_ref, g_ref, w_ref, o_ref, h_ref, xs_ref=None, *, interleave):
    @pl.when(pl.program_id(1) == 0)
    def _():
        _norm_rows_into(h_ref, x_ref, g_ref, interleave, xs_ref)

    acc = jnp.dot(h_ref[...], w_ref[...], preferred_element_type=F32)
    o_ref[...] = acc.astype(o_ref.dtype).reshape(o_ref.shape)


def _norm_matmul_headnorm_kernel(x_ref, g_ref, w_ref, hg_ref, hm_ref, ones_ref, *rest, interleave, n_extra,
                                 norm_tiles):
    rest = list(rest)
    o_ref = rest.pop(0)
    extra_ref = rest.pop(0) if n_extra else None
    h_ref = rest.pop(0)
    xs_ref = rest.pop(0) if interleave is not None else None
    j = pl.program_id(1)

    @pl.when(j == 0)
    def _():
        _norm_rows_into(h_ref, x_ref, g_ref, interleave, xs_ref)

    acc = jnp.dot(h_ref[...], w_ref[...], preferred_element_type=F32)
    n_main = acc.shape[1] - n_extra

    def head_norm_store():
        chunks = []
        for c in range(n_main // MXU_WIDTH):
            cols = slice(c * MXU_WIDTH, (c + 1) * MXU_WIDTH)
            a = acc[:, cols]
            ss = jnp.dot((a * a).astype(BF16), ones_ref[...], preferred_element_type=F32)
            r = lax.rsqrt(ss * (1.0 / HEAD_DIM) + NORM_EPS)
            chunks.append((a * jnp.where(hm_ref[:, cols] > 0.0, r, 1.0) * hg_ref[:, cols]).astype(o_ref.dtype))
        o_ref[...] = jnp.concatenate(chunks, axis=1).reshape(o_ref.shape)

    if norm_tiles is None:
        head_norm_store()
    else:
        pl.when(j < norm_tiles)(head_norm_store)

        @pl.when(j >= norm_tiles)
        def _():
            o_ref[...] = acc.astype(o_ref.dtype).reshape(o_ref.shape)
    if n_extra:
        extra_ref[...] = acc[:, n_main:]


def _proj_scratch(tm, d, interleave):
    scratch = [pltpu.VMEM((tm, d), BF16)]
    if interleave is not None:
        scratch.append(pltpu.VMEM((d // LANES, tm // interleave * STAGE_PITCH, LANES), F32))
    return scratch


def _row_specs(t, tm, tn, interleave):
    if interleave is None:
        return (lambda n: jax.ShapeDtypeStruct((t, n), BF16)), pl.BlockSpec((tm, tn), lambda i, j: (i, j))
    per_super = SUPERBLOCK // tm
    rows = tm // interleave
    shape = lambda n: jax.ShapeDtypeStruct((t // SUPERBLOCK, interleave, SUPERBLOCK // interleave, n), BF16)
    spec = pl.BlockSpec((1, interleave, rows, tn), lambda i, j: (i // per_super, 0, i % per_super, j))
    return shape, spec


def _norm_matmul(x, gain, w, tm, tn, name, interleave=None):
    t, d = x.shape
    n = w.shape[1]
    tm = min(tm, t)
    shape, out_spec = _row_specs(t, tm, tn, interleave)
    out = pl.pallas_call(
        functools.partial(_norm_matmul_kernel, interleave=interleave),
        grid=(t // tm, n // tn),
        in_specs=[
            pl.BlockSpec((tm, d), lambda i, j: (i, 0)),
            pl.BlockSpec((1, d), lambda i, j: (0, 0)),
            pl.BlockSpec((d, tn), lambda i, j: (0, j)),
        ],
        out_specs=out_spec,
        out_shape=shape(n),
        scratch_shapes=_proj_scratch(tm, d, interleave),
        compiler_params=_compiler_params(("parallel", "arbitrary")),
        name=name,
    )(x, gain.reshape(1, d).astype(F32), w)
    return out.reshape(t, n)


def _head_ones():
    r = lax.broadcasted_iota(jnp.int32, (MXU_WIDTH, MXU_WIDTH), 0)
    c = lax.broadcasted_iota(jnp.int32, (MXU_WIDTH, MXU_WIDTH), 1)
    return ((r // HEAD_DIM) == (c // HEAD_DIM)).astype(BF16)


def _norm_matmul_headnorm(x, gain, w, head_gain, head_mask, tm, tn, name, interleave=None, n_extra=0,
                          norm_tiles=None):
    t, d = x.shape
    n_main = w.shape[1] - n_extra
    tm = min(tm, t)
    assert n_extra == 0 or tn == n_main
    shape, out_spec = _row_specs(t, tm, tn, interleave)
    out_shapes = [shape(n_main)]
    out_specs = [out_spec]
    if n_extra:
        out_shapes.append(jax.ShapeDtypeStruct((t, n_extra), F32))
        out_specs.append(pl.BlockSpec((tm, n_extra), lambda i, j: (i, 0)))
    outs = pl.pallas_call(
        functools.partial(_norm_matmul_headnorm_kernel, interleave=interleave, n_extra=n_extra,
                          norm_tiles=norm_tiles),
        grid=(t // tm, n_main // tn),
        in_specs=[
            pl.BlockSpec((tm, d), lambda i, j: (i, 0)),
            pl.BlockSpec((1, d), lambda i, j: (0, 0)),
            pl.BlockSpec((d, tn + n_extra), lambda i, j: (0, j)),
            pl.BlockSpec((1, tn), lambda i, j: (0, j)),
            pl.BlockSpec((1, tn), lambda i, j: (0, j)),
            pl.BlockSpec((MXU_WIDTH, MXU_WIDTH), lambda i, j: (0, 0)),
        ],
        out_specs=out_specs,
        out_shape=out_shapes,
        scratch_shapes=_proj_scratch(tm, d, interleave),
        compiler_params=_compiler_params(("parallel", "arbitrary")),
        name=name,
    )(x, gain.reshape(1, d).astype(F32), w, head_gain.reshape(1, n_main).astype(F32),
      head_mask.reshape(1, n_main).astype(F32), _head_ones())
    if n_extra:
        return outs[0].reshape(t, n_main), outs[1]
    return outs[0].reshape(t, n_main)


def _lane_lo():
    return lax.broadcasted_iota(jnp.int32, (1, LANES), 1) < HEAD_DIM


def _attn_tiles(qs, ks, vs, masks, lane_lo, sinks=None, normalize=False):
    sel_rows = _iota((4 * ATTN_BLOCK, LANES), 0) < 2 * ATTN_BLOCK
    row_sum = (sel_rows == (_iota((4 * ATTN_BLOCK, LANES), 1) < HEAD_DIM)).astype(BF16)
    results = []
    for i, (q, k, v, mask) in enumerate(zip(qs, ks, vs, masks)):
        zq = jnp.zeros_like(q)
        q_heads = jnp.concatenate([jnp.where(lane_lo, q, zq), jnp.where(lane_lo, zq, q)], axis=0)
        s2 = lax.dot_general(q_heads, k, (((1,), (1,)), ((), ())), preferred_element_type=F32)
        ps, ms = [], []
        for idx in range(2):
            s = jnp.where(mask, s2[idx * ATTN_BLOCK:(idx + 1) * ATTN_BLOCK], -jnp.inf)
            m = jnp.max(s, axis=-1, keepdims=True)
            if sinks is not None:
                m = jnp.maximum(m, sinks[i][idx])
            ps.append(jnp.exp2(s - m).astype(BF16))
            ms.append(m)
        zv = jnp.zeros_like(v)
        v_heads = jnp.concatenate([jnp.where(lane_lo, v, zv), jnp.where(lane_lo, zv, v)], axis=0)
        res = jnp.dot(jnp.concatenate(ps, axis=1), jnp.concatenate([v_heads, row_sum], axis=1),
                      preferred_element_type=F32)
        o, l = res[:, :LANES], res[:, LANES:]
        if sinks is not None:
            l = l + jnp.where(lane_lo, jnp.exp2(sinks[i][0] - ms[0]), jnp.exp2(sinks[i][1] - ms[1]))
        if normalize:
            o = o * (1.0 / l)
        results.append((o, ms, l))
    return results


def _dist_mask(q_pos, k_pos, max_dist, key_exists=None):
    dist = q_pos - k_pos
    mask = (dist >= 0) & (dist <= max_dist)
    if key_exists is not None:
        mask = mask & key_exists
    return mask


def _iota(shape, dim):
    return lax.broadcasted_iota(jnp.int32, shape, dim)


def _swa_kernel(sinks_ref, q_ref, kvp_ref, kvc_ref, o_ref):
    first = pl.program_id(1) == 0
    lane_lo = _lane_lo()
    kj = _iota((1, 2 * ATTN_BLOCK), 1)
    mask = _dist_mask(_iota((ATTN_BLOCK, 1), 0) + ATTN_BLOCK, kj, SWA_WINDOW - 1,
                      (kj >= ATTN_BLOCK) | jnp.logical_not(first))
    kv2 = jnp.concatenate([kvp_ref[...], kvc_ref[...]], axis=0).astype(F32)

    def dup_heads(x):
        xr = pltpu.roll(x, HEAD_DIM, axis=1)
        return jnp.where(lane_lo, x, xr).astype(BF16), jnp.where(lane_lo, xr, x).astype(BF16)

    k_dup = dup_heads(kv2[:, :LANES])
    v_dup = dup_heads(kv2[:, LANES:])
    n_pairs = q_ref.shape[1] // LANES
    pairs_per_kv = n_pairs // SWA_KV_HEADS
    qs, ks, vs, masks, sinks = [], [], [], [], []
    for j in range(n_pairs):
        kh = j // pairs_per_kv
        qs.append(q_ref[:, j * LANES:(j + 1) * LANES])
        ks.append(k_dup[kh])
        vs.append(v_dup[kh])
        masks.append(mask)
        sinks.append((sinks_ref[2 * j], sinks_ref[2 * j + 1]))
    for j, (o, _, _) in enumerate(_attn_tiles(qs, ks, vs, masks, lane_lo, sinks, normalize=True)):
        o_ref[:, j * LANES:(j + 1) * LANES] = o.astype(o_ref.dtype)


def _swa_attention(qkv, sinks, batch, seq, q_width):
    t = qkv.shape[0]
    nb = seq // ATTN_BLOCK
    kv_block = q_width // (2 * LANES)
    return pl.pallas_call(
        _swa_kernel,
        grid=(batch, nb),
        in_specs=[
            pl.BlockSpec(memory_space=pltpu.SMEM),
            pl.BlockSpec((ATTN_BLOCK, q_width), lambda b, n: (b * nb + n, 0)),
            pl.BlockSpec((ATTN_BLOCK, 2 * LANES), lambda b, n: (b * nb + jnp.maximum(n - 1, 0), kv_block)),
            pl.BlockSpec((ATTN_BLOCK, 2 * LANES), lambda b, n: (b * nb + n, kv_block)),
        ],
        out_specs=pl.BlockSpec((ATTN_BLOCK, q_width), lambda b, n: (b * nb + n, 0)),
        out_shape=jax.ShapeDtypeStruct((t, q_width), BF16),
        compiler_params=_compiler_params(("parallel", "arbitrary")),
        name="swa_attention",
    )(sinks.astype(F32) * LOG2E, qkv, qkv, qkv)


STAT_GROUP = 4


def _write_stats(acc, lane, pair, ms, l):
    lane_lo = lane < HEAD_DIM
    stat = jnp.where(lane % STAT_GROUP < STAT_GROUP // 2, l, jnp.where(lane_lo, ms[0], ms[1]))
    return jnp.where((lane % HEAD_DIM) // STAT_GROUP == pair, stat, acc)


def _dil16_kernel(q_ref, kp_ref, kc_ref, vp_ref, vc_ref, o_ref, lse_ref):
    first = pl.program_id(2) == 0
    n_res = q_ref.shape[1]
    n_pairs = q_ref.shape[3] // LANES
    lane = _iota((1, LANES), 1)
    lane_lo = lane < HEAD_DIM
    kj = _iota((1, 2 * ATTN_BLOCK), 1)
    mask = _dist_mask(_iota((ATTN_BLOCK, 1), 0) + ATTN_BLOCK, kj, ATTN_BLOCK,
                      (kj >= ATTN_BLOCK) | jnp.logical_not(first))

    def body(j, lse_accs):
        cols = pl.ds(pl.multiple_of(j * LANES, LANES), LANES)
        qs = [q_ref[0, r, :, cols] for r in range(n_res)]
        ks = [jnp.concatenate([kp_ref[0, r, :, cols], kc_ref[0, r, :, cols]], axis=0) for r in range(n_res)]
        vs = [jnp.concatenate([vp_ref[0, r, :, cols], vc_ref[0, r, :, cols]], axis=0) for r in range(n_res)]
        out = []
        for r, (o, ms, ls) in enumerate(_attn_tiles(qs, ks, vs, [mask] * n_res, lane_lo)):
            o_ref[0, r, :, cols] = o.astype(o_ref.dtype)
            out.append(_write_stats(lse_accs[r], lane, j, ms, ls))
        return tuple(out)

    zeros = tuple(jnp.zeros((ATTN_BLOCK, LANES), F32) for _ in range(n_res))
    for r, acc in enumerate(lax.fori_loop(0, n_pairs, body, zeros, unroll=4)):
        lse_ref[0, r] = acc


def _dil4_kernel(q_ref, kp_ref, kc_ref, vp_ref, vc_ref, o_ref, lse_ref):
    first = pl.program_id(2) == 0
    n_sub = q_ref.shape[1]
    rows = ATTN_BLOCK // n_sub
    n_tiles = q_ref.shape[3] // rows
    n_pairs = q_ref.shape[4] // LANES
    lane = _iota((1, LANES), 1)
    lane_lo = lane < HEAD_DIM
    qi = _iota((ATTN_BLOCK, 1), 0)
    kj = _iota((1, 2 * ATTN_BLOCK), 1)
    q_pos = n_sub * (rows + qi % rows) + qi // rows
    k_pos = n_sub * (kj % (2 * rows)) + kj // (2 * rows)
    mask = _dist_mask(q_pos, k_pos, ATTN_BLOCK)
    mask_first = mask & ((kj % (2 * rows) >= rows) | jnp.logical_not(first))

    def gather(cur_ref, prev_ref, tile, cols):
        if tile == 0:
            parts = [jnp.concatenate([prev_ref[0, a, 0, :, cols], cur_ref[0, a, 0, 0:rows, cols]], axis=0)
                     for a in range(n_sub)]
        else:
            parts = [cur_ref[0, a, 0, (tile - 1) * rows:(tile + 1) * rows, cols] for a in range(n_sub)]
        return jnp.concatenate(parts, axis=0)

    def body(j, lse_accs):
        cols = pl.ds(pl.multiple_of(j * LANES, LANES), LANES)
        qs = [jnp.concatenate([q_ref[0, a, 0, t * rows:(t + 1) * rows, cols] for a in range(n_sub)], axis=0)
              for t in range(n_tiles)]
        ks = [gather(kc_ref, kp_ref, t, cols) for t in range(n_tiles)]
        vs = [gather(vc_ref, vp_ref, t, cols) for t in range(n_tiles)]
        masks = [mask_first] + [mask] * (n_tiles - 1)
        out = []
        for t, (o, ms, ls) in enumerate(_attn_tiles(qs, ks, vs, masks, lane_lo)):
            ob = o.astype(o_ref.dtype)
            for a in range(n_sub):
                o_ref[0, a, 0, t * rows:(t + 1) * rows, cols] = ob[a * rows:(a + 1) * rows, :]
            out.append(_write_stats(lse_accs[t], lane, j, ms, ls))
        return tuple(out)

    zeros = tuple(jnp.zeros((ATTN_BLOCK, LANES), F32) for _ in range(n_tiles))
    for t, acc in enumerate(lax.fori_loop(0, n_pairs, body, zeros, unroll=4)):
        for a in range(n_sub):
            lse_ref[0, a, 0, t * rows:(t + 1) * rows, :] = acc[a * rows:(a + 1) * rows, :]


def _dil1_kernel(q_ref, kp_ref, kc_ref, vp_ref, vc_ref, o_ref, lse_ref):
    first = (pl.program_id(1) == 0) & (pl.program_id(2) == 0)
    n_res = q_ref.shape[1]
    sub = ATTN_BLOCK // n_res
    n_tiles = q_ref.shape[2] // sub
    n_pairs = q_ref.shape[3] // LANES
    prev_rows = kp_ref.shape[2]
    lane = _iota((1, LANES), 1)
    lane_lo = lane < HEAD_DIM
    qi = _iota((ATTN_BLOCK, 1), 0)
    kj = _iota((1, 2 * ATTN_BLOCK), 1)
    q_pos = n_res * (sub + qi % sub) + qi // sub
    k_pos = n_res * (kj % (2 * sub)) + kj // (2 * sub)
    mask = _dist_mask(q_pos, k_pos, ATTN_BLOCK)
    mask_first = mask & ((kj % (2 * sub) >= sub) | jnp.logical_not(first))

    def tile_rows(x, lo, hi):
        return x[:, lo:hi, :].reshape(n_res * (hi - lo), LANES)

    def body(j, lse_accs):
        cols = pl.ds(pl.multiple_of(j * LANES, LANES), LANES)
        q = q_ref[0, :, :, cols].astype(F32)
        qs = [tile_rows(q, t * sub, (t + 1) * sub).astype(BF16) for t in range(n_tiles)]

        def gather(cur_ref, prev_ref):
            cur = cur_ref[0, :, :, cols].astype(F32)
            prev = prev_ref[0, :, :, cols].astype(F32)
            tiles = []
            for t in range(n_tiles):
                if t == 0:
                    x = jnp.concatenate([prev[:, prev_rows - sub:, :], cur[:, 0:sub, :]], axis=1)
                else:
                    x = cur[:, (t - 1) * sub:(t + 1) * sub, :]
                tiles.append(x.reshape(n_res * 2 * sub, LANES).astype(BF16))
            return tiles

        ks = gather(kc_ref, kp_ref)
        vs = gather(vc_ref, vp_ref)
        masks = [mask_first] + [mask] * (n_tiles - 1)
        res = _attn_tiles(qs, ks, vs, masks, lane_lo)
        o_full = jnp.concatenate([o.reshape(n_res, sub, LANES) for o, _, _ in res], axis=1)
        o_ref[0, :, :, cols] = o_full.astype(o_ref.dtype)
        return tuple(_write_stats(lse_accs[t], lane, j, ms, ls) for t, (_, ms, ls) in enumerate(res))

    zeros = tuple(jnp.zeros((ATTN_BLOCK, LANES), F32) for _ in range(n_tiles))
    accs = lax.fori_loop(0, n_pairs, body, zeros, unroll=4)
    lse_ref[0] = jnp.concatenate([acc.reshape(n_res, sub, LANES) for acc in accs], axis=1)


def _dilated_branches(qkv, batch, seq):
    t, w = qkv.shape[0], qkv.shape[1] // 3
    nsb = seq // SUPERBLOCK
    n_super = t // SUPERBLOCK
    per_res = SUPERBLOCK // RESIDUES
    outs, lses = [], []

    def in_specs(block, cur_map, prev_block, prev_map):
        cur = lambda col: pl.BlockSpec(block, lambda *g: cur_map(*g) + (col,))
        prev = lambda col: pl.BlockSpec(prev_block, lambda *g: prev_map(*g) + (col,))
        return [cur(0), prev(1), cur(1), prev(2), cur(2)]

    r16 = 4
    shp = (n_super, RESIDUES, per_res)
    cur_map = lambda b, r, s: (b * nsb + s, r, 0)
    prev_map = lambda b, r, s: (b * nsb + jnp.maximum(s - 1, 0), r, 0)
    qkv4 = qkv.reshape(shp + (3 * w,))
    o16, l16 = pl.pallas_call(
        _dil16_kernel, grid=(batch, RESIDUES // r16, nsb),
        in_specs=in_specs((1, r16, per_res, w), cur_map, (1, r16, per_res, w), prev_map),
        out_specs=[pl.BlockSpec((1, r16, per_res, w), lambda *g: cur_map(*g) + (0,)),
                   pl.BlockSpec((1, r16, per_res, LANES), lambda *g: cur_map(*g) + (0,))],
        out_shape=[jax.ShapeDtypeStruct(shp + (w,), BF16), jax.ShapeDtypeStruct(shp + (LANES,), F32)],
        compiler_params=_compiler_params(("parallel", "parallel", "arbitrary")),
        name="dilated_attention_d16",
    )(qkv4, qkv4, qkv4, qkv4, qkv4)

    shp5 = (n_super, 4, 4, per_res)
    tail = per_res // 4
    cur_map5 = lambda b, r, s: (b * nsb + s, 0, r, 0)
    prev_map5 = lambda b, r, s: (b * nsb + jnp.maximum(s - 1, 0), 0, r, per_res // tail - 1)
    qkv5 = qkv.reshape(shp5 + (3 * w,))
    o4, l4 = pl.pallas_call(
        _dil4_kernel, grid=(batch, 4, nsb),
        in_specs=in_specs((1, 4, 1, per_res, w), cur_map5, (1, 4, 1, tail, w), prev_map5),
        out_specs=[pl.BlockSpec((1, 4, 1, per_res, w), lambda *g: cur_map5(*g) + (0,)),
                   pl.BlockSpec((1, 4, 1, per_res, LANES), lambda *g: cur_map5(*g) + (0,))],
        out_shape=[jax.ShapeDtypeStruct(shp5 + (w,), BF16), jax.ShapeDtypeStruct(shp5 + (LANES,), F32)],
        compiler_params=_compiler_params(("parallel", "parallel", "arbitrary")),
        name="dilated_attention_d4",
    )(qkv5, qkv5, qkv5, qkv5, qkv5)

    run, hist = 32, 16
    runs = per_res // run
    cur_map1 = lambda b, s, c: (b * nsb + s, 0, c)

    def prev_map1(b, s, c):
        g = (b * nsb + s) * (per_res // hist) + c * (run // hist) - 1
        g = jnp.maximum(g, 0)
        return (g // (per_res // hist), 0, g % (per_res // hist))

    o1, l1 = pl.pallas_call(
        _dil1_kernel, grid=(batch, nsb, runs),
        in_specs=in_specs((1, RESIDUES, run, w), cur_map1, (1, RESIDUES, hist, w), prev_map1),
        out_specs=[pl.BlockSpec((1, RESIDUES, run, w), lambda *g: cur_map1(*g) + (0,)),
                   pl.BlockSpec((1, RESIDUES, run, LANES), lambda *g: cur_map1(*g) + (0,))],
        out_shape=[jax.ShapeDtypeStruct(shp + (w,), BF16), jax.ShapeDtypeStruct(shp + (LANES,), F32)],
        compiler_params=_compiler_params(("parallel", "arbitrary", "arbitrary")),
        name="dilated_attention_d1",
    )(qkv4, qkv4, qkv4, qkv4, qkv4)

    for o, l in ((o1, l1), (o4, l4), (o16, l16)):
        outs.append(o.reshape(t, w))
        lses.append(l.reshape(t, LANES))
    return outs, lses


def _softplus(x):
    return jnp.maximum(x, 0.0) + jnp.log1p(jnp.exp(-jnp.abs(x)))


def _silu(x):
    h = 0.5 * x
    return h * jnp.tanh(h) + h


def _expand_heads(v, head_mask, e3):
    n_heads = e3.shape[1] // HEAD_DIM
    v = jnp.where(head_mask, v, 0.0)
    hi = v.astype(BF16).astype(F32)
    r = v - hi
    mid = r.astype(BF16).astype(F32)
    lo = (r - mid).astype(BF16).astype(F32)
    cat = hi + pltpu.roll(mid, n_heads, axis=1) + pltpu.roll(lo, 2 * n_heads, axis=1)
    return jnp.dot(cat.astype(BF16), e3, preferred_element_type=F32)


def _ssd_kernel(xs_ref, bc_ref, z_ref, dt_ref, cw_ref, cb_ref, dtb_ref, a_ref, dskip_ref, gn_ref, tril_ref, e3_ref,
                shift_ref, y_ref, hist_ref, state_ref):
    chunk = SSM_CHUNK
    d_inner = z_ref.shape[1]
    n_heads = d_inner // HEAD_DIM
    group_w = d_inner // SSM_GROUPS
    heads_per_group = n_heads // SSM_GROUPS

    @pl.when(pl.program_id(1) == 0)
    def _():
        hist_ref[...] = jnp.zeros(hist_ref.shape, hist_ref.dtype)
        state_ref[...] = jnp.zeros(state_ref.shape, F32)

    parts = []
    for ref, cols in ((xs_ref, slice(0, d_inner)), (bc_ref, slice(d_inner, hist_ref.shape[1]))):
        cur = ref[...]
        ext = jnp.concatenate([hist_ref[:, cols], cur], axis=0)
        shifted = jnp.dot(shift_ref[...], ext, preferred_element_type=F32)
        conv = cb_ref[:, cols] + cw_ref[SSM_CONV - 1:SSM_CONV, cols] * cur.astype(F32)
        for j in range(SSM_CONV - 1):
            conv = conv + cw_ref[j:j + 1, cols] * shifted[j * chunk:(j + 1) * chunk]
        hist_ref[:, cols] = cur[chunk - CONV_HIST:, :]
        parts.append(_silu(conv))
    xs = parts[0]
    xbc_bc = parts[1]

    lane = _iota((1, LANES), 1)
    head_mask = lane < n_heads
    lane_lo = lane < HEAD_DIM
    e3 = e3_ref[...]

    dt = jnp.where(head_mask, _softplus(dt_ref[...] + dtb_ref[...]), 0.0)
    cum = jnp.dot(tril_ref[...], dt * a_ref[...], precision=lax.Precision.HIGHEST,
                  preferred_element_type=F32)
    cum_t = cum.T
    ecum_e = _expand_heads(jnp.exp(cum), head_mask, e3)
    dte_e = _expand_heads(jnp.exp(cum[chunk - 1:chunk, :] - cum), head_mask, e3)
    xdt = xs * _expand_heads(dt, head_mask, e3)
    xend = xdt * dte_e

    causal = _iota((chunk, chunk), 0) >= _iota((chunk, chunk), 1)

    for g in range(SSM_GROUPS):
        gcols = slice(g * group_w, (g + 1) * group_w)
        b_f32 = xbc_bc[:, g * SSM_STATE: (g + 1) * SSM_STATE]
        c_off = SSM_GROUPS * SSM_STATE
        c_bf = xbc_bc[:, c_off + g * SSM_STATE: c_off + (g + 1) * SSM_STATE].astype(BF16)
        cb = lax.dot_general(c_bf, b_f32.astype(BF16), (((1,), (1,)), ((), ())), preferred_element_type=F32)
        st = state_ref[g]
        y_off = jnp.dot(c_bf, st.astype(BF16), preferred_element_type=F32) * ecum_e[:, gcols]
        state_ref[g] = st * ecum_e[chunk - 1:chunk, gcols] + jnp.dot(
            b_f32.T.astype(BF16), xend[:, gcols].astype(BF16), preferred_element_type=F32)

        y_pairs = []
        for pair in range(heads_per_group // 2):
            h0 = g * heads_per_group + 2 * pair
            ms = []
            for h in (h0, h0 + 1):
                diff = cum[:, h:h + 1] - cum_t[h:h + 1, :]
                seg = jnp.exp(jnp.where(causal, diff, -jnp.inf))
                ms.append((cb * seg).astype(BF16))
            xp = xdt[:, h0 * HEAD_DIM: h0 * HEAD_DIM + LANES]
            x_heads = jnp.concatenate([jnp.where(lane_lo, xp, 0.0), jnp.where(lane_lo, 0.0, xp)], axis=0)
            y_pairs.append(jnp.dot(jnp.concatenate(ms, axis=1), x_heads.astype(BF16),
                                   preferred_element_type=F32))
        y = jnp.concatenate(y_pairs, axis=1) + y_off + dskip_ref[:, gcols] * xs[:, gcols]
        gy = y * _silu(z_ref[:, gcols].astype(F32))
        ms_g = jnp.mean(gy * gy, axis=-1, keepdims=True)
        y_ref[:, gcols] = (gy * lax.rsqrt(ms_g + NORM_EPS) * gn_ref[:, gcols]).astype(y_ref.dtype)


def _ssd_mixer(proj, dt_raw, conv_w, conv_b, dt_bias, a_log, d_skip, gate_norm, batch, seq):
    t = proj.shape[0]
    d_inner = gate_norm.shape[0]
    n_heads = d_inner // HEAD_DIM
    conv_dim = conv_w.shape[1]
    bc_w = conv_dim - d_inner
    nc = seq // SSM_CHUNK
    assert d_inner % bc_w == 0

    def pad_heads(v):
        return jnp.zeros((1, LANES), F32).at[0, :n_heads].set(v.astype(F32))

    tril = (_iota((SSM_CHUNK, SSM_CHUNK), 0) >= _iota((SSM_CHUNK, SSM_CHUNK), 1)).astype(F32)
    row = _iota((LANES, d_inner), 0)
    col = _iota((LANES, d_inner), 1)
    e3 = ((row < 3 * n_heads) & ((row % n_heads) == (col // HEAD_DIM))).astype(BF16)
    srow = _iota(((SSM_CONV - 1) * SSM_CHUNK, CONV_HIST + SSM_CHUNK), 0)
    scol = _iota(((SSM_CONV - 1) * SSM_CHUNK, CONV_HIST + SSM_CHUNK), 1)
    shift = (scol == CONV_HIST + srow % SSM_CHUNK - (SSM_CONV - 1 - srow // SSM_CHUNK)).astype(BF16)
    const = lambda shape: pl.BlockSpec(shape, lambda b, c: (0, 0))
    return pl.pallas_call(
        _ssd_kernel,
        grid=(batch, nc),
        in_specs=[
            pl.BlockSpec((SSM_CHUNK, d_inner), lambda b, c: (b * nc + c, 1)),
            pl.BlockSpec((SSM_CHUNK, bc_w), lambda b, c: (b * nc + c, 2 * d_inner // bc_w)),
            pl.BlockSpec((SSM_CHUNK, d_inner), lambda b, c: (b * nc + c, 0)),
            pl.BlockSpec((SSM_CHUNK, LANES), lambda b, c: (b * nc + c, 0)),
            const((SSM_CONV, conv_dim)), const((1, conv_dim)), const((1, LANES)), const((1, LANES)),
            const((1, d_inner)), const((1, d_inner)), const((SSM_CHUNK, SSM_CHUNK)), const((LANES, d_inner)),
            const(shift.shape),
        ],
        out_specs=pl.BlockSpec((SSM_CHUNK, d_inner), lambda b, c: (b * nc + c, 0)),
        out_shape=jax.ShapeDtypeStruct((t, d_inner), BF16),
        scratch_shapes=[pltpu.VMEM((CONV_HIST, conv_dim), BF16),
                        pltpu.VMEM((SSM_GROUPS, SSM_STATE, d_inner // SSM_GROUPS), F32)],
        compiler_params=_compiler_params(("parallel", "arbitrary")),
        name="ssd_mixer",
    )(proj, proj, proj, dt_raw, conv_w.astype(F32), conv_b.reshape(1, conv_dim).astype(F32),
      pad_heads(dt_bias), pad_heads(-jnp.exp(a_log.astype(F32))),
      jnp.repeat(d_skip.astype(F32), HEAD_DIM).reshape(1, d_inner), gate_norm.reshape(1, d_inner).astype(F32),
      tril, e3, shift)


def _outproj_swa_ssd_kernel(x_ref, a_ref, y_ref, wa_ref, wy_ref, o_ref):
    o_ref[...] = (x_ref[...]
                  + jnp.dot(a_ref[...], wa_ref[...], preferred_element_type=F32)
                  + jnp.dot(y_ref[...], wy_ref[...], preferred_element_type=F32))


def _outproj_swa_ssd(x, attn, y, w_attn, w_y, tm, tn):
    t, d = x.shape
    tm = min(tm, t)
    return pl.pallas_call(
        _outproj_swa_ssd_kernel,
        grid=(t // tm, d // tn),
        in_specs=[
            pl.BlockSpec((tm, tn), lambda i, j: (i, j)),
            pl.BlockSpec((tm, attn.shape[1]), lambda i, j: (i, 0)),
            pl.BlockSpec((tm, y.shape[1]), lambda i, j: (i, 0)),
            pl.BlockSpec((attn.shape[1], tn), lambda i, j: (0, j)),
            pl.BlockSpec((y.shape[1], tn), lambda i, j: (0, j)),
        ],
        out_specs=pl.BlockSpec((tm, tn), lambda i, j: (i, j)),
        out_shape=jax.ShapeDtypeStruct((t, d), F32),
        compiler_params=_compiler_params(("parallel", "arbitrary")),
        name="outproj_swa_ssd",
    )(x, attn, y, w_attn, w_y)


def _outproj_dilated_kernel(x_ref, o1_ref, o2_ref, o3_ref, l1_ref, l2_ref, l3_ref, e2_ref, w_ref, out_ref,
                            om_ref, scr_ref):
    @pl.when(pl.program_id(1) == 0)
    def _():
        tm, width = om_ref.shape
        n_res = o1_ref.shape[1]
        lane = _iota((1, LANES), 1)
        head_mask = lane % STAT_GROUP == 0
        stats = [l_ref[...].reshape(tm, LANES) for l_ref in (l1_ref, l2_ref, l3_ref)]
        maxes = [pltpu.roll(st, LANES - STAT_GROUP // 2, axis=1) for st in stats]
        m = jnp.maximum(jnp.maximum(maxes[0], maxes[1]), maxes[2])
        es = [jnp.exp2(mx - m) for mx in maxes]
        den = sum(e * st for e, st in zip(es, stats))
        inv = 1.0 / jnp.where(head_mask, den, 1.0)
        acc = None
        for e, o_ref in zip(es, (o1_ref, o2_ref, o3_ref)):
            w = jnp.where(head_mask, e * inv, 0.0)
            hi = w.astype(BF16).astype(F32)
            lo = (w - hi).astype(BF16).astype(F32)
            cat = (hi + pltpu.roll(lo, 1, axis=1)).astype(BF16)
            term = jnp.dot(cat, e2_ref[...], preferred_element_type=F32) * o_ref[...].reshape(tm, width).astype(F32)
            acc = term if acc is None else acc + term
        rows = tm // n_res
        pitch = scr_ref.shape[1] // n_res
        for c in range(width // LANES):
            for r in range(n_res):
                scr_ref[c, r * pitch:r * pitch + rows, :] = acc[r * rows:(r + 1) * rows, c * LANES:(c + 1) * LANES]
        for n in range(rows):
            om_ref[n * n_res:(n + 1) * n_res, :] = jnp.concatenate(
                [scr_ref[c, pl.ds(n, n_res, stride=pitch), :] for c in range(width // LANES)], axis=1).astype(BF16)

    out_ref[...] = x_ref[...] + jnp.dot(om_ref[...], w_ref[...], preferred_element_type=F32)


def _outproj_dilated(x, outs, lses, w_o, tm, tn):
    t, d = x.shape
    width = w_o.shape[0]
    n_heads = width // HEAD_DIM
    tm = min(tm, t)
    rows = tm // RESIDUES
    per_super = SUPERBLOCK // tm
    row = _iota((LANES, width), 0)
    col = _iota((LANES, width), 1)
    stat_head = 2 * ((row % HEAD_DIM) // STAT_GROUP) + row // HEAD_DIM
    e2 = ((row % STAT_GROUP < 2) & (stat_head == col // HEAD_DIM)).astype(BF16)
    assert n_heads * STAT_GROUP // 2 <= HEAD_DIM
    shp = (t // SUPERBLOCK, RESIDUES, SUPERBLOCK // RESIDUES)
    o_spec = pl.BlockSpec((1, RESIDUES, rows, width), lambda i, j: (i // per_super, 0, i % per_super, 0))
    l_spec = pl.BlockSpec((1, RESIDUES, rows, LANES), lambda i, j: (i // per_super, 0, i % per_super, 0))
    return pl.pallas_call(
        _outproj_dilated_kernel,
        grid=(t // tm, d // tn),
        in_specs=[pl.BlockSpec((tm, tn), lambda i, j: (i, j)), o_spec, o_spec, o_spec, l_spec, l_spec, l_spec,
                  pl.BlockSpec((LANES, width), lambda i, j: (0, 0)),
                  pl.BlockSpec((width, tn), lambda i, j: (0, j),
                               pipeline_mode=pl.Buffered(1) if tn == d else None)],
        out_specs=pl.BlockSpec((tm, tn), lambda i, j: (i, j)),
        out_shape=jax.ShapeDtypeStruct((t, d), F32),
        scratch_shapes=[pltpu.VMEM((tm, width), BF16),
                        pltpu.VMEM((width // LANES, RESIDUES * (rows + STAGE_PITCH - RESIDUES), LANES), F32)],
        compiler_params=_compiler_params(("parallel", "arbitrary")),
        name="outproj_dilated",
    )(x, *[o.reshape(shp + (width,)) for o in outs], *[l.reshape(shp + (LANES,)) for l in lses], e2, w_o)


def _mlp_kernel(x_ref, g_ref, wu_ref, wd_ref, o_ref, h_ref):
    @pl.when(pl.program_id(1) == 0)
    def _():
        h_ref[...] = _rms_norm_rows(x_ref[...], g_ref[...]).astype(BF16)
        o_ref[...] = x_ref[...]

    u = jnp.dot(h_ref[...], wu_ref[...], preferred_element_type=F32)
    a = jnp.square(jnp.maximum(u, 0.0)).astype(BF16)
    o_ref[...] += jnp.dot(a, wd_ref[...], preferred_element_type=F32)


def _mlp(x, gain, w_up, w_down, tm, tf):
    t, d = x.shape
    ff = w_up.shape[1]
    tm = min(tm, t)
    return pl.pallas_call(
        _mlp_kernel,
        grid=(t // tm, ff // tf),
        in_specs=[
            pl.BlockSpec((tm, d), lambda i, j: (i, 0)),
            pl.BlockSpec((1, d), lambda i, j: (0, 0)),
            pl.BlockSpec((d, tf), lambda i, j: (0, j)),
            pl.BlockSpec((tf, d), lambda i, j: (j, 0)),
        ],
        out_specs=pl.BlockSpec((tm, d), lambda i, j: (i, 0)),
        out_shape=jax.ShapeDtypeStruct((t, d), F32),
        scratch_shapes=[pltpu.VMEM((tm, d), BF16)],
        compiler_params=_compiler_params(("parallel", "arbitrary")),
        name="sqrelu_mlp",
    )(x, gain.reshape(1, d).astype(F32), w_up, w_down)


def _swa_ssd_layer(x, gain, w_in, q_norm, k_norm, sinks, conv_w, conv_b, dt_bias, a_log, d_skip, gate_norm,
                   w_out, batch, seq):
    d_inner = gate_norm.shape[0]
    n_ssm_heads = dt_bias.shape[0]
    n_q_heads = sinks.shape[0]
    q_w = n_q_heads * HEAD_DIM
    kv_w = SWA_KV_HEADS * HEAD_DIM
    conv_dim = conv_w.shape[1]
    splits = (q_w, q_w + kv_w, q_w + 2 * kv_w, q_w + 2 * kv_w + d_inner, q_w + 2 * kv_w + d_inner + conv_dim)
    wq, wk, wv, wz, wxbc, wdt = jnp.split(w_in, splits, axis=1)
    w_qkv_dt = jnp.concatenate([wq, wk, wv, jnp.pad(wdt, ((0, 0), (0, LANES - n_ssm_heads)))], axis=1).astype(BF16)
    w_ssm = jnp.concatenate([wz, wxbc], axis=1).astype(BF16)
    qkv_w = q_w + 2 * kv_w
    head_gain = jnp.concatenate([jnp.tile(q_norm.astype(F32) * Q_SCALE, n_q_heads),
                                 jnp.tile(k_norm.astype(F32), SWA_KV_HEADS), jnp.ones((kv_w,), F32)])
    head_mask = jnp.concatenate([jnp.ones((q_w + kv_w,), F32), jnp.zeros((kv_w,), F32)])

    qkv, dt_raw = _norm_matmul_headnorm(x, gain, w_qkv_dt, head_gain, head_mask, tm=1024, tn=qkv_w,
                                        name="norm_qkv_dt_proj", n_extra=LANES)
    proj = _norm_matmul(x, gain, w_ssm, tm=1024, tn=w_ssm.shape[1] // 4, name="norm_ssm_proj")
    attn = _swa_attention(qkv, sinks, batch, seq, q_w)
    y = _ssd_mixer(proj, dt_raw, conv_w, conv_b, dt_bias, a_log, d_skip, gate_norm, batch, seq)
    w_out = w_out.astype(BF16)
    return _outproj_swa_ssd(x, attn, y, w_out[:q_w], w_out[q_w:], tm=1024, tn=1024)


def _dilated_layer(x, gain, w_qkv, q_norm, k_norm, w_o, batch, seq):
    width = w_o.shape[0]
    n_heads = width // HEAD_DIM
    w_qkv = w_qkv.astype(BF16)
    head_gain = jnp.concatenate([jnp.tile(q_norm.astype(F32) * Q_SCALE, n_heads),
                                 jnp.tile(k_norm.astype(F32), n_heads), jnp.ones((width,), F32)])
    head_mask = jnp.concatenate([jnp.ones((2 * width,), F32), jnp.zeros((width,), F32)])
    tn = 1024
    qkv = _norm_matmul_headnorm(x, gain, w_qkv, head_gain, head_mask, tm=1024, tn=tn, name="norm_qkv_proj",
                                interleave=RESIDUES, norm_tiles=2 * width // tn)
    outs, lses = _dilated_branches(qkv, batch, seq)
    return _outproj_dilated(x, outs, lses, w_o.astype(BF16), tm=512, tn=w_o.shape[1])


def kernel(x, norm_mix, norm_ffn, w_up, w_down, ab_w_in, ab_q_norm, ab_k_norm, ab_sinks, ab_conv_w, ab_conv_b,
           ab_dt_bias, ab_a_log, ab_d_skip, ab_gate_norm, ab_w_out, c_w_qkv, c_q_norm, c_k_norm, c_w_o):
    batch, seq, d_model = x.shape
    depth = norm_mix.shape[0]
    assert seq % SUPERBLOCK == 0 and seq % SSM_CHUNK == 0
    h = x.reshape(batch * seq, d_model).astype(F32)
    for layer in range(depth):
        i = layer // 2
        if layer % 2 == 0:
            h = _swa_ssd_layer(h, norm_mix[layer], ab_w_in[i], ab_q_norm[i], ab_k_norm[i], ab_sinks[i],
                               ab_conv_w[i], ab_conv_b[i], ab_dt_bias[i], ab_a_log[i], ab_d_skip[i],
                               ab_gate_norm[i], ab_w_out[i], batch, seq)
        else:
            h = _dilated_layer(h, norm_mix[layer], c_w_qkv[i], c_q_norm[i], c_k_norm[i], c_w_o[i], batch, seq)
        h = _mlp(h, norm_ffn[layer], w_up[layer].astype(BF16), w_down[layer].astype(BF16), tm=512, tf=1024)
    return h.reshape(batch, seq, d_model).astype(x.dtype)
```

```python
import functools
import math

import jax
import jax.numpy as jnp
from jax import lax
from jax.experimental import pallas as pl
from jax.experimental.pallas import tpu as pltpu

F32 = jnp.float32
BF16 = jnp.bfloat16

NORM_EPS = 1e-5
HEAD_DIM = 64
ATTN_BLOCK = 128
SWA_KV_HEADS = 2
SWA_WINDOW = 128
SSM_GROUPS = 4
SSM_STATE = 128
SSM_CONV = 4
SSM_CHUNK = 128
DIL_BRANCHES = ((128, 1), (512, 4), (2048, 16))
RESIDUES = 16
SUPERBLOCK = RESIDUES * ATTN_BLOCK
STAGE_PITCH = 20

LOG2E = math.log2(math.e)
Q_SCALE = HEAD_DIM ** -0.5 * LOG2E

LANES = 128
MXU_WIDTH = 256
CONV_HIST = 16
VMEM_LIMIT_BYTES = 56 * 1024 * 1024
VMEM_LIMIT_LARGE_BYTES = 60 * 1024 * 1024


def _compiler_params(semantics, vmem_limit_bytes=VMEM_LIMIT_BYTES):
    return pltpu.CompilerParams(dimension_semantics=semantics, vmem_limit_bytes=vmem_limit_bytes)


def _rms_norm_rows(x, gain):
    ms = jnp.mean(x * x, axis=-1, keepdims=True)
    return x * lax.rsqrt(ms + NORM_EPS) * gain


def _norm_rows_into(h_ref, x_ref, g_ref, interleave, xs_ref):
    if interleave is None:
        h_ref[...] = _rms_norm_rows(x_ref[...], g_ref[...]).astype(BF16)
        return
    n_chunks = xs_ref.shape[0]
    rows = x_ref.shape[0] // interleave
    pitch = xs_ref.shape[1] // rows
    for c in range(n_chunks):
        for k in range(rows):
            xs_ref[c, k * pitch:k * pitch + interleave, :] = x_ref[k * interleave:(k + 1) * interleave,
                                                                   c * LANES:(c + 1) * LANES]
    for r in range(interleave):
        xr = jnp.concatenate([xs_ref[c, pl.ds(r, rows, stride=pitch), :] for c in range(n_chunks)], axis=1)
        h_ref[r * rows:(r + 1) * rows, :] = _rms_norm_rows(xr, g_ref[...]).astype(BF16)


def _norm_matmul_kernel(x_ref, g_ref, w_ref, o_ref, h_ref, xs_ref=None, *, interleave):
    @pl.when(pl.program_id(1) == 0)
    def _():
        _norm_rows_into(h_ref, x_ref, g_ref, interleave, xs_ref)

    acc = jnp.dot(h_ref[...], w_ref[...], preferred_element_type=F32)
    o_ref[...] = acc.astype(o_ref.dtype).reshape(o_ref.shape)


def _norm_matmul_headnorm_kernel(x_ref, g_ref, w_ref, hg_ref, hm_ref, ones_ref, *rest, interleave, has_extra,
                                 norm_tiles):
    rest = list(rest)
    we_ref = rest.pop(0) if has_extra else None
    o_ref = rest.pop(0)
    extra_ref = rest.pop(0) if has_extra else None
    h_ref = rest.pop(0)
    xs_ref = rest.pop(0) if interleave is not None else None
    j = pl.program_id(1)

    @pl.when(j == 0)
    def _():
        _norm_rows_into(h_ref, x_ref, g_ref, interleave, xs_ref)
        if has_extra:
            extra_ref[...] = jnp.dot(h_ref[...], we_ref[...], preferred_element_type=F32)

    acc = jnp.dot(h_ref[...], w_ref[...], preferred_element_type=F32)
    in_norm_range = (j >= norm_tiles[0]) & (j < norm_tiles[1])

    @pl.when(in_norm_range)
    def _():
        chunks = []
        for c in range(acc.shape[1] // MXU_WIDTH):
            cols = slice(c * MXU_WIDTH, (c + 1) * MXU_WIDTH)
            a = acc[:, cols]
            ss = jnp.dot((a * a).astype(BF16), ones_ref[...], preferred_element_type=F32)
            r = lax.rsqrt(ss * (1.0 / HEAD_DIM) + NORM_EPS)
            chunks.append((a * jnp.where(hm_ref[:, cols] > 0.0, r, 1.0) * hg_ref[:, cols]).astype(o_ref.dtype))
        o_ref[...] = jnp.concatenate(chunks, axis=1).reshape(o_ref.shape)

    @pl.when(jnp.logical_not(in_norm_range))
    def _():
        o_ref[...] = acc.astype(o_ref.dtype).reshape(o_ref.shape)


def _proj_scratch(tm, d, interleave):
    scratch = [pltpu.VMEM((tm, d), BF16)]
    if interleave is not None:
        scratch.append(pltpu.VMEM((d // LANES, tm // interleave * STAGE_PITCH, LANES), F32))
    return scratch


def _row_specs(t, tm, tn, interleave):
    if interleave is None:
        return (lambda n: jax.ShapeDtypeStruct((t, n), BF16)), pl.BlockSpec((tm, tn), lambda i, j: (i, j))
    per_super = SUPERBLOCK // tm
    rows = tm // interleave
    shape = lambda n: jax.ShapeDtypeStruct((t // SUPERBLOCK, interleave, SUPERBLOCK // interleave, n), BF16)
    spec = pl.BlockSpec((1, interleave, rows, tn), lambda i, j: (i // per_super, 0, i % per_super, j))
    return shape, spec


def _norm_matmul(x, gain, w, tm, tn, name, interleave=None):
    t, d = x.shape
    n = w.shape[1]
    tm = min(tm, t)
    shape, out_spec = _row_specs(t, tm, tn, interleave)
    out = pl.pallas_call(
        functools.partial(_norm_matmul_kernel, interleave=interleave),
        grid=(t // tm, n // tn),
        in_specs=[
            pl.BlockSpec((tm, d), lambda i, j: (i, 0)),
            pl.BlockSpec((1, d), lambda i, j: (0, 0)),
            pl.BlockSpec((d, tn), lambda i, j: (0, j)),
        ],
        out_specs=out_spec,
        out_shape=shape(n),
        scratch_shapes=_proj_scratch(tm, d, interleave),
        compiler_params=_compiler_params(("parallel", "arbitrary")),
        name=name,
    )(x, gain.reshape(1, d).astype(F32), w)
    return out.reshape(t, n)


def _head_ones():
    r = lax.broadcasted_iota(jnp.int32, (MXU_WIDTH, MXU_WIDTH), 0)
    c = lax.broadcasted_iota(jnp.int32, (MXU_WIDTH, MXU_WIDTH), 1)
    return ((r // HEAD_DIM) == (c // HEAD_DIM)).astype(BF16)


def _norm_matmul_headnorm(x, gain, w, head_gain, head_mask, tm, tn, name, norm_tiles, interleave=None,
                          w_extra=None):
    t, d = x.shape
    n_main = w.shape[1]
    tm = min(tm, t)
    shape, out_spec = _row_specs(t, tm, tn, interleave)
    operands = [x, gain.reshape(1, d).astype(F32), w, head_gain.reshape(1, n_main).astype(F32),
                head_mask.reshape(1, n_main).astype(F32), _head_ones()]
    in_specs = [
        pl.BlockSpec((tm, d), lambda i, j: (i, 0)),
        pl.BlockSpec((1, d), lambda i, j: (0, 0)),
        pl.BlockSpec((d, tn), lambda i, j: (0, j)),
        pl.BlockSpec((1, tn), lambda i, j: (0, j)),
        pl.BlockSpec((1, tn), lambda i, j: (0, j)),
        pl.BlockSpec((MXU_WIDTH, MXU_WIDTH), lambda i, j: (0, 0)),
    ]
    out_shapes = [shape(n_main)]
    out_specs = [out_spec]
    if w_extra is not None:
        n_extra = w_extra.shape[1]
        operands.append(w_extra)
        in_specs.append(pl.BlockSpec((d, n_extra), lambda i, j: (0, 0)))
        out_shapes.append(jax.ShapeDtypeStruct((t, n_extra), F32))
        out_specs.append(pl.BlockSpec((tm, n_extra), lambda i, j: (i, 0)))
    outs = pl.pallas_call(
        functools.partial(_norm_matmul_headnorm_kernel, interleave=interleave, has_extra=w_extra is not None,
                          norm_tiles=norm_tiles),
        grid=(t // tm, n_main // tn),
        in_specs=in_specs,
        out_specs=out_specs,
        out_shape=out_shapes,
        scratch_shapes=_proj_scratch(tm, d, interleave),
        compiler_params=_compiler_params(("parallel", "arbitrary")),
        name=name,
    )(*operands)
    if w_extra is not None:
        return outs[0].reshape(t, n_main), outs[1]
    return outs[0].reshape(t, n_main)


def _lane_lo():
    return lax.broadcasted_iota(jnp.int32, (1, LANES), 1) < HEAD_DIM


def _attn_tiles(qs, ks, vs, masks, lane_lo, sinks=None, normalize=False):
    sel_rows = _iota((4 * ATTN_BLOCK, LANES), 0) < 2 * ATTN_BLOCK
    row_sum = (sel_rows == (_iota((4 * ATTN_BLOCK, LANES), 1) < HEAD_DIM)).astype(BF16)
    results = []
    for i, (q, k, v, mask) in enumerate(zip(qs, ks, vs, masks)):
        zq = jnp.zeros_like(q)
        q_heads = jnp.concatenate([jnp.where(lane_lo, q, zq), jnp.where(lane_lo, zq, q)], axis=0)
        s2 = lax.dot_general(q_heads, k, (((1,), (1,)), ((), ())), preferred_element_type=F32)
        ps, ms = [], []
        for idx in range(2):
            s = jnp.where(mask, s2[idx * ATTN_BLOCK:(idx + 1) * ATTN_BLOCK], -jnp.inf)
            m = jnp.max(s, axis=-1, keepdims=True)
            if sinks is not None:
                m = jnp.maximum(m, sinks[i][idx])
            ps.append(jnp.exp2(s - m).astype(BF16))
            ms.append(m)
        zv = jnp.zeros_like(v)
        v_heads = jnp.concatenate([jnp.where(lane_lo, v, zv), jnp.where(lane_lo, zv, v)], axis=0)
        res = jnp.dot(jnp.concatenate(ps, axis=1), jnp.concatenate([v_heads, row_sum], axis=1),
                      preferred_element_type=F32)
        o, l = res[:, :LANES], res[:, LANES:]
        if sinks is not None:
            l = l + jnp.where(lane_lo, jnp.exp2(sinks[i][0] - ms[0]), jnp.exp2(sinks[i][1] - ms[1]))
        if normalize:
            o = o * (1.0 / l)
        results.append((o, ms, l))
    return results


def _dist_mask(q_pos, k_pos, max_dist, key_exists=None):
    dist = q_pos - k_pos
    mask = (dist >= 0) & (dist <= max_dist)
    if key_exists is not None:
        mask = mask & key_exists
    return mask


def _iota(shape, dim):
    return lax.broadcasted_iota(jnp.int32, shape, dim)


def _swa_kernel(sinks_ref, q_ref, kvp_ref, kvc_ref, o_ref):
    first = pl.program_id(1) == 0
    lane_lo = _lane_lo()
    kj = _iota((1, 2 * ATTN_BLOCK), 1)
    mask = _dist_mask(_iota((ATTN_BLOCK, 1), 0) + ATTN_BLOCK, kj, SWA_WINDOW - 1,
                      (kj >= ATTN_BLOCK) | jnp.logical_not(first))
    kv2 = jnp.concatenate([kvp_ref[...], kvc_ref[...]], axis=0).astype(F32)

    def dup_heads(x):
        xr = pltpu.roll(x, HEAD_DIM, axis=1)
        return jnp.where(lane_lo, x, xr).astype(BF16), jnp.where(lane_lo, xr, x).astype(BF16)

    k_dup = dup_heads(kv2[:, :LANES])
    v_dup = dup_heads(kv2[:, LANES:])
    n_pairs = q_ref.shape[1] // LANES
    pairs_per_kv = n_pairs // SWA_KV_HEADS
    qs, ks, vs, masks, sinks = [], [], [], [], []
    for j in range(n_pairs):
        kh = j // pairs_per_kv
        qs.append(q_ref[:, j * LANES:(j + 1) * LANES])
        ks.append(k_dup[kh])
        vs.append(v_dup[kh])
        masks.append(mask)
        sinks.append((sinks_ref[2 * j], sinks_ref[2 * j + 1]))
    for j, (o, _, _) in enumerate(_attn_tiles(qs, ks, vs, masks, lane_lo, sinks, normalize=True)):
        o_ref[:, j * LANES:(j + 1) * LANES] = o.astype(o_ref.dtype)


def _swa_attention(qkv, sinks, batch, seq, q_width, q_block, kv_block):
    t = qkv.shape[0]
    nb = seq // ATTN_BLOCK
    return pl.pallas_call(
        _swa_kernel,
        grid=(batch, nb),
        in_specs=[
            pl.BlockSpec(memory_space=pltpu.SMEM),
            pl.BlockSpec((ATTN_BLOCK, q_width), lambda b, n: (b * nb + n, q_block)),
            pl.BlockSpec((ATTN_BLOCK, 2 * LANES), lambda b, n: (b * nb + jnp.maximum(n - 1, 0), kv_block)),
            pl.BlockSpec((ATTN_BLOCK, 2 * LANES), lambda b, n: (b * nb + n, kv_block)),
        ],
        out_specs=pl.BlockSpec((ATTN_BLOCK, q_width), lambda b, n: (b * nb + n, 0)),
        out_shape=jax.ShapeDtypeStruct((t, q_width), BF16),
        compiler_params=_compiler_params(("parallel", "arbitrary")),
        name="swa_attention",
    )(sinks.astype(F32) * LOG2E, qkv, qkv, qkv)


STAT_GROUP = 4


def _write_stats(acc, lane, pair, ms, l):
    lane_lo = lane < HEAD_DIM
    stat = jnp.where(lane % STAT_GROUP < STAT_GROUP // 2, l, jnp.where(lane_lo, ms[0], ms[1]))
    return jnp.where((lane % HEAD_DIM) // STAT_GROUP == pair, stat, acc)


def _dil16_kernel(q_ref, kp_ref, kc_ref, vp_ref, vc_ref, o_ref, lse_ref):
    first = pl.program_id(2) == 0
    n_res = q_ref.shape[1]
    n_pairs = q_ref.shape[3] // LANES
    lane = _iota((1, LANES), 1)
    lane_lo = lane < HEAD_DIM
    kj = _iota((1, 2 * ATTN_BLOCK), 1)
    mask = _dist_mask(_iota((ATTN_BLOCK, 1), 0) + ATTN_BLOCK, kj, ATTN_BLOCK,
                      (kj >= ATTN_BLOCK) | jnp.logical_not(first))

    def body(j, lse_accs):
        cols = pl.ds(pl.multiple_of(j * LANES, LANES), LANES)
        qs = [q_ref[0, r, :, cols] for r in range(n_res)]
        ks = [jnp.concatenate([kp_ref[0, r, :, cols], kc_ref[0, r, :, cols]], axis=0) for r in range(n_res)]
        vs = [jnp.concatenate([vp_ref[0, r, :, cols], vc_ref[0, r, :, cols]], axis=0) for r in range(n_res)]
        out = []
        for r, (o, ms, ls) in enumerate(_attn_tiles(qs, ks, vs, [mask] * n_res, lane_lo)):
            o_ref[0, r, :, cols] = o.astype(o_ref.dtype)
            out.append(_write_stats(lse_accs[r], lane, j, ms, ls))
        return tuple(out)

    zeros = tuple(jnp.zeros((ATTN_BLOCK, LANES), F32) for _ in range(n_res))
    for r, acc in enumerate(lax.fori_loop(0, n_pairs, body, zeros, unroll=4)):
        lse_ref[0, r] = acc


def _dil4_kernel(q_ref, kp_ref, kc_ref, vp_ref, vc_ref, o_ref, lse_ref):
    first = pl.program_id(2) == 0
    n_sub = q_ref.shape[1]
    rows = ATTN_BLOCK // n_sub
    n_tiles = q_ref.shape[3] // rows
    n_pairs = q_ref.shape[4] // LANES
    lane = _iota((1, LANES), 1)
    lane_lo = lane < HEAD_DIM
    qi = _iota((ATTN_BLOCK, 1), 0)
    kj = _iota((1, 2 * ATTN_BLOCK), 1)
    q_pos = n_sub * (rows + qi % rows) + qi // rows
    k_pos = n_sub * (kj % (2 * rows)) + kj // (2 * rows)
    mask = _dist_mask(q_pos, k_pos, ATTN_BLOCK)
    mask_first = mask & ((kj % (2 * rows) >= rows) | jnp.logical_not(first))

    def gather(cur_ref, prev_ref, tile, cols):
        if tile == 0:
            parts = [jnp.concatenate([prev_ref[0, a, 0, :, cols], cur_ref[0, a, 0, 0:rows, cols]], axis=0)
                     for a in range(n_sub)]
        else:
            parts = [cur_ref[0, a, 0, (tile - 1) * rows:(tile + 1) * rows, cols] for a in range(n_sub)]
        return jnp.concatenate(parts, axis=0)

    def body(j, lse_accs):
        cols = pl.ds(pl.multiple_of(j * LANES, LANES), LANES)
        qs = [jnp.concatenate([q_ref[0, a, 0, t * rows:(t + 1) * rows, cols] for a in range(n_sub)], axis=0)
              for t in range(n_tiles)]
        ks = [gather(kc_ref, kp_ref, t, cols) for t in range(n_tiles)]
        vs = [gather(vc_ref, vp_ref, t, cols) for t in range(n_tiles)]
        masks = [mask_first] + [mask] * (n_tiles - 1)
        out = []
        for t, (o, ms, ls) in enumerate(_attn_tiles(qs, ks, vs, masks, lane_lo)):
            ob = o.astype(o_ref.dtype)
            for a in range(n_sub):
                o_ref[0, a, 0, t * rows:(t + 1) * rows, cols] = ob[a * rows:(a + 1) * rows, :]
            out.append(_write_stats(lse_accs[t], lane, j, ms, ls))
        return tuple(out)

    zeros = tuple(jnp.zeros((ATTN_BLOCK, LANES), F32) for _ in range(n_tiles))
    for t, acc in enumerate(lax.fori_loop(0, n_pairs, body, zeros, unroll=4)):
        for a in range(n_sub):
            lse_ref[0, a, 0, t * rows:(t + 1) * rows, :] = acc[a * rows:(a + 1) * rows, :]


def _dil1_kernel(q_ref, kp_ref, kc_ref, vp_ref, vc_ref, o_ref, lse_ref):
    first = (pl.program_id(1) == 0) & (pl.program_id(2) == 0)
    n_res = q_ref.shape[1]
    sub = ATTN_BLOCK // n_res
    n_tiles = q_ref.shape[2] // sub
    n_pairs = q_ref.shape[3] // LANES
    prev_rows = kp_ref.shape[2]
    lane = _iota((1, LANES), 1)
    lane_lo = lane < HEAD_DIM
    qi = _iota((ATTN_BLOCK, 1), 0)
    kj = _iota((1, 2 * ATTN_BLOCK), 1)
    q_pos = n_res * (sub + qi % sub) + qi // sub
    k_pos = n_res * (kj % (2 * sub)) + kj // (2 * sub)
    mask = _dist_mask(q_pos, k_pos, ATTN_BLOCK)
    mask_first = mask & ((kj % (2 * sub) >= sub) | jnp.logical_not(first))

    def tile_rows(x, lo, hi):
        return x[:, lo:hi, :].reshape(n_res * (hi - lo), LANES)

    def body(j, lse_accs):
        cols = pl.ds(pl.multiple_of(j * LANES, LANES), LANES)
        q = q_ref[0, :, :, cols].astype(F32)
        qs = [tile_rows(q, t * sub, (t + 1) * sub).astype(BF16) for t in range(n_tiles)]

        def gather(cur_ref, prev_ref):
            cur = cur_ref[0, :, :, cols].astype(F32)
            prev = prev_ref[0, :, :, cols].astype(F32)
            tiles = []
            for t in range(n_tiles):
                if t == 0:
                    x = jnp.concatenate([prev[:, prev_rows - sub:, :], cur[:, 0:sub, :]], axis=1)
                else:
                    x = cur[:, (t - 1) * sub:(t + 1) * sub, :]
                tiles.append(x.reshape(n_res * 2 * sub, LANES).astype(BF16))
            return tiles

        ks = gather(kc_ref, kp_ref)
        vs = gather(vc_ref, vp_ref)
        masks = [mask_first] + [mask] * (n_tiles - 1)
        res = _attn_tiles(qs, ks, vs, masks, lane_lo)
        o_full = jnp.concatenate([o.reshape(n_res, sub, LANES) for o, _, _ in res], axis=1)
        o_ref[0, :, :, cols] = o_full.astype(o_ref.dtype)
        return tuple(_write_stats(lse_accs[t], lane, j, ms, ls) for t, (_, ms, ls) in enumerate(res))

    zeros = tuple(jnp.zeros((ATTN_BLOCK, LANES), F32) for _ in range(n_tiles))
    accs = lax.fori_loop(0, n_pairs, body, zeros, unroll=4)
    lse_ref[0] = jnp.concatenate([acc.reshape(n_res, sub, LANES) for acc in accs], axis=1)


def _dilated_branches(qkv, batch, seq):
    t, w = qkv.shape[0], qkv.shape[1] // 3
    nsb = seq // SUPERBLOCK
    n_super = t // SUPERBLOCK
    per_res = SUPERBLOCK // RESIDUES
    outs, lses = [], []

    def in_specs(block, cur_map, prev_block, prev_map):
        cur = lambda col: pl.BlockSpec(block, lambda *g: cur_map(*g) + (col,))
        prev = lambda col: pl.BlockSpec(prev_block, lambda *g: prev_map(*g) + (col,))
        return [cur(0), prev(1), cur(1), prev(2), cur(2)]

    r16 = 4
    shp = (n_super, RESIDUES, per_res)
    cur_map = lambda b, r, s: (b * nsb + s, r, 0)
    prev_map = lambda b, r, s: (b * nsb + jnp.maximum(s - 1, 0), r, 0)
    qkv4 = qkv.reshape(shp + (3 * w,))
    o16, l16 = pl.pallas_call(
        _dil16_kernel, grid=(batch, RESIDUES // r16, nsb),
        in_specs=in_specs((1, r16, per_res, w), cur_map, (1, r16, per_res, w), prev_map),
        out_specs=[pl.BlockSpec((1, r16, per_res, w), lambda *g: cur_map(*g) + (0,)),
                   pl.BlockSpec((1, r16, per_res, LANES), lambda *g: cur_map(*g) + (0,))],
        out_shape=[jax.ShapeDtypeStruct(shp + (w,), BF16), jax.ShapeDtypeStruct(shp + (LANES,), F32)],
        compiler_params=_compiler_params(("parallel", "parallel", "arbitrary")),
        name="dilated_attention_d16",
    )(qkv4, qkv4, qkv4, qkv4, qkv4)

    shp5 = (n_super, 4, 4, per_res)
    tail = per_res // 4
    cur_map5 = lambda b, r, s: (b * nsb + s, 0, r, 0)
    prev_map5 = lambda b, r, s: (b * nsb + jnp.maximum(s - 1, 0), 0, r, per_res // tail - 1)
    qkv5 = qkv.reshape(shp5 + (3 * w,))
    o4, l4 = pl.pallas_call(
        _dil4_kernel, grid=(batch, 4, nsb),
        in_specs=in_specs((1, 4, 1, per_res, w), cur_map5, (1, 4, 1, tail, w), prev_map5),
        out_specs=[pl.BlockSpec((1, 4, 1, per_res, w), lambda *g: cur_map5(*g) + (0,)),
                   pl.BlockSpec((1, 4, 1, per_res, LANES), lambda *g: cur_map5(*g) + (0,))],
        out_shape=[jax.ShapeDtypeStruct(shp5 + (w,), BF16), jax.ShapeDtypeStruct(shp5 + (LANES,), F32)],
        compiler_params=_compiler_params(("parallel", "parallel", "arbitrary")),
        name="dilated_attention_d4",
    )(qkv5, qkv5, qkv5, qkv5, qkv5)

    run, hist = 32, 16
    runs = per_res // run
    cur_map1 = lambda b, s, c: (b * nsb + s, 0, c)

    def prev_map1(b, s, c):
        g = (b * nsb + s) * (per_res // hist) + c * (run // hist) - 1
        g = jnp.maximum(g, 0)
        return (g // (per_res // hist), 0, g % (per_res // hist))

    o1, l1 = pl.pallas_call(
        _dil1_kernel, grid=(batch, nsb, runs),
        in_specs=in_specs((1, RESIDUES, run, w), cur_map1, (1, RESIDUES, hist, w), prev_map1),
        out_specs=[pl.BlockSpec((1, RESIDUES, run, w), lambda *g: cur_map1(*g) + (0,)),
                   pl.BlockSpec((1, RESIDUES, run, LANES), lambda *g: cur_map1(*g) + (0,))],
        out_shape=[jax.ShapeDtypeStruct(shp + (w,), BF16), jax.ShapeDtypeStruct(shp + (LANES,), F32)],
        compiler_params=_compiler_params(("parallel", "arbitrary", "arbitrary")),
        name="dilated_attention_d1",
    )(qkv4, qkv4, qkv4, qkv4, qkv4)

    for o, l in ((o1, l1), (o4, l4), (o16, l16)):
        outs.append(o.reshape(t, w))
        lses.append(l.reshape(t, LANES))
    return outs, lses


def _softplus(x):
    return jnp.maximum(x, 0.0) + jnp.log1p(jnp.exp(-jnp.abs(x)))


def _silu(x):
    h = 0.5 * x
    return h * jnp.tanh(h) + h


def _expand_heads(v, head_mask, e3):
    n_heads = e3.shape[1] // HEAD_DIM
    v = jnp.where(head_mask, v, 0.0)
    hi = v.astype(BF16).astype(F32)
    r = v - hi
    mid = r.astype(BF16).astype(F32)
    lo = (r - mid).astype(BF16).astype(F32)
    cat = hi + pltpu.roll(mid, n_heads, axis=1) + pltpu.roll(lo, 2 * n_heads, axis=1)
    return jnp.dot(cat.astype(BF16), e3, preferred_element_type=F32)


def _ssd_kernel(xs_ref, bc_ref, z_ref, dt_ref, cw_ref, cb_ref, dtb_ref, a_ref, dskip_ref, gn_ref, tril_ref, e3_ref,
                shift_ref, y_ref, hist_ref, state_ref):
    chunk = SSM_CHUNK
    d_inner = z_ref.shape[1]
    n_heads = d_inner // HEAD_DIM
    group_w = d_inner // SSM_GROUPS
    heads_per_group = n_heads // SSM_GROUPS

    @pl.when(pl.program_id(1) == 0)
    def _():
        hist_ref[...] = jnp.zeros(hist_ref.shape, hist_ref.dtype)
        state_ref[...] = jnp.zeros(state_ref.shape, F32)

    parts = []
    for ref, cols in ((xs_ref, slice(0, d_inner)), (bc_ref, slice(d_inner, hist_ref.shape[1]))):
        cur = ref[...]
        ext = jnp.concatenate([hist_ref[:, cols], cur], axis=0)
        shifted = jnp.dot(shift_ref[...], ext, preferred_element_type=F32)
        conv = cb_ref[:, cols] + cw_ref[SSM_CONV - 1:SSM_CONV, cols] * cur.astype(F32)
        for j in range(SSM_CONV - 1):
            conv = conv + cw_ref[j:j + 1, cols] * shifted[j * chunk:(j + 1) * chunk]
        hist_ref[:, cols] = cur[chunk - CONV_HIST:, :]
        parts.append(_silu(conv))
    xs = parts[0]
    xbc_bc = parts[1]

    lane = _iota((1, LANES), 1)
    head_mask = lane < n_heads
    lane_lo = lane < HEAD_DIM
    e3 = e3_ref[...]

    dt = jnp.where(head_mask, _softplus(dt_ref[...] + dtb_ref[...]), 0.0)
    cum = jnp.dot(tril_ref[...], dt * a_ref[...], precision=lax.Precision.HIGHEST,
                  preferred_element_type=F32)
    cum_t = cum.T
    ecum_e = _expand_heads(jnp.exp(cum), head_mask, e3)
    dte_e = _expand_heads(jnp.exp(cum[chunk - 1:chunk, :] - cum), head_mask, e3)
    xdt = xs * _expand_heads(dt, head_mask, e3)
    xend = xdt * dte_e

    causal = _iota((chunk, chunk), 0) >= _iota((chunk, chunk), 1)

    for g in range(SSM_GROUPS):
        gcols = slice(g * group_w, (g + 1) * group_w)
        b_f32 = xbc_bc[:, g * SSM_STATE: (g + 1) * SSM_STATE]
        c_off = SSM_GROUPS * SSM_STATE
        c_bf = xbc_bc[:, c_off + g * SSM_STATE: c_off + (g + 1) * SSM_STATE].astype(BF16)
        cb = lax.dot_general(c_bf, b_f32.astype(BF16), (((1,), (1,)), ((), ())), preferred_element_type=F32)
        st = state_ref[g]
        y_off = jnp.dot(c_bf, st.astype(BF16), preferred_element_type=F32) * ecum_e[:, gcols]
        state_ref[g] = st * ecum_e[chunk - 1:chunk, gcols] + jnp.dot(
            b_f32.T.astype(BF16), xend[:, gcols].astype(BF16), preferred_element_type=F32)

        y_pairs = []
        for pair in range(heads_per_group // 2):
            h0 = g * heads_per_group + 2 * pair
            ms = []
            for h in (h0, h0 + 1):
                diff = cum[:, h:h + 1] - cum_t[h:h + 1, :]
                seg = jnp.exp(jnp.where(causal, diff, -jnp.inf))
                ms.append((cb * seg).astype(BF16))
            xp = xdt[:, h0 * HEAD_DIM: h0 * HEAD_DIM + LANES]
            x_heads = jnp.concatenate([jnp.where(lane_lo, xp, 0.0), jnp.where(lane_lo, 0.0, xp)], axis=0)
            y_pairs.append(jnp.dot(jnp.concatenate(ms, axis=1), x_heads.astype(BF16),
                                   preferred_element_type=F32))
        y = jnp.concatenate(y_pairs, axis=1) + y_off + dskip_ref[:, gcols] * xs[:, gcols]
        gy = y * _silu(z_ref[:, gcols].astype(F32))
        ms_g = jnp.mean(gy * gy, axis=-1, keepdims=True)
        y_ref[:, gcols] = (gy * lax.rsqrt(ms_g + NORM_EPS) * gn_ref[:, gcols]).astype(y_ref.dtype)


def _ssd_mixer(proj, dt_raw, conv_w, conv_b, dt_bias, a_log, d_skip, gate_norm, batch, seq):
    t = proj.shape[0]
    d_inner = gate_norm.shape[0]
    n_heads = d_inner // HEAD_DIM
    conv_dim = conv_w.shape[1]
    bc_w = conv_dim - d_inner
    nc = seq // SSM_CHUNK
    assert d_inner % bc_w == 0

    def pad_heads(v):
        return jnp.zeros((1, LANES), F32).at[0, :n_heads].set(v.astype(F32))

    tril = (_iota((SSM_CHUNK, SSM_CHUNK), 0) >= _iota((SSM_CHUNK, SSM_CHUNK), 1)).astype(F32)
    row = _iota((LANES, d_inner), 0)
    col = _iota((LANES, d_inner), 1)
    e3 = ((row < 3 * n_heads) & ((row % n_heads) == (col // HEAD_DIM))).astype(BF16)
    srow = _iota(((SSM_CONV - 1) * SSM_CHUNK, CONV_HIST + SSM_CHUNK), 0)
    scol = _iota(((SSM_CONV - 1) * SSM_CHUNK, CONV_HIST + SSM_CHUNK), 1)
    shift = (scol == CONV_HIST + srow % SSM_CHUNK - (SSM_CONV - 1 - srow // SSM_CHUNK)).astype(BF16)
    const = lambda shape: pl.BlockSpec(shape, lambda b, c: (0, 0))
    return pl.pallas_call(
        _ssd_kernel,
        grid=(batch, nc),
        in_specs=[
            pl.BlockSpec((SSM_CHUNK, d_inner), lambda b, c: (b * nc + c, 1)),
            pl.BlockSpec((SSM_CHUNK, bc_w), lambda b, c: (b * nc + c, 2 * d_inner // bc_w)),
            pl.BlockSpec((SSM_CHUNK, d_inner), lambda b, c: (b * nc + c, 0)),
            pl.BlockSpec((SSM_CHUNK, LANES), lambda b, c: (b * nc + c, 0)),
            const((SSM_CONV, conv_dim)), const((1, conv_dim)), const((1, LANES)), const((1, LANES)),
            const((1, d_inner)), const((1, d_inner)), const((SSM_CHUNK, SSM_CHUNK)), const((LANES, d_inner)),
            const(shift.shape),
        ],
        out_specs=pl.BlockSpec((SSM_CHUNK, d_inner), lambda b, c: (b * nc + c, 0)),
        out_shape=jax.ShapeDtypeStruct((t, d_inner), BF16),
        scratch_shapes=[pltpu.VMEM((CONV_HIST, conv_dim), BF16),
                        pltpu.VMEM((SSM_GROUPS, SSM_STATE, d_inner // SSM_GROUPS), F32)],
        compiler_params=_compiler_params(("parallel", "arbitrary")),
        name="ssd_mixer",
    )(proj, proj, proj, dt_raw, conv_w.astype(F32), conv_b.reshape(1, conv_dim).astype(F32),
      pad_heads(dt_bias), pad_heads(-jnp.exp(a_log.astype(F32))),
      jnp.repeat(d_skip.astype(F32), HEAD_DIM).reshape(1, d_inner), gate_norm.reshape(1, d_inner).astype(F32),
      tril, e3, shift)


def _outproj_swa_ssd_kernel(x_ref, a_ref, y_ref, wa_ref, wy_ref, g_ref, o_ref, h_ref):
    out = (x_ref[...]
           + jnp.dot(a_ref[...], wa_ref[...], preferred_element_type=F32)
           + jnp.dot(y_ref[...], wy_ref[...], preferred_element_type=F32))
    o_ref[...] = out
    h_ref[...] = _rms_norm_rows(out, g_ref[...]).astype(BF16)


def _outproj_swa_ssd(x, attn, y, w_attn, w_y, next_gain, tm):
    t, d = x.shape
    tm = min(tm, t)
    resident = lambda shape: pl.BlockSpec(shape, lambda i: (0, 0), pipeline_mode=pl.Buffered(1))
    return pl.pallas_call(
        _outproj_swa_ssd_kernel,
        grid=(t // tm,),
        in_specs=[
            pl.BlockSpec((tm, d), lambda i: (i, 0)),
            pl.BlockSpec((tm, attn.shape[1]), lambda i: (i, 0)),
            pl.BlockSpec((tm, y.shape[1]), lambda i: (i, 0)),
            resident((attn.shape[1], d)),
            resident((y.shape[1], d)),
            pl.BlockSpec((1, d), lambda i: (0, 0)),
        ],
        out_specs=[pl.BlockSpec((tm, d), lambda i: (i, 0)), pl.BlockSpec((tm, d), lambda i: (i, 0))],
        out_shape=[jax.ShapeDtypeStruct((t, d), F32), jax.ShapeDtypeStruct((t, d), BF16)],
        compiler_params=_compiler_params(("parallel",)),
        name="outproj_swa_ssd",
    )(x, attn, y, w_attn, w_y, next_gain.reshape(1, d).astype(F32))


def _outproj_dilated_kernel(x_ref, o1_ref, o2_ref, o3_ref, l1_ref, l2_ref, l3_ref, e2_ref, w_ref, g_ref,
                            out_ref, h_ref, om_ref, scr_ref):
    def merge_branches():
        tm, width = om_ref.shape
        n_res = o1_ref.shape[1]
        lane = _iota((1, LANES), 1)
        head_mask = lane % STAT_GROUP == 0
        stats = [l_ref[...].reshape(tm, LANES) for l_ref in (l1_ref, l2_ref, l3_ref)]
        maxes = [pltpu.roll(st, LANES - STAT_GROUP // 2, axis=1) for st in stats]
        m = jnp.maximum(jnp.maximum(maxes[0], maxes[1]), maxes[2])
        es = [jnp.exp2(mx - m) for mx in maxes]
        den = sum(e * st for e, st in zip(es, stats))
        inv = 1.0 / jnp.where(head_mask, den, 1.0)
        acc = None
        for e, o_ref in zip(es, (o1_ref, o2_ref, o3_ref)):
            w = jnp.where(head_mask, e * inv, 0.0)
            hi = w.astype(BF16).astype(F32)
            lo = (w - hi).astype(BF16).astype(F32)
            cat = (hi + pltpu.roll(lo, 1, axis=1)).astype(BF16)
            term = jnp.dot(cat, e2_ref[...], preferred_element_type=F32) * o_ref[...].reshape(tm, width).astype(F32)
            acc = term if acc is None else acc + term
        rows = tm // n_res
        pitch = scr_ref.shape[1] // n_res
        for c in range(width // LANES):
            for r in range(n_res):
                scr_ref[c, r * pitch:r * pitch + rows, :] = acc[r * rows:(r + 1) * rows, c * LANES:(c + 1) * LANES]
        for n in range(rows):
            om_ref[n * n_res:(n + 1) * n_res, :] = jnp.concatenate(
                [scr_ref[c, pl.ds(n, n_res, stride=pitch), :] for c in range(width // LANES)], axis=1).astype(BF16)

    merge_branches()
    out = x_ref[...] + jnp.dot(om_ref[...], w_ref[...], preferred_element_type=F32)
    out_ref[...] = out
    h_ref[...] = _rms_norm_rows(out, g_ref[...]).astype(BF16)


def _outproj_dilated(x, outs, lses, w_o, next_gain, tm):
    t, d = x.shape
    width = w_o.shape[0]
    n_heads = width // HEAD_DIM
    tm = min(tm, t)
    rows = tm // RESIDUES
    per_super = SUPERBLOCK // tm
    row = _iota((LANES, width), 0)
    col = _iota((LANES, width), 1)
    stat_head = 2 * ((row % HEAD_DIM) // STAT_GROUP) + row // HEAD_DIM
    e2 = ((row % STAT_GROUP < 2) & (stat_head == col // HEAD_DIM)).astype(BF16)
    assert n_heads * STAT_GROUP // 2 <= HEAD_DIM
    shp = (t // SUPERBLOCK, RESIDUES, SUPERBLOCK // RESIDUES)
    o_spec = pl.BlockSpec((1, RESIDUES, rows, width), lambda i: (i // per_super, 0, i % per_super, 0))
    l_spec = pl.BlockSpec((1, RESIDUES, rows, LANES), lambda i: (i // per_super, 0, i % per_super, 0))
    row_spec = pl.BlockSpec((tm, d), lambda i: (i, 0))
    return pl.pallas_call(
        _outproj_dilated_kernel,
        grid=(t // tm,),
        in_specs=[row_spec, o_spec, o_spec, o_spec, l_spec, l_spec, l_spec,
                  pl.BlockSpec((LANES, width), lambda i: (0, 0)),
                  pl.BlockSpec((width, d), lambda i: (0, 0), pipeline_mode=pl.Buffered(1)),
                  pl.BlockSpec((1, d), lambda i: (0, 0))],
        out_specs=[row_spec, row_spec],
        out_shape=[jax.ShapeDtypeStruct((t, d), F32), jax.ShapeDtypeStruct((t, d), BF16)],
        scratch_shapes=[pltpu.VMEM((tm, width), BF16),
                        pltpu.VMEM((width // LANES, RESIDUES * (rows + STAGE_PITCH - RESIDUES), LANES), F32)],
        compiler_params=_compiler_params(("parallel",), VMEM_LIMIT_LARGE_BYTES),
        name="outproj_dilated",
    )(x, *[o.reshape(shp + (width,)) for o in outs], *[l.reshape(shp + (LANES,)) for l in lses], e2, w_o,
      next_gain.reshape(1, d).astype(F32))


def _mlp_kernel(x_ref, h_ref, wu_ref, wd_ref, o_ref):
    @pl.when(pl.program_id(1) == 0)
    def _():
        o_ref[...] = x_ref[...]

    u = jnp.dot(h_ref[...], wu_ref[...], preferred_element_type=F32)
    a = jnp.square(jnp.maximum(u, 0.0)).astype(BF16)
    o_ref[...] += jnp.dot(a, wd_ref[...], preferred_element_type=F32)


def _mlp(x, h, w_up, w_down, tm, tf):
    t, d = x.shape
    ff = w_up.shape[1]
    tm = min(tm, t)
    return pl.pallas_call(
        _mlp_kernel,
        grid=(t // tm, ff // tf),
        in_specs=[
            pl.BlockSpec((tm, d), lambda i, j: (i, 0)),
            pl.BlockSpec((tm, d), lambda i, j: (i, 0)),
            pl.BlockSpec((d, tf), lambda i, j: (0, j)),
            pl.BlockSpec((tf, d), lambda i, j: (j, 0)),
        ],
        out_specs=pl.BlockSpec((tm, d), lambda i, j: (i, 0)),
        out_shape=jax.ShapeDtypeStruct((t, d), F32),
        compiler_params=_compiler_params(("parallel", "arbitrary")),
        name="sqrelu_mlp",
    )(x, h, w_up, w_down)


def _swa_ssd_layer(x, gain, w_in, q_norm, k_norm, sinks, conv_w, conv_b, dt_bias, a_log, d_skip, gate_norm,
                   w_out, next_gain, batch, seq):
    d_inner = gate_norm.shape[0]
    n_ssm_heads = dt_bias.shape[0]
    n_q_heads = sinks.shape[0]
    q_w = n_q_heads * HEAD_DIM
    kv_w = SWA_KV_HEADS * HEAD_DIM
    conv_dim = conv_w.shape[1]
    splits = (q_w, q_w + kv_w, q_w + 2 * kv_w, q_w + 2 * kv_w + d_inner, q_w + 2 * kv_w + d_inner + conv_dim)
    wq, wk, wv, wz, wxbc, wdt = jnp.split(w_in, splits, axis=1)
    w_main = jnp.concatenate([wz, wxbc, wq, wk, wv], axis=1).astype(BF16)
    w_dt = jnp.pad(wdt, ((0, 0), (0, LANES - n_ssm_heads))).astype(BF16)
    qkv_w = q_w + 2 * kv_w
    ssm_w = d_inner + conv_dim
    assert ssm_w % qkv_w == 0 and ssm_w % q_w == 0 and (ssm_w + q_w) % (2 * kv_w) == 0
    head_gain = jnp.concatenate([jnp.ones((ssm_w,), F32), jnp.tile(q_norm.astype(F32) * Q_SCALE, n_q_heads),
                                 jnp.tile(k_norm.astype(F32), SWA_KV_HEADS), jnp.ones((kv_w,), F32)])
    head_mask = jnp.concatenate([jnp.zeros((ssm_w,), F32), jnp.ones((q_w + kv_w,), F32), jnp.zeros((kv_w,), F32)])
    last = ssm_w // qkv_w
    proj, dt_raw = _norm_matmul_headnorm(x, gain, w_main, head_gain, head_mask, tm=1024, tn=qkv_w,
                                         name="norm_inproj", norm_tiles=(last, last + 1), w_extra=w_dt)
    attn = _swa_attention(proj, sinks, batch, seq, q_w, ssm_w // q_w, (ssm_w + q_w) // (2 * kv_w))
    y = _ssd_mixer(proj, dt_raw, conv_w, conv_b, dt_bias, a_log, d_skip, gate_norm, batch, seq)
    w_out = w_out.astype(BF16)
    return _outproj_swa_ssd(x, attn, y, w_out[:q_w], w_out[q_w:], next_gain, tm=512)


def _dilated_layer(x, gain, w_qkv, q_norm, k_norm, w_o, next_gain, batch, seq):
    width = w_o.shape[0]
    n_heads = width // HEAD_DIM
    w_qkv = w_qkv.astype(BF16)
    head_gain = jnp.concatenate([jnp.tile(q_norm.astype(F32) * Q_SCALE, n_heads),
                                 jnp.tile(k_norm.astype(F32), n_heads), jnp.ones((width,), F32)])
    head_mask = jnp.concatenate([jnp.ones((2 * width,), F32), jnp.zeros((width,), F32)])
    tn = 1024
    qkv = _norm_matmul_headnorm(x, gain, w_qkv, head_gain, head_mask, tm=1024, tn=tn, name="norm_qkv_proj",
                                norm_tiles=(0, 2 * width // tn), interleave=RESIDUES)
    outs, lses = _dilated_branches(qkv, batch, seq)
    return _outproj_dilated(x, outs, lses, w_o.astype(BF16), next_gain, tm=512)


def kernel(x, norm_mix, norm_ffn, w_up, w_down, ab_w_in, ab_q_norm, ab_k_norm, ab_sinks, ab_conv_w, ab_conv_b,
           ab_dt_bias, ab_a_log, ab_d_skip, ab_gate_norm, ab_w_out, c_w_qkv, c_q_norm, c_k_norm, c_w_o):
    batch, seq, d_model = x.shape
    depth = norm_mix.shape[0]
    assert seq % SUPERBLOCK == 0 and seq % SSM_CHUNK == 0
    h = x.reshape(batch * seq, d_model).astype(F32)
    for layer in range(depth):
        i = layer // 2
        if layer % 2 == 0:
            h, hn = _swa_ssd_layer(h, norm_mix[layer], ab_w_in[i], ab_q_norm[i], ab_k_norm[i], ab_sinks[i],
                                   ab_conv_w[i], ab_conv_b[i], ab_dt_bias[i], ab_a_log[i], ab_d_skip[i],
                                   ab_gate_norm[i], ab_w_out[i], norm_ffn[layer], batch, seq)
        else:
            h, hn = _dilated_layer(h, norm_mix[layer], c_w_qkv[i], c_q_norm[i], c_k_norm[i], c_w_o[i],
                                   norm_ffn[layer], batch, seq)
        h = _mlp(h, hn, w_up[layer].astype(BF16), w_down[layer].astype(BF16), tm=512, tf=1024)
    return h.reshape(batch, seq, d_model).astype(x.dtype)
```

```python
import functools
import math

import jax
import jax.numpy as jnp
from jax import lax
from jax.experimental import pallas as pl
from jax.experimental.pallas import tpu as pltpu

F32 = jnp.float32
BF16 = jnp.bfloat16

NORM_EPS = 1e-5
HEAD_DIM = 64
ATTN_BLOCK = 128
SWA_KV_HEADS = 2
SWA_WINDOW = 128
SSM_GROUPS = 4
SSM_STATE = 128
SSM_CONV = 4
SSM_CHUNK = 128
DIL_BRANCHES = ((128, 1), (512, 4), (2048, 16))
RESIDUES = 16
SUPERBLOCK = RESIDUES * ATTN_BLOCK
STAGE_PITCH = 20

LOG2E = math.log2(math.e)
Q_SCALE = HEAD_DIM ** -0.5 * LOG2E

LANES = 128
MXU_WIDTH = 256
CONV_HIST = 16
VMEM_LIMIT_BYTES = 56 * 1024 * 1024
VMEM_LIMIT_LARGE_BYTES = 60 * 1024 * 1024


def _compiler_params(semantics, vmem_limit_bytes=VMEM_LIMIT_BYTES):
    return pltpu.CompilerParams(dimension_semantics=semantics, vmem_limit_bytes=vmem_limit_bytes)


def _rms_norm_rows(x, gain):
    ms = jnp.mean(x * x, axis=-1, keepdims=True)
    return x * lax.rsqrt(ms + NORM_EPS) * gain


def _norm_rows_into(h_ref, x_ref, g_ref, interleave, xs_ref):
    if interleave is None:
        h_ref[...] = _rms_norm_rows(x_ref[...], g_ref[...]).astype(BF16)
        return
    n_chunks = xs_ref.shape[0]
    rows = x_ref.shape[0] // interleave
    pitch = xs_ref.shape[1] // rows
    for c in range(n_chunks):
        for k in range(rows):
            xs_ref[c, k * pitch:k * pitch + interleave, :] = x_ref[k * interleave:(k + 1) * interleave,
                                                                   c * LANES:(c + 1) * LANES]
    for r in range(interleave):
        xr = jnp.concatenate([xs_ref[c, pl.ds(r, rows, stride=pitch), :] for c in range(n_chunks)], axis=1)
        h_ref[r * rows:(r + 1) * rows, :] = _rms_norm_rows(xr, g_ref[...]).astype(BF16)


def _norm_matmul_headnorm_kernel(x_ref, g_ref, w_ref, hg_ref, hm_ref, ones_ref, *rest, interleave, has_extra,
                                 n_col_tiles, n_tiles):
    rest = list(rest)
    we_ref = rest.pop(0) if has_extra else None
    o_ref = rest.pop(0)
    extra_ref = rest.pop(0) if has_extra else None
    h_ref = rest.pop(0)
    acc_ref = rest.pop(0)
    xs_ref = rest.pop(0) if interleave is not None else None
    s = pl.program_id(0)

    @pl.when(s == 0)
    def _():
        acc_ref[...] = jnp.zeros(acc_ref.shape, F32)

    @pl.when((s % n_col_tiles == 0) & (s < n_tiles))
    def _():
        _norm_rows_into(h_ref, x_ref, g_ref, interleave, xs_ref)
        if has_extra:
            extra_ref[...] = jnp.dot(h_ref[...], we_ref[...], preferred_element_type=F32)

    chunks = []
    for c in range(acc_ref.shape[1] // MXU_WIDTH):
        cols = slice(c * MXU_WIDTH, (c + 1) * MXU_WIDTH)
        a = acc_ref[:, cols]
        ss = jnp.dot((a * a).astype(BF16), ones_ref[...], preferred_element_type=F32)
        r = lax.rsqrt(ss * (1.0 / HEAD_DIM) + NORM_EPS)
        chunks.append((a * jnp.where(hm_ref[:, cols] > 0.0, r, 1.0) * hg_ref[:, cols]).astype(o_ref.dtype))
    o_ref[...] = jnp.concatenate(chunks, axis=1).reshape(o_ref.shape)
    acc_ref[...] = jnp.dot(h_ref[...], w_ref[...], preferred_element_type=F32)


def _proj_scratch(tm, d, tn, interleave):
    scratch = [pltpu.VMEM((tm, d), BF16), pltpu.VMEM((tm, tn), F32)]
    if interleave is not None:
        scratch.append(pltpu.VMEM((d // LANES, tm // interleave * STAGE_PITCH, LANES), F32))
    return scratch


def _row_specs(t, tm, tn, interleave):
    if interleave is None:
        return (lambda n: jax.ShapeDtypeStruct((t, n), BF16)), pl.BlockSpec((tm, tn), lambda i, j: (i, j))
    per_super = SUPERBLOCK // tm
    rows = tm // interleave
    shape = lambda n: jax.ShapeDtypeStruct((t // SUPERBLOCK, interleave, SUPERBLOCK // interleave, n), BF16)
    spec = pl.BlockSpec((1, interleave, rows, tn), lambda i, j: (i // per_super, 0, i % per_super, j))
    return shape, spec


def _head_ones():
    r = lax.broadcasted_iota(jnp.int32, (MXU_WIDTH, MXU_WIDTH), 0)
    c = lax.broadcasted_iota(jnp.int32, (MXU_WIDTH, MXU_WIDTH), 1)
    return ((r // HEAD_DIM) == (c // HEAD_DIM)).astype(BF16)


def _norm_matmul_headnorm(x, gain, w, head_gain, head_mask, tm, tn, name, interleave=None, w_extra=None):
    t, d = x.shape
    n_main = w.shape[1]
    tm = min(tm, t)
    n_col_tiles = n_main // tn
    n_tiles = (t // tm) * n_col_tiles
    shape, out_spec = _row_specs(t, tm, tn, interleave)
    mm = lambda s: jnp.minimum(s, n_tiles - 1)
    ep = lambda s: jnp.maximum(s - 1, 0)
    row_of = lambda tile: tile // n_col_tiles
    col_of = lambda tile: tile % n_col_tiles
    operands = [x, gain.reshape(1, d).astype(F32), w, head_gain.reshape(1, n_main).astype(F32),
                head_mask.reshape(1, n_main).astype(F32), _head_ones()]
    in_specs = [
        pl.BlockSpec((tm, d), lambda s: (row_of(mm(s)), 0)),
        pl.BlockSpec((1, d), lambda s: (0, 0)),
        pl.BlockSpec((d, tn), lambda s: (0, col_of(mm(s)))),
        pl.BlockSpec((1, tn), lambda s: (0, col_of(ep(s)))),
        pl.BlockSpec((1, tn), lambda s: (0, col_of(ep(s)))),
        pl.BlockSpec((MXU_WIDTH, MXU_WIDTH), lambda s: (0, 0)),
    ]
    out_shapes = [shape(n_main)]
    out_specs = [pl.BlockSpec(out_spec.block_shape, lambda s: out_spec.index_map(row_of(ep(s)), col_of(ep(s))))]
    if w_extra is not None:
        n_extra = w_extra.shape[1]
        operands.append(w_extra)
        in_specs.append(pl.BlockSpec((d, n_extra), lambda s: (0, 0)))
        out_shapes.append(jax.ShapeDtypeStruct((t, n_extra), F32))
        out_specs.append(pl.BlockSpec((tm, n_extra), lambda s: (row_of(mm(s)), 0)))
    outs = pl.pallas_call(
        functools.partial(_norm_matmul_headnorm_kernel, interleave=interleave, has_extra=w_extra is not None,
                          n_col_tiles=n_col_tiles, n_tiles=n_tiles),
        grid=(n_tiles + 1,),
        in_specs=in_specs,
        out_specs=out_specs,
        out_shape=out_shapes,
        scratch_shapes=_proj_scratch(tm, d, tn, interleave),
        compiler_params=_compiler_params(("arbitrary",)),
        name=name,
    )(*operands)
    if w_extra is not None:
        return outs[0].reshape(t, n_main), outs[1]
    return outs[0].reshape(t, n_main)


def _lane_lo():
    return lax.broadcasted_iota(jnp.int32, (1, LANES), 1) < HEAD_DIM


def _attn_tiles(qs, ks, vs, masks, lane_lo, sinks=None, normalize=False):
    sel_rows = _iota((4 * ATTN_BLOCK, LANES), 0) < 2 * ATTN_BLOCK
    row_sum = (sel_rows == (_iota((4 * ATTN_BLOCK, LANES), 1) < HEAD_DIM)).astype(BF16)
    results = []
    for i, (q, k, v, mask) in enumerate(zip(qs, ks, vs, masks)):
        zq = jnp.zeros_like(q)
        q_heads = jnp.concatenate([jnp.where(lane_lo, q, zq), jnp.where(lane_lo, zq, q)], axis=0)
        s2 = lax.dot_general(q_heads, k, (((1,), (1,)), ((), ())), preferred_element_type=F32)
        ps, ms = [], []
        for idx in range(2):
            s = jnp.where(mask, s2[idx * ATTN_BLOCK:(idx + 1) * ATTN_BLOCK], -jnp.inf)
            m = jnp.max(s, axis=-1, keepdims=True)
            if sinks is not None:
                m = jnp.maximum(m, sinks[i][idx])
            ps.append(jnp.exp2(s - m).astype(BF16))
            ms.append(m)
        zv = jnp.zeros_like(v)
        v_heads = jnp.concatenate([jnp.where(lane_lo, v, zv), jnp.where(lane_lo, zv, v)], axis=0)
        res = jnp.dot(jnp.concatenate(ps, axis=1), jnp.concatenate([v_heads, row_sum], axis=1),
                      preferred_element_type=F32)
        o, l = res[:, :LANES], res[:, LANES:]
        if sinks is not None:
            l = l + jnp.where(lane_lo, jnp.exp2(sinks[i][0] - ms[0]), jnp.exp2(sinks[i][1] - ms[1]))
        if normalize:
            o = o * (1.0 / l)
        results.append((o, ms, l))
    return results


def _dist_mask(q_pos, k_pos, max_dist, key_exists=None):
    dist = q_pos - k_pos
    mask = (dist >= 0) & (dist <= max_dist)
    if key_exists is not None:
        mask = mask & key_exists
    return mask


def _iota(shape, dim):
    return lax.broadcasted_iota(jnp.int32, shape, dim)


def _swa_kernel(sinks_ref, q_ref, kvp_ref, kvc_ref, o_ref):
    first = pl.program_id(1) == 0
    lane_lo = _lane_lo()
    kj = _iota((1, 2 * ATTN_BLOCK), 1)
    mask = _dist_mask(_iota((ATTN_BLOCK, 1), 0) + ATTN_BLOCK, kj, SWA_WINDOW - 1,
                      (kj >= ATTN_BLOCK) | jnp.logical_not(first))
    kv2 = jnp.concatenate([kvp_ref[...], kvc_ref[...]], axis=0).astype(F32)

    def dup_heads(x):
        xr = pltpu.roll(x, HEAD_DIM, axis=1)
        return jnp.where(lane_lo, x, xr).astype(BF16), jnp.where(lane_lo, xr, x).astype(BF16)

    k_dup = dup_heads(kv2[:, :LANES])
    v_dup = dup_heads(kv2[:, LANES:])
    n_pairs = q_ref.shape[1] // LANES
    pairs_per_kv = n_pairs // SWA_KV_HEADS
    qs, ks, vs, masks, sinks = [], [], [], [], []
    for j in range(n_pairs):
        kh = j // pairs_per_kv
        qs.append(q_ref[:, j * LANES:(j + 1) * LANES])
        ks.append(k_dup[kh])
        vs.append(v_dup[kh])
        masks.append(mask)
        sinks.append((sinks_ref[2 * j], sinks_ref[2 * j + 1]))
    for j, (o, _, _) in enumerate(_attn_tiles(qs, ks, vs, masks, lane_lo, sinks, normalize=True)):
        o_ref[:, j * LANES:(j + 1) * LANES] = o.astype(o_ref.dtype)


def _swa_attention(qkv, sinks, batch, seq, q_width, q_block, kv_block):
    t = qkv.shape[0]
    nb = seq // ATTN_BLOCK
    return pl.pallas_call(
        _swa_kernel,
        grid=(batch, nb),
        in_specs=[
            pl.BlockSpec(memory_space=pltpu.SMEM),
            pl.BlockSpec((ATTN_BLOCK, q_width), lambda b, n: (b * nb + n, q_block)),
            pl.BlockSpec((ATTN_BLOCK, 2 * LANES), lambda b, n: (b * nb + jnp.maximum(n - 1, 0), kv_block)),
            pl.BlockSpec((ATTN_BLOCK, 2 * LANES), lambda b, n: (b * nb + n, kv_block)),
        ],
        out_specs=pl.BlockSpec((ATTN_BLOCK, q_width), lambda b, n: (b * nb + n, 0)),
        out_shape=jax.ShapeDtypeStruct((t, q_width), BF16),
        compiler_params=_compiler_params(("parallel", "arbitrary")),
        name="swa_attention",
    )(sinks.astype(F32) * LOG2E, qkv, qkv, qkv)


STAT_GROUP = 4


def _write_stats(acc, lane, pair, ms, l):
    lane_lo = lane < HEAD_DIM
    stat = jnp.where(lane % STAT_GROUP < STAT_GROUP // 2, l, jnp.where(lane_lo, ms[0], ms[1]))
    return jnp.where((lane % HEAD_DIM) // STAT_GROUP == pair, stat, acc)


def _dil16_kernel(q_ref, kp_ref, kc_ref, vp_ref, vc_ref, o_ref, lse_ref):
    first = pl.program_id(2) == 0
    n_res = q_ref.shape[1]
    n_pairs = q_ref.shape[3] // LANES
    lane = _iota((1, LANES), 1)
    lane_lo = lane < HEAD_DIM
    kj = _iota((1, 2 * ATTN_BLOCK), 1)
    mask = _dist_mask(_iota((ATTN_BLOCK, 1), 0) + ATTN_BLOCK, kj, ATTN_BLOCK,
                      (kj >= ATTN_BLOCK) | jnp.logical_not(first))

    def body(j, lse_accs):
        cols = pl.ds(pl.multiple_of(j * LANES, LANES), LANES)
        qs = [q_ref[0, r, :, cols] for r in range(n_res)]
        ks = [jnp.concatenate([kp_ref[0, r, :, cols], kc_ref[0, r, :, cols]], axis=0) for r in range(n_res)]
        vs = [jnp.concatenate([vp_ref[0, r, :, cols], vc_ref[0, r, :, cols]], axis=0) for r in range(n_res)]
        out = []
        for r, (o, ms, ls) in enumerate(_attn_tiles(qs, ks, vs, [mask] * n_res, lane_lo)):
            o_ref[0, r, :, cols] = o.astype(o_ref.dtype)
            out.append(_write_stats(lse_accs[r], lane, j, ms, ls))
        return tuple(out)

    zeros = tuple(jnp.zeros((ATTN_BLOCK, LANES), F32) for _ in range(n_res))
    for r, acc in enumerate(lax.fori_loop(0, n_pairs, body, zeros, unroll=4)):
        lse_ref[0, r] = acc


def _dil4_kernel(q_ref, kp_ref, kc_ref, vp_ref, vc_ref, o_ref, lse_ref):
    first = pl.program_id(2) == 0
    n_sub = q_ref.shape[1]
    rows = ATTN_BLOCK // n_sub
    n_tiles = q_ref.shape[3] // rows
    n_pairs = q_ref.shape[4] // LANES
    lane = _iota((1, LANES), 1)
    lane_lo = lane < HEAD_DIM
    qi = _iota((ATTN_BLOCK, 1), 0)
    kj = _iota((1, 2 * ATTN_BLOCK), 1)
    q_pos = n_sub * (rows + qi % rows) + qi // rows
    k_pos = n_sub * (kj % (2 * rows)) + kj // (2 * rows)
    mask = _dist_mask(q_pos, k_pos, ATTN_BLOCK)
    mask_first = mask & ((kj % (2 * rows) >= rows) | jnp.logical_not(first))

    def gather(cur_ref, prev_ref, tile, cols):
        if tile == 0:
            parts = [jnp.concatenate([prev_ref[0, a, 0, :, cols], cur_ref[0, a, 0, 0:rows, cols]], axis=0)
                     for a in range(n_sub)]
        else:
            parts = [cur_ref[0, a, 0, (tile - 1) * rows:(tile + 1) * rows, cols] for a in range(n_sub)]
        return jnp.concatenate(parts, axis=0)

    def body(j, lse_accs):
        cols = pl.ds(pl.multiple_of(j * LANES, LANES), LANES)
        qs = [jnp.concatenate([q_ref[0, a, 0, t * rows:(t + 1) * rows, cols] for a in range(n_sub)], axis=0)
              for t in range(n_tiles)]
        ks = [gather(kc_ref, kp_ref, t, cols) for t in range(n_tiles)]
        vs = [gather(vc_ref, vp_ref, t, cols) for t in range(n_tiles)]
        masks = [mask_first] + [mask] * (n_tiles - 1)
        out = []
        for t, (o, ms, ls) in enumerate(_attn_tiles(qs, ks, vs, masks, lane_lo)):
            ob = o.astype(o_ref.dtype)
            for a in range(n_sub):
                o_ref[0, a, 0, t * rows:(t + 1) * rows, cols] = ob[a * rows:(a + 1) * rows, :]
            out.append(_write_stats(lse_accs[t], lane, j, ms, ls))
        return tuple(out)

    zeros = tuple(jnp.zeros((ATTN_BLOCK, LANES), F32) for _ in range(n_tiles))
    for t, acc in enumerate(lax.fori_loop(0, n_pairs, body, zeros, unroll=4)):
        for a in range(n_sub):
            lse_ref[0, a, 0, t * rows:(t + 1) * rows, :] = acc[a * rows:(a + 1) * rows, :]


def _dil1_kernel(q_ref, kp_ref, kc_ref, vp_ref, vc_ref, o_ref, lse_ref):
    first = (pl.program_id(1) == 0) & (pl.program_id(2) == 0)
    n_res = q_ref.shape[1]
    sub = ATTN_BLOCK // n_res
    n_tiles = q_ref.shape[2] // sub
    n_pairs = q_ref.shape[3] // LANES
    prev_rows = kp_ref.shape[2]
    lane = _iota((1, LANES), 1)
    lane_lo = lane < HEAD_DIM
    qi = _iota((ATTN_BLOCK, 1), 0)
    kj = _iota((1, 2 * ATTN_BLOCK), 1)
    q_pos = n_res * (sub + qi % sub) + qi // sub
    k_pos = n_res * (kj % (2 * sub)) + kj // (2 * sub)
    mask = _dist_mask(q_pos, k_pos, ATTN_BLOCK)
    mask_first = mask & ((kj % (2 * sub) >= sub) | jnp.logical_not(first))

    def tile_rows(x, lo, hi):
        return x[:, lo:hi, :].reshape(n_res * (hi - lo), LANES)

    def body(j, lse_accs):
        cols = pl.ds(pl.multiple_of(j * LANES, LANES), LANES)
        q = q_ref[0, :, :, cols].astype(F32)
        qs = [tile_rows(q, t * sub, (t + 1) * sub).astype(BF16) for t in range(n_tiles)]

        def gather(cur_ref, prev_ref):
            cur = cur_ref[0, :, :, cols].astype(F32)
            prev = prev_ref[0, :, :, cols].astype(F32)
            tiles = []
            for t in range(n_tiles):
                if t == 0:
                    x = jnp.concatenate([prev[:, prev_rows - sub:, :], cur[:, 0:sub, :]], axis=1)
                else:
                    x = cur[:, (t - 1) * sub:(t + 1) * sub, :]
                tiles.append(x.reshape(n_res * 2 * sub, LANES).astype(BF16))
            return tiles

        ks = gather(kc_ref, kp_ref)
        vs = gather(vc_ref, vp_ref)
        masks = [mask_first] + [mask] * (n_tiles - 1)
        res = _attn_tiles(qs, ks, vs, masks, lane_lo)
        o_full = jnp.concatenate([o.reshape(n_res, sub, LANES) for o, _, _ in res], axis=1)
        o_ref[0, :, :, cols] = o_full.astype(o_ref.dtype)
        return tuple(_write_stats(lse_accs[t], lane, j, ms, ls) for t, (_, ms, ls) in enumerate(res))

    zeros = tuple(jnp.zeros((ATTN_BLOCK, LANES), F32) for _ in range(n_tiles))
    accs = lax.fori_loop(0, n_pairs, body, zeros, unroll=4)
    lse_ref[0] = jnp.concatenate([acc.reshape(n_res, sub, LANES) for acc in accs], axis=1)


def _dilated_branches(qkv, batch, seq):
    t, w = qkv.shape[0], qkv.shape[1] // 3
    nsb = seq // SUPERBLOCK
    n_super = t // SUPERBLOCK
    per_res = SUPERBLOCK // RESIDUES
    outs, lses = [], []

    def in_specs(block, cur_map, prev_block, prev_map):
        cur = lambda col: pl.BlockSpec(block, lambda *g: cur_map(*g) + (col,))
        prev = lambda col: pl.BlockSpec(prev_block, lambda *g: prev_map(*g) + (col,))
        return [cur(0), prev(1), cur(1), prev(2), cur(2)]

    r16 = 4
    shp = (n_super, RESIDUES, per_res)
    cur_map = lambda b, r, s: (b * nsb + s, r, 0)
    prev_map = lambda b, r, s: (b * nsb + jnp.maximum(s - 1, 0), r, 0)
    qkv4 = qkv.reshape(shp + (3 * w,))
    o16, l16 = pl.pallas_call(
        _dil16_kernel, grid=(batch, RESIDUES // r16, nsb),
        in_specs=in_specs((1, r16, per_res, w), cur_map, (1, r16, per_res, w), prev_map),
        out_specs=[pl.BlockSpec((1, r16, per_res, w), lambda *g: cur_map(*g) + (0,)),
                   pl.BlockSpec((1, r16, per_res, LANES), lambda *g: cur_map(*g) + (0,))],
        out_shape=[jax.ShapeDtypeStruct(shp + (w,), BF16), jax.ShapeDtypeStruct(shp + (LANES,), F32)],
        compiler_params=_compiler_params(("parallel", "parallel", "arbitrary")),
        name="dilated_attention_d16",
    )(qkv4, qkv4, qkv4, qkv4, qkv4)

    shp5 = (n_super, 4, 4, per_res)
    tail = per_res // 4
    cur_map5 = lambda b, r, s: (b * nsb + s, 0, r, 0)
    prev_map5 = lambda b, r, s: (b * nsb + jnp.maximum(s - 1, 0), 0, r, per_res // tail - 1)
    qkv5 = qkv.reshape(shp5 + (3 * w,))
    o4, l4 = pl.pallas_call(
        _dil4_kernel, grid=(batch, 4, nsb),
        in_specs=in_specs((1, 4, 1, per_res, w), cur_map5, (1, 4, 1, tail, w), prev_map5),
        out_specs=[pl.BlockSpec((1, 4, 1, per_res, w), lambda *g: cur_map5(*g) + (0,)),
                   pl.BlockSpec((1, 4, 1, per_res, LANES), lambda *g: cur_map5(*g) + (0,))],
        out_shape=[jax.ShapeDtypeStruct(shp5 + (w,), BF16), jax.ShapeDtypeStruct(shp5 + (LANES,), F32)],
        compiler_params=_compiler_params(("parallel", "parallel", "arbitrary")),
        name="dilated_attention_d4",
    )(qkv5, qkv5, qkv5, qkv5, qkv5)

    run, hist = 32, 16
    runs = per_res // run
    cur_map1 = lambda b, s, c: (b * nsb + s, 0, c)

    def prev_map1(b, s, c):
        g = (b * nsb + s) * (per_res // hist) + c * (run // hist) - 1
        g = jnp.maximum(g, 0)
        return (g // (per_res // hist), 0, g % (per_res // hist))

    o1, l1 = pl.pallas_call(
        _dil1_kernel, grid=(batch, nsb, runs),
        in_specs=in_specs((1, RESIDUES, run, w), cur_map1, (1, RESIDUES, hist, w), prev_map1),
        out_specs=[pl.BlockSpec((1, RESIDUES, run, w), lambda *g: cur_map1(*g) + (0,)),
                   pl.BlockSpec((1, RESIDUES, run, LANES), lambda *g: cur_map1(*g) + (0,))],
        out_shape=[jax.ShapeDtypeStruct(shp + (w,), BF16), jax.ShapeDtypeStruct(shp + (LANES,), F32)],
        compiler_params=_compiler_params(("parallel", "arbitrary", "arbitrary")),
        name="dilated_attention_d1",
    )(qkv4, qkv4, qkv4, qkv4, qkv4)

    for o, l in ((o1, l1), (o4, l4), (o16, l16)):
        outs.append(o.reshape(t, w))
        lses.append(l.reshape(t, LANES))
    return outs, lses


def _softplus(x):
    return jnp.maximum(x, 0.0) + jnp.log1p(jnp.exp(-jnp.abs(x)))


def _silu(x):
    h = 0.5 * x
    return h * jnp.tanh(h) + h


def _expand_heads(v, head_mask, e3):
    n_heads = e3.shape[1] // HEAD_DIM
    v = jnp.where(head_mask, v, 0.0)
    hi = v.astype(BF16).astype(F32)
    r = v - hi
    mid = r.astype(BF16).astype(F32)
    lo = (r - mid).astype(BF16).astype(F32)
    cat = hi + pltpu.roll(mid, n_heads, axis=1) + pltpu.roll(lo, 2 * n_heads, axis=1)
    return jnp.dot(cat.astype(BF16), e3, preferred_element_type=F32)


def _ssd_kernel(xs_ref, bc_ref, z_ref, dt_ref, cw_ref, cb_ref, dtb_ref, a_ref, dskip_ref, gn_ref, tril_ref, e3_ref,
                shift_ref, y_ref, hist_ref, state_ref):
    chunk = SSM_CHUNK
    d_inner = z_ref.shape[1]
    n_heads = d_inner // HEAD_DIM
    group_w = d_inner // SSM_GROUPS
    heads_per_group = n_heads // SSM_GROUPS

    @pl.when(pl.program_id(1) == 0)
    def _():
        hist_ref[...] = jnp.zeros(hist_ref.shape, hist_ref.dtype)
        state_ref[...] = jnp.zeros(state_ref.shape, F32)

    parts = []
    for ref, cols in ((xs_ref, slice(0, d_inner)), (bc_ref, slice(d_inner, hist_ref.shape[1]))):
        cur = ref[...]
        ext = jnp.concatenate([hist_ref[:, cols], cur], axis=0)
        shifted = jnp.dot(shift_ref[...], ext, preferred_element_type=F32)
        conv = cb_ref[:, cols] + cw_ref[SSM_CONV - 1:SSM_CONV, cols] * cur.astype(F32)
        for j in range(SSM_CONV - 1):
            conv = conv + cw_ref[j:j + 1, cols] * shifted[j * chunk:(j + 1) * chunk]
        hist_ref[:, cols] = cur[chunk - CONV_HIST:, :]
        parts.append(_silu(conv))
    xs = parts[0]
    xbc_bc = parts[1]

    lane = _iota((1, LANES), 1)
    head_mask = lane < n_heads
    lane_lo = lane < HEAD_DIM
    e3 = e3_ref[...]

    dt = jnp.where(head_mask, _softplus(dt_ref[...] + dtb_ref[...]), 0.0)
    cum = jnp.dot(tril_ref[...], dt * a_ref[...], precision=lax.Precision.HIGHEST,
                  preferred_element_type=F32)
    cum_t = cum.T
    ecum_e = _expand_heads(jnp.exp(cum), head_mask, e3)
    dte_e = _expand_heads(jnp.exp(cum[chunk - 1:chunk, :] - cum), head_mask, e3)
    xdt = xs * _expand_heads(dt, head_mask, e3)
    xend = xdt * dte_e

    causal = _iota((chunk, chunk), 0) >= _iota((chunk, chunk), 1)

    for g in range(SSM_GROUPS):
        gcols = slice(g * group_w, (g + 1) * group_w)
        b_f32 = xbc_bc[:, g * SSM_STATE: (g + 1) * SSM_STATE]
        c_off = SSM_GROUPS * SSM_STATE
        c_bf = xbc_bc[:, c_off + g * SSM_STATE: c_off + (g + 1) * SSM_STATE].astype(BF16)
        cb = lax.dot_general(c_bf, b_f32.astype(BF16), (((1,), (1,)), ((), ())), preferred_element_type=F32)
        st = state_ref[g]
        y_off = jnp.dot(c_bf, st.astype(BF16), preferred_element_type=F32) * ecum_e[:, gcols]
        state_ref[g] = st * ecum_e[chunk - 1:chunk, gcols] + jnp.dot(
            b_f32.T.astype(BF16), xend[:, gcols].astype(BF16), preferred_element_type=F32)

        y_pairs = []
        for pair in range(heads_per_group // 2):
            h0 = g * heads_per_group + 2 * pair
            ms = []
            for h in (h0, h0 + 1):
                diff = cum[:, h:h + 1] - cum_t[h:h + 1, :]
                seg = jnp.exp(jnp.where(causal, diff, -jnp.inf))
                ms.append((cb * seg).astype(BF16))
            xp = xdt[:, h0 * HEAD_DIM: h0 * HEAD_DIM + LANES]
            x_heads = jnp.concatenate([jnp.where(lane_lo, xp, 0.0), jnp.where(lane_lo, 0.0, xp)], axis=0)
            y_pairs.append(jnp.dot(jnp.concatenate(ms, axis=1), x_heads.astype(BF16),
                                   preferred_element_type=F32))
        y = jnp.concatenate(y_pairs, axis=1) + y_off + dskip_ref[:, gcols] * xs[:, gcols]
        gy = y * _silu(z_ref[:, gcols].astype(F32))
        ms_g = jnp.mean(gy * gy, axis=-1, keepdims=True)
        y_ref[:, gcols] = (gy * lax.rsqrt(ms_g + NORM_EPS) * gn_ref[:, gcols]).astype(y_ref.dtype)


def _ssd_mixer(proj, dt_raw, conv_w, conv_b, dt_bias, a_log, d_skip, gate_norm, batch, seq):
    t = proj.shape[0]
    d_inner = gate_norm.shape[0]
    n_heads = d_inner // HEAD_DIM
    conv_dim = conv_w.shape[1]
    bc_w = conv_dim - d_inner
    nc = seq // SSM_CHUNK
    assert d_inner % bc_w == 0

    def pad_heads(v):
        return jnp.zeros((1, LANES), F32).at[0, :n_heads].set(v.astype(F32))

    tril = (_iota((SSM_CHUNK, SSM_CHUNK), 0) >= _iota((SSM_CHUNK, SSM_CHUNK), 1)).astype(F32)
    row = _iota((LANES, d_inner), 0)
    col = _iota((LANES, d_inner), 1)
    e3 = ((row < 3 * n_heads) & ((row % n_heads) == (col // HEAD_DIM))).astype(BF16)
    srow = _iota(((SSM_CONV - 1) * SSM_CHUNK, CONV_HIST + SSM_CHUNK), 0)
    scol = _iota(((SSM_CONV - 1) * SSM_CHUNK, CONV_HIST + SSM_CHUNK), 1)
    shift = (scol == CONV_HIST + srow % SSM_CHUNK - (SSM_CONV - 1 - srow // SSM_CHUNK)).astype(BF16)
    const = lambda shape: pl.BlockSpec(shape, lambda b, c: (0, 0))
    return pl.pallas_call(
        _ssd_kernel,
        grid=(batch, nc),
        in_specs=[
            pl.BlockSpec((SSM_CHUNK, d_inner), lambda b, c: (b * nc + c, 1)),
            pl.BlockSpec((SSM_CHUNK, bc_w), lambda b, c: (b * nc + c, 2 * d_inner // bc_w)),
            pl.BlockSpec((SSM_CHUNK, d_inner), lambda b, c: (b * nc + c, 0)),
            pl.BlockSpec((SSM_CHUNK, LANES), lambda b, c: (b * nc + c, 0)),
            const((SSM_CONV, conv_dim)), const((1, conv_dim)), const((1, LANES)), const((1, LANES)),
            const((1, d_inner)), const((1, d_inner)), const((SSM_CHUNK, SSM_CHUNK)), const((LANES, d_inner)),
            const(shift.shape),
        ],
        out_specs=pl.BlockSpec((SSM_CHUNK, d_inner), lambda b, c: (b * nc + c, 0)),
        out_shape=jax.ShapeDtypeStruct((t, d_inner), BF16),
        scratch_shapes=[pltpu.VMEM((CONV_HIST, conv_dim), BF16),
                        pltpu.VMEM((SSM_GROUPS, SSM_STATE, d_inner // SSM_GROUPS), F32)],
        compiler_params=_compiler_params(("parallel", "arbitrary")),
        name="ssd_mixer",
    )(proj, proj, proj, dt_raw, conv_w.astype(F32), conv_b.reshape(1, conv_dim).astype(F32),
      pad_heads(dt_bias), pad_heads(-jnp.exp(a_log.astype(F32))),
      jnp.repeat(d_skip.astype(F32), HEAD_DIM).reshape(1, d_inner), gate_norm.reshape(1, d_inner).astype(F32),
      tril, e3, shift)


def _outproj_swa_ssd_kernel(x_ref, a_ref, y_ref, wa_ref, wy_ref, g_ref, o_ref, h_ref):
    out = (x_ref[...]
           + jnp.dot(a_ref[...], wa_ref[...], preferred_element_type=F32)
           + jnp.dot(y_ref[...], wy_ref[...], preferred_element_type=F32))
    o_ref[...] = out
    h_ref[...] = _rms_norm_rows(out, g_ref[...]).astype(BF16)


def _outproj_swa_ssd(x, attn, y, w_attn, w_y, next_gain, tm):
    t, d = x.shape
    tm = min(tm, t)
    resident = lambda shape: pl.BlockSpec(shape, lambda i: (0, 0), pipeline_mode=pl.Buffered(1))
    return pl.pallas_call(
        _outproj_swa_ssd_kernel,
        grid=(t // tm,),
        in_specs=[
            pl.BlockSpec((tm, d), lambda i: (i, 0)),
            pl.BlockSpec((tm, attn.shape[1]), lambda i: (i, 0)),
            pl.BlockSpec((tm, y.shape[1]), lambda i: (i, 0)),
            resident((attn.shape[1], d)),
            resident((y.shape[1], d)),
            pl.BlockSpec((1, d), lambda i: (0, 0)),
        ],
        out_specs=[pl.BlockSpec((tm, d), lambda i: (i, 0)), pl.BlockSpec((tm, d), lambda i: (i, 0))],
        out_shape=[jax.ShapeDtypeStruct((t, d), F32), jax.ShapeDtypeStruct((t, d), BF16)],
        compiler_params=_compiler_params(("parallel",)),
        name="outproj_swa_ssd",
    )(x, attn, y, w_attn, w_y, next_gain.reshape(1, d).astype(F32))


def _outproj_dilated_kernel(x_ref, o1_ref, o2_ref, o3_ref, l1_ref, l2_ref, l3_ref, e2_ref, w_ref, g_ref,
                            out_ref, h_ref, om_ref, scr_ref):
    def merge_branches():
        tm, width = om_ref.shape
        n_res = o1_ref.shape[1]
        lane = _iota((1, LANES), 1)
        head_mask = lane % STAT_GROUP == 0
        stats = [l_ref[...].reshape(tm, LANES) for l_ref in (l1_ref, l2_ref, l3_ref)]
        maxes = [pltpu.roll(st, LANES - STAT_GROUP // 2, axis=1) for st in stats]
        m = jnp.maximum(jnp.maximum(maxes[0], maxes[1]), maxes[2])
        es = [jnp.exp2(mx - m) for mx in maxes]
        den = sum(e * st for e, st in zip(es, stats))
        inv = 1.0 / jnp.where(head_mask, den, 1.0)
        acc = None
        for e, o_ref in zip(es, (o1_ref, o2_ref, o3_ref)):
            w = jnp.where(head_mask, e * inv, 0.0)
            hi = w.astype(BF16).astype(F32)
            lo = (w - hi).astype(BF16).astype(F32)
            cat = (hi + pltpu.roll(lo, 1, axis=1)).astype(BF16)
            term = jnp.dot(cat, e2_ref[...], preferred_element_type=F32) * o_ref[...].reshape(tm, width).astype(F32)
            acc = term if acc is None else acc + term
        rows = tm // n_res
        pitch = scr_ref.shape[1] // n_res
        for c in range(width // LANES):
            for r in range(n_res):
                scr_ref[c, r * pitch:r * pitch + rows, :] = acc[r * rows:(r + 1) * rows, c * LANES:(c + 1) * LANES]
        for n in range(rows):
            om_ref[n * n_res:(n + 1) * n_res, :] = jnp.concatenate(
                [scr_ref[c, pl.ds(n, n_res, stride=pitch), :] for c in range(width // LANES)], axis=1).astype(BF16)

    merge_branches()
    out = x_ref[...] + jnp.dot(om_ref[...], w_ref[...], preferred_element_type=F32)
    out_ref[...] = out
    h_ref[...] = _rms_norm_rows(out, g_ref[...]).astype(BF16)


def _outproj_dilated(x, outs, lses, w_o, next_gain, tm):
    t, d = x.shape
    width = w_o.shape[0]
    n_heads = width // HEAD_DIM
    tm = min(tm, t)
    rows = tm // RESIDUES
    per_super = SUPERBLOCK // tm
    row = _iota((LANES, width), 0)
    col = _iota((LANES, width), 1)
    stat_head = 2 * ((row % HEAD_DIM) // STAT_GROUP) + row // HEAD_DIM
    e2 = ((row % STAT_GROUP < 2) & (stat_head == col // HEAD_DIM)).astype(BF16)
    assert n_heads * STAT_GROUP // 2 <= HEAD_DIM
    shp = (t // SUPERBLOCK, RESIDUES, SUPERBLOCK // RESIDUES)
    o_spec = pl.BlockSpec((1, RESIDUES, rows, width), lambda i: (i // per_super, 0, i % per_super, 0))
    l_spec = pl.BlockSpec((1, RESIDUES, rows, LANES), lambda i: (i // per_super, 0, i % per_super, 0))
    row_spec = pl.BlockSpec((tm, d), lambda i: (i, 0))
    return pl.pallas_call(
        _outproj_dilated_kernel,
        grid=(t // tm,),
        in_specs=[row_spec, o_spec, o_spec, o_spec, l_spec, l_spec, l_spec,
                  pl.BlockSpec((LANES, width), lambda i: (0, 0)),
                  pl.BlockSpec((width, d), lambda i: (0, 0), pipeline_mode=pl.Buffered(1)),
                  pl.BlockSpec((1, d), lambda i: (0, 0))],
        out_specs=[row_spec, row_spec],
        out_shape=[jax.ShapeDtypeStruct((t, d), F32), jax.ShapeDtypeStruct((t, d), BF16)],
        scratch_shapes=[pltpu.VMEM((tm, width), BF16),
                        pltpu.VMEM((width // LANES, RESIDUES * (rows + STAGE_PITCH - RESIDUES), LANES), F32)],
        compiler_params=_compiler_params(("parallel",), VMEM_LIMIT_LARGE_BYTES),
        name="outproj_dilated",
    )(x, *[o.reshape(shp + (width,)) for o in outs], *[l.reshape(shp + (LANES,)) for l in lses], e2, w_o,
      next_gain.reshape(1, d).astype(F32))


def _mlp_kernel(x_ref, h_ref, wu_ref, wd_ref, o_ref):
    @pl.when(pl.program_id(1) == 0)
    def _():
        o_ref[...] = x_ref[...]

    u = jnp.dot(h_ref[...], wu_ref[...], preferred_element_type=F32)
    a = jnp.square(jnp.maximum(u, 0.0)).astype(BF16)
    o_ref[...] += jnp.dot(a, wd_ref[...], preferred_element_type=F32)


def _mlp(x, h, w_up, w_down, tm, tf):
    t, d = x.shape
    ff = w_up.shape[1]
    tm = min(tm, t)
    return pl.pallas_call(
        _mlp_kernel,
        grid=(t // tm, ff // tf),
        in_specs=[
            pl.BlockSpec((tm, d), lambda i, j: (i, 0)),
            pl.BlockSpec((tm, d), lambda i, j: (i, 0)),
            pl.BlockSpec((d, tf), lambda i, j: (0, j)),
            pl.BlockSpec((tf, d), lambda i, j: (j, 0)),
        ],
        out_specs=pl.BlockSpec((tm, d), lambda i, j: (i, 0)),
        out_shape=jax.ShapeDtypeStruct((t, d), F32),
        compiler_params=_compiler_params(("parallel", "arbitrary")),
        name="sqrelu_mlp",
    )(x, h, w_up, w_down)


def _swa_ssd_layer(x, gain, w_in, q_norm, k_norm, sinks, conv_w, conv_b, dt_bias, a_log, d_skip, gate_norm,
                   w_out, next_gain, batch, seq):
    d_inner = gate_norm.shape[0]
    n_ssm_heads = dt_bias.shape[0]
    n_q_heads = sinks.shape[0]
    q_w = n_q_heads * HEAD_DIM
    kv_w = SWA_KV_HEADS * HEAD_DIM
    conv_dim = conv_w.shape[1]
    splits = (q_w, q_w + kv_w, q_w + 2 * kv_w, q_w + 2 * kv_w + d_inner, q_w + 2 * kv_w + d_inner + conv_dim)
    wq, wk, wv, wz, wxbc, wdt = jnp.split(w_in, splits, axis=1)
    w_main = jnp.concatenate([wz, wxbc, wq, wk, wv], axis=1).astype(BF16)
    w_dt = jnp.pad(wdt, ((0, 0), (0, LANES - n_ssm_heads))).astype(BF16)
    qkv_w = q_w + 2 * kv_w
    ssm_w = d_inner + conv_dim
    assert ssm_w % qkv_w == 0 and ssm_w % q_w == 0 and (ssm_w + q_w) % (2 * kv_w) == 0
    head_gain = jnp.concatenate([jnp.ones((ssm_w,), F32), jnp.tile(q_norm.astype(F32) * Q_SCALE, n_q_heads),
                                 jnp.tile(k_norm.astype(F32), SWA_KV_HEADS), jnp.ones((kv_w,), F32)])
    head_mask = jnp.concatenate([jnp.zeros((ssm_w,), F32), jnp.ones((q_w + kv_w,), F32), jnp.zeros((kv_w,), F32)])
    proj, dt_raw = _norm_matmul_headnorm(x, gain, w_main, head_gain, head_mask, tm=1024, tn=qkv_w,
                                         name="norm_inproj", w_extra=w_dt)
    attn = _swa_attention(proj, sinks, batch, seq, q_w, ssm_w // q_w, (ssm_w + q_w) // (2 * kv_w))
    y = _ssd_mixer(proj, dt_raw, conv_w, conv_b, dt_bias, a_log, d_skip, gate_norm, batch, seq)
    w_out = w_out.astype(BF16)
    return _outproj_swa_ssd(x, attn, y, w_out[:q_w], w_out[q_w:], next_gain, tm=512)


def _dilated_layer(x, gain, w_qkv, q_norm, k_norm, w_o, next_gain, batch, seq):
    width = w_o.shape[0]
    n_heads = width // HEAD_DIM
    w_qkv = w_qkv.astype(BF16)
    head_gain = jnp.concatenate([jnp.tile(q_norm.astype(F32) * Q_SCALE, n_heads),
                                 jnp.tile(k_norm.astype(F32), n_heads), jnp.ones((width,), F32)])
    head_mask = jnp.concatenate([jnp.ones((2 * width,), F32), jnp.zeros((width,), F32)])
    qkv = _norm_matmul_headnorm(x, gain, w_qkv, head_gain, head_mask, tm=1024, tn=1024, name="norm_qkv_proj",
                                interleave=RESIDUES)
    outs, lses = _dilated_branches(qkv, batch, seq)
    return _outproj_dilated(x, outs, lses, w_o.astype(BF16), next_gain, tm=512)


def kernel(x, norm_mix, norm_ffn, w_up, w_down, ab_w_in, ab_q_norm, ab_k_norm, ab_sinks, ab_conv_w, ab_conv_b,
           ab_dt_bias, ab_a_log, ab_d_skip, ab_gate_norm, ab_w_out, c_w_qkv, c_q_norm, c_k_norm, c_w_o):
    batch, seq, d_model = x.shape
    depth = norm_mix.shape[0]
    assert seq % SUPERBLOCK == 0 and seq % SSM_CHUNK == 0
    h = x.reshape(batch * seq, d_model).astype(F32)
    for layer in range(depth):
        i = layer // 2
        if layer % 2 == 0:
            h, hn = _swa_ssd_layer(h, norm_mix[layer], ab_w_in[i], ab_q_norm[i], ab_k_norm[i], ab_sinks[i],
                                   ab_conv_w[i], ab_conv_b[i], ab_dt_bias[i], ab_a_log[i], ab_d_skip[i],
                                   ab_gate_norm[i], ab_w_out[i], norm_ffn[layer], batch, seq)
        else:
            h, hn = _dilated_layer(h, norm_mix[layer], c_w_qkv[i], c_q_norm[i], c_k_norm[i], c_w_o[i],
                                   norm_ffn[layer], batch, seq)
        h = _mlp(h, hn, w_up[layer].astype(BF16), w_down[layer].astype(BF16), tm=512, tf=1024)
    return h.reshape(batch, seq, d_model).astype(x.dtype)
```

```python
import functools
import math

import jax
import jax.numpy as jnp
from jax import lax
from jax.experimental import pallas as pl
from jax.experimental.pallas import tpu as pltpu

F32 = jnp.float32
BF16 = jnp.bfloat16

NORM_EPS = 1e-5
HEAD_DIM = 64
ATTN_BLOCK = 128
SWA_KV_HEADS = 2
SWA_WINDOW = 128
SSM_GROUPS = 4
SSM_STATE = 128
SSM_CONV = 4
SSM_CHUNK = 128
DIL_BRANCHES = ((128, 1), (512, 4), (2048, 16))
RESIDUES = 16
SUPERBLOCK = RESIDUES * ATTN_BLOCK
STAGE_PITCH = 20

LOG2E = math.log2(math.e)
Q_SCALE = HEAD_DIM ** -0.5 * LOG2E

LANES = 128
MXU_WIDTH = 256
CONV_HIST = 16
VMEM_LIMIT_BYTES = 56 * 1024 * 1024


def _compiler_params(semantics):
    return pltpu.CompilerParams(dimension_semantics=semantics, vmem_limit_bytes=VMEM_LIMIT_BYTES)


def _col_tiles(w, tn):
    k, n = w.shape
    return w.reshape(k, n // tn, tn).transpose(1, 0, 2)


def _col_tile_spec(k, tn):
    return pl.BlockSpec((None, k, tn), lambda i, j: (j, 0, 0))


def _rms_norm_rows(x, gain):
    ms = jnp.mean(x * x, axis=-1, keepdims=True)
    return x * lax.rsqrt(ms + NORM_EPS) * gain


def _norm_rows_into(h_ref, x_ref, g_ref, interleave, xs_ref):
    if interleave is None:
        h_ref[...] = _rms_norm_rows(x_ref[...], g_ref[...]).astype(BF16)
        return
    n_chunks = xs_ref.shape[0]
    rows = x_ref.shape[0] // interleave
    pitch = xs_ref.shape[1] // rows
    for c in range(n_chunks):
        for k in range(rows):
            xs_ref[c, k * pitch:k * pitch + interleave, :] = x_ref[k * interleave:(k + 1) * interleave,
                                                                   c * LANES:(c + 1) * LANES]
    for r in range(interleave):
        xr = jnp.concatenate([xs_ref[c, pl.ds(r, rows, stride=pitch), :] for c in range(n_chunks)], axis=1)
        h_ref[r * rows:(r + 1) * rows, :] = _rms_norm_rows(xr, g_ref[...]).astype(BF16)


def _norm_matmul_kernel(x_ref, g_ref, w_ref, o_ref, h_ref, xs_ref=None, *, interleave):
    @pl.when(pl.program_id(1) == 0)
    def _():
        _norm_rows_into(h_ref, x_ref, g_ref, interleave, xs_ref)

    acc = jnp.dot(h_ref[...], w_ref[...], preferred_element_type=F32)
    o_ref[...] = acc.astype(o_ref.dtype).reshape(o_ref.shape)


def _norm_matmul_headnorm_kernel(x_ref, g_ref, w_ref, hg_ref, hm_ref, ones_ref, *rest, interleave, n_extra,
                                 norm_tiles):
    rest = list(rest)
    o_ref = rest.pop(0)
    extra_ref = rest.pop(0) if n_extra else None
    h_ref = rest.pop(0)
    xs_ref = rest.pop(0) if interleave is not None else None
    j = pl.program_id(1)

    @pl.when(j == 0)
    def _():
        _norm_rows_into(h_ref, x_ref, g_ref, interleave, xs_ref)

    acc = jnp.dot(h_ref[...], w_ref[...], preferred_element_type=F32)
    n_main = acc.shape[1] - n_extra

    def head_norm_store():
        chunks = []
        for c in range(n_main // MXU_WIDTH):
            cols = slice(c * MXU_WIDTH, (c + 1) * MXU_WIDTH)
            a = acc[:, cols]
            ss = jnp.dot((a * a).astype(BF16), ones_ref[...], preferred_element_type=F32)
            r = lax.rsqrt(ss * (1.0 / HEAD_DIM) + NORM_EPS)
            chunks.append((a * jnp.where(hm_ref[:, cols] > 0.0, r, 1.0) * hg_ref[:, cols]).astype(o_ref.dtype))
        o_ref[...] = jnp.concatenate(chunks, axis=1).reshape(o_ref.shape)

    if norm_tiles is None:
        head_norm_store()
    else:
        pl.when(j < norm_tiles)(head_norm_store)

        @pl.when(j >= norm_tiles)
        def _():
            o_ref[...] = acc.astype(o_ref.dtype).reshape(o_ref.shape)
    if n_extra:
        extra_ref[...] = acc[:, n_main:]


def _proj_scratch(tm, d, interleave):
    scratch = [pltpu.VMEM((tm, d), BF16)]
    if interleave is not None:
        scratch.append(pltpu.VMEM((d // LANES, tm // interleave * STAGE_PITCH, LANES), F32))
    return scratch


def _row_specs(t, tm, tn, interleave):
    if interleave is None:
        return (lambda n: jax.ShapeDtypeStruct((t, n), BF16)), pl.BlockSpec((tm, tn), lambda i, j: (i, j))
    per_super = SUPERBLOCK // tm
    rows = tm // interleave
    shape = lambda n: jax.ShapeDtypeStruct((t // SUPERBLOCK, interleave, SUPERBLOCK // interleave, n), BF16)
    spec = pl.BlockSpec((1, interleave, rows, tn), lambda i, j: (i // per_super, 0, i % per_super, j))
    return shape, spec


def _norm_matmul(x, gain, w, tm, tn, name, interleave=None):
    t, d = x.shape
    n = w.shape[1]
    tm = min(tm, t)
    shape, out_spec = _row_specs(t, tm, tn, interleave)
    out = pl.pallas_call(
        functools.partial(_norm_matmul_kernel, interleave=interleave),
        grid=(t // tm, n // tn),
        in_specs=[
            pl.BlockSpec((tm, d), lambda i, j: (i, 0)),
            pl.BlockSpec((1, d), lambda i, j: (0, 0)),
            _col_tile_spec(d, tn),
        ],
        out_specs=out_spec,
        out_shape=shape(n),
        scratch_shapes=_proj_scratch(tm, d, interleave),
        compiler_params=_compiler_params(("parallel", "arbitrary")),
        name=name,
    )(x, gain.reshape(1, d).astype(F32), _col_tiles(w, tn))
    return out.reshape(t, n)


def _head_ones():
    r = lax.broadcasted_iota(jnp.int32, (MXU_WIDTH, MXU_WIDTH), 0)
    c = lax.broadcasted_iota(jnp.int32, (MXU_WIDTH, MXU_WIDTH), 1)
    return ((r // HEAD_DIM) == (c // HEAD_DIM)).astype(BF16)


def _norm_matmul_headnorm(x, gain, w, head_gain, head_mask, tm, tn, name, interleave=None, n_extra=0,
                          norm_tiles=None):
    t, d = x.shape
    n_main = w.shape[1] - n_extra
    tm = min(tm, t)
    assert n_extra == 0 or tn == n_main
    shape, out_spec = _row_specs(t, tm, tn, interleave)
    out_shapes = [shape(n_main)]
    out_specs = [out_spec]
    if n_extra:
        out_shapes.append(jax.ShapeDtypeStruct((t, n_extra), F32))
        out_specs.append(pl.BlockSpec((tm, n_extra), lambda i, j: (i, 0)))
    outs = pl.pallas_call(
        functools.partial(_norm_matmul_headnorm_kernel, interleave=interleave, n_extra=n_extra,
                          norm_tiles=norm_tiles),
        grid=(t // tm, n_main // tn),
        in_specs=[
            pl.BlockSpec((tm, d), lambda i, j: (i, 0)),
            pl.BlockSpec((1, d), lambda i, j: (0, 0)),
            _col_tile_spec(d, tn + n_extra),
            pl.BlockSpec((1, tn), lambda i, j: (0, j)),
            pl.BlockSpec((1, tn), lambda i, j: (0, j)),
            pl.BlockSpec((MXU_WIDTH, MXU_WIDTH), lambda i, j: (0, 0)),
        ],
        out_specs=out_specs,
        out_shape=out_shapes,
        scratch_shapes=_proj_scratch(tm, d, interleave),
        compiler_params=_compiler_params(("parallel", "arbitrary")),
        name=name,
    )(x, gain.reshape(1, d).astype(F32), _col_tiles(w, tn + n_extra), head_gain.reshape(1, n_main).astype(F32),
      head_mask.reshape(1, n_main).astype(F32), _head_ones())
    if n_extra:
        return outs[0].reshape(t, n_main), outs[1]
    return outs[0].reshape(t, n_main)


def _lane_lo():
    return lax.broadcasted_iota(jnp.int32, (1, LANES), 1) < HEAD_DIM


def _attn_tiles(qs, ks, vs, masks, lane_lo, sinks=None, normalize=False):
    sel_rows = _iota((4 * ATTN_BLOCK, LANES), 0) < 2 * ATTN_BLOCK
    row_sum = (sel_rows == (_iota((4 * ATTN_BLOCK, LANES), 1) < HEAD_DIM)).astype(BF16)
    results = []
    for i, (q, k, v, mask) in enumerate(zip(qs, ks, vs, masks)):
        zq = jnp.zeros_like(q)
        q_heads = jnp.concatenate([jnp.where(lane_lo, q, zq), jnp.where(lane_lo, zq, q)], axis=0)
        s2 = lax.dot_general(q_heads, k, (((1,), (1,)), ((), ())), preferred_element_type=F32)
        ps, ms = [], []
        for idx in range(2):
            s = jnp.where(mask, s2[idx * ATTN_BLOCK:(idx + 1) * ATTN_BLOCK], -jnp.inf)
            m = jnp.max(s, axis=-1, keepdims=True)
            if sinks is not None:
                m = jnp.maximum(m, sinks[i][idx])
            ps.append(jnp.exp2(s - m).astype(BF16))
            ms.append(m)
        zv = jnp.zeros_like(v)
        v_heads = jnp.concatenate([jnp.where(lane_lo, v, zv), jnp.where(lane_lo, zv, v)], axis=0)
        res = jnp.dot(jnp.concatenate(ps, axis=1), jnp.concatenate([v_heads, row_sum], axis=1),
                      preferred_element_type=F32)
        o, l = res[:, :LANES], res[:, LANES:]
        if sinks is not None:
            l = l + jnp.where(lane_lo, jnp.exp2(sinks[i][0] - ms[0]), jnp.exp2(sinks[i][1] - ms[1]))
        if normalize:
            o = o * (1.0 / l)
        results.append((o, ms, l))
    return results


def _dist_mask(q_pos, k_pos, max_dist, key_exists=None):
    dist = q_pos - k_pos
    mask = (dist >= 0) & (dist <= max_dist)
    if key_exists is not None:
        mask = mask & key_exists
    return mask


def _iota(shape, dim):
    return lax.broadcasted_iota(jnp.int32, shape, dim)


def _swa_kernel(sinks_ref, q_ref, kvp_ref, kvc_ref, o_ref):
    first = pl.program_id(1) == 0
    lane_lo = _lane_lo()
    kj = _iota((1, 2 * ATTN_BLOCK), 1)
    mask = _dist_mask(_iota((ATTN_BLOCK, 1), 0) + ATTN_BLOCK, kj, SWA_WINDOW - 1,
                      (kj >= ATTN_BLOCK) | jnp.logical_not(first))
    kv2 = jnp.concatenate([kvp_ref[...], kvc_ref[...]], axis=0).astype(F32)

    def dup_heads(x):
        xr = pltpu.roll(x, HEAD_DIM, axis=1)
        return jnp.where(lane_lo, x, xr).astype(BF16), jnp.where(lane_lo, xr, x).astype(BF16)

    k_dup = dup_heads(kv2[:, :LANES])
    v_dup = dup_heads(kv2[:, LANES:])
    n_pairs = q_ref.shape[1] // LANES
    pairs_per_kv = n_pairs // SWA_KV_HEADS
    qs, ks, vs, masks, sinks = [], [], [], [], []
    for j in range(n_pairs):
        kh = j // pairs_per_kv
        qs.append(q_ref[:, j * LANES:(j + 1) * LANES])
        ks.append(k_dup[kh])
        vs.append(v_dup[kh])
        masks.append(mask)
        sinks.append((sinks_ref[2 * j], sinks_ref[2 * j + 1]))
    for j, (o, _, _) in enumerate(_attn_tiles(qs, ks, vs, masks, lane_lo, sinks, normalize=True)):
        o_ref[:, j * LANES:(j + 1) * LANES] = o.astype(o_ref.dtype)


def _swa_attention(qkv, sinks, batch, seq, q_width):
    t = qkv.shape[0]
    nb = seq // ATTN_BLOCK
    kv_block = q_width // (2 * LANES)
    return pl.pallas_call(
        _swa_kernel,
        grid=(batch, nb),
        in_specs=[
            pl.BlockSpec(memory_space=pltpu.SMEM),
            pl.BlockSpec((ATTN_BLOCK, q_width), lambda b, n: (b * nb + n, 0)),
            pl.BlockSpec((ATTN_BLOCK, 2 * LANES), lambda b, n: (b * nb + jnp.maximum(n - 1, 0), kv_block)),
            pl.BlockSpec((ATTN_BLOCK, 2 * LANES), lambda b, n: (b * nb + n, kv_block)),
        ],
        out_specs=pl.BlockSpec((ATTN_BLOCK, q_width), lambda b, n: (b * nb + n, 0)),
        out_shape=jax.ShapeDtypeStruct((t, q_width), BF16),
        compiler_params=_compiler_params(("parallel", "arbitrary")),
        name="swa_attention",
    )(sinks.astype(F32) * LOG2E, qkv, qkv, qkv)


STAT_GROUP = 4


def _write_stats(acc, lane, pair, ms, l):
    lane_lo = lane < HEAD_DIM
    stat = jnp.where(lane % STAT_GROUP < STAT_GROUP // 2, l, jnp.where(lane_lo, ms[0], ms[1]))
    return jnp.where((lane % HEAD_DIM) // STAT_GROUP == pair, stat, acc)


def _dil16_kernel(q_ref, kp_ref, kc_ref, vp_ref, vc_ref, o_ref, lse_ref):
    first = pl.program_id(2) == 0
    n_res = q_ref.shape[1]
    n_pairs = q_ref.shape[3] // LANES
    lane = _iota((1, LANES), 1)
    lane_lo = lane < HEAD_DIM
    kj = _iota((1, 2 * ATTN_BLOCK), 1)
    mask = _dist_mask(_iota((ATTN_BLOCK, 1), 0) + ATTN_BLOCK, kj, ATTN_BLOCK,
                      (kj >= ATTN_BLOCK) | jnp.logical_not(first))

    def body(j, lse_accs):
        cols = pl.ds(pl.multiple_of(j * LANES, LANES), LANES)
        qs = [q_ref[0, r, :, cols] for r in range(n_res)]
        ks = [jnp.concatenate([kp_ref[0, r, :, cols], kc_ref[0, r, :, cols]], axis=0) for r in range(n_res)]
        vs = [jnp.concatenate([vp_ref[0, r, :, cols], vc_ref[0, r, :, cols]], axis=0) for r in range(n_res)]
        out = []
        for r, (o, ms, ls) in enumerate(_attn_tiles(qs, ks, vs, [mask] * n_res, lane_lo)):
            o_ref[0, r, :, cols] = o.astype(o_ref.dtype)
            out.append(_write_stats(lse_accs[r], lane, j, ms, ls))
        return tuple(out)

    zeros = tuple(jnp.zeros((ATTN_BLOCK, LANES), F32) for _ in range(n_res))
    for r, acc in enumerate(lax.fori_loop(0, n_pairs, body, zeros, unroll=4)):
        lse_ref[0, r] = acc


def _dil4_kernel(q_ref, kp_ref, kc_ref, vp_ref, vc_ref, o_ref, lse_ref):
    first = pl.program_id(2) == 0
    n_sub = q_ref.shape[1]
    rows = ATTN_BLOCK // n_sub
    n_tiles = q_ref.shape[3] // rows
    n_pairs = q_ref.shape[4] // LANES
    lane = _iota((1, LANES), 1)
    lane_lo = lane < HEAD_DIM
    qi = _iota((ATTN_BLOCK, 1), 0)
    kj = _iota((1, 2 * ATTN_BLOCK), 1)
    q_pos = n_sub * (rows + qi % rows) + qi // rows
    k_pos = n_sub * (kj % (2 * rows)) + kj // (2 * rows)
    mask = _dist_mask(q_pos, k_pos, ATTN_BLOCK)
    mask_first = mask & ((kj % (2 * rows) >= rows) | jnp.logical_not(first))

    def gather(cur_ref, prev_ref, tile, cols):
        if tile == 0:
            parts = [jnp.concatenate([prev_ref[0, a, 0, :, cols], cur_ref[0, a, 0, 0:rows, cols]], axis=0)
                     for a in range(n_sub)]
        else:
            parts = [cur_ref[0, a, 0, (tile - 1) * rows:(tile + 1) * rows, cols] for a in range(n_sub)]
        return jnp.concatenate(parts, axis=0)

    def body(j, lse_accs):
        cols = pl.ds(pl.multiple_of(j * LANES, LANES), LANES)
        qs = [jnp.concatenate([q_ref[0, a, 0, t * rows:(t + 1) * rows, cols] for a in range(n_sub)], axis=0)
              for t in range(n_tiles)]
        ks = [gather(kc_ref, kp_ref, t, cols) for t in range(n_tiles)]
        vs = [gather(vc_ref, vp_ref, t, cols) for t in range(n_tiles)]
        masks = [mask_first] + [mask] * (n_tiles - 1)
        out = []
        for t, (o, ms, ls) in enumerate(_attn_tiles(qs, ks, vs, masks, lane_lo)):
            ob = o.astype(o_ref.dtype)
            for a in range(n_sub):
                o_ref[0, a, 0, t * rows:(t + 1) * rows, cols] = ob[a * rows:(a + 1) * rows, :]
            out.append(_write_stats(lse_accs[t], lane, j, ms, ls))
        return tuple(out)

    zeros = tuple(jnp.zeros((ATTN_BLOCK, LANES), F32) for _ in range(n_tiles))
    for t, acc in enumerate(lax.fori_loop(0, n_pairs, body, zeros, unroll=4)):
        for a in range(n_sub):
            lse_ref[0, a, 0, t * rows:(t + 1) * rows, :] = acc[a * rows:(a + 1) * rows, :]


def _dil1_kernel(q_ref, kp_ref, kc_ref, vp_ref, vc_ref, o_ref, lse_ref):
    first = (pl.program_id(1) == 0) & (pl.program_id(2) == 0)
    n_res = q_ref.shape[1]
    sub = ATTN_BLOCK // n_res
    n_tiles = q_ref.shape[2] // sub
    n_pairs = q_ref.shape[3] // LANES
    prev_rows = kp_ref.shape[2]
    lane = _iota((1, LANES), 1)
    lane_lo = lane < HEAD_DIM
    qi = _iota((ATTN_BLOCK, 1), 0)
    kj = _iota((1, 2 * ATTN_BLOCK), 1)
    q_pos = n_res * (sub + qi % sub) + qi // sub
    k_pos = n_res * (kj % (2 * sub)) + kj // (2 * sub)
    mask = _dist_mask(q_pos, k_pos, ATTN_BLOCK)
    mask_first = mask & ((kj % (2 * sub) >= sub) | jnp.logical_not(first))

    def tile_rows(x, lo, hi):
        return x[:, lo:hi, :].reshape(n_res * (hi - lo), LANES)

    def body(j, lse_accs):
        cols = pl.ds(pl.multiple_of(j * LANES, LANES), LANES)
        q = q_ref[0, :, :, cols].astype(F32)
        qs = [tile_rows(q, t * sub, (t + 1) * sub).astype(BF16) for t in range(n_tiles)]

        def gather(cur_ref, prev_ref):
            cur = cur_ref[0, :, :, cols].astype(F32)
            prev = prev_ref[0, :, :, cols].astype(F32)
            tiles = []
            for t in range(n_tiles):
                if t == 0:
                    x = jnp.concatenate([prev[:, prev_rows - sub:, :], cur[:, 0:sub, :]], axis=1)
                else:
                    x = cur[:, (t - 1) * sub:(t + 1) * sub, :]
                tiles.append(x.reshape(n_res * 2 * sub, LANES).astype(BF16))
            return tiles

        ks = gather(kc_ref, kp_ref)
        vs = gather(vc_ref, vp_ref)
        masks = [mask_first] + [mask] * (n_tiles - 1)
        res = _attn_tiles(qs, ks, vs, masks, lane_lo)
        o_full = jnp.concatenate([o.reshape(n_res, sub, LANES) for o, _, _ in res], axis=1)
        o_ref[0, :, :, cols] = o_full.astype(o_ref.dtype)
        return tuple(_write_stats(lse_accs[t], lane, j, ms, ls) for t, (_, ms, ls) in enumerate(res))

    zeros = tuple(jnp.zeros((ATTN_BLOCK, LANES), F32) for _ in range(n_tiles))
    accs = lax.fori_loop(0, n_pairs, body, zeros, unroll=4)
    lse_ref[0] = jnp.concatenate([acc.reshape(n_res, sub, LANES) for acc in accs], axis=1)


def _dilated_branches(qkv, batch, seq):
    t, w = qkv.shape[0], qkv.shape[1] // 3
    nsb = seq // SUPERBLOCK
    n_super = t // SUPERBLOCK
    per_res = SUPERBLOCK // RESIDUES
    outs, lses = [], []

    def in_specs(block, cur_map, prev_block, prev_map):
        cur = lambda col: pl.BlockSpec(block, lambda *g: cur_map(*g) + (col,))
        prev = lambda col: pl.BlockSpec(prev_block, lambda *g: prev_map(*g) + (col,))
        return [cur(0), prev(1), cur(1), prev(2), cur(2)]

    r16 = 4
    shp = (n_super, RESIDUES, per_res)
    cur_map = lambda b, r, s: (b * nsb + s, r, 0)
    prev_map = lambda b, r, s: (b * nsb + jnp.maximum(s - 1, 0), r, 0)
    qkv4 = qkv.reshape(shp + (3 * w,))
    o16, l16 = pl.pallas_call(
        _dil16_kernel, grid=(batch, RESIDUES // r16, nsb),
        in_specs=in_specs((1, r16, per_res, w), cur_map, (1, r16, per_res, w), prev_map),
        out_specs=[pl.BlockSpec((1, r16, per_res, w), lambda *g: cur_map(*g) + (0,)),
                   pl.BlockSpec((1, r16, per_res, LANES), lambda *g: cur_map(*g) + (0,))],
        out_shape=[jax.ShapeDtypeStruct(shp + (w,), BF16), jax.ShapeDtypeStruct(shp + (LANES,), F32)],
        compiler_params=_compiler_params(("parallel", "parallel", "arbitrary")),
        name="dilated_attention_d16",
    )(qkv4, qkv4, qkv4, qkv4, qkv4)

    shp5 = (n_super, 4, 4, per_res)
    tail = per_res // 4
    cur_map5 = lambda b, r, s: (b * nsb + s, 0, r, 0)
    prev_map5 = lambda b, r, s: (b * nsb + jnp.maximum(s - 1, 0), 0, r, per_res // tail - 1)
    qkv5 = qkv.reshape(shp5 + (3 * w,))
    o4, l4 = pl.pallas_call(
        _dil4_kernel, grid=(batch, 4, nsb),
        in_specs=in_specs((1, 4, 1, per_res, w), cur_map5, (1, 4, 1, tail, w), prev_map5),
        out_specs=[pl.BlockSpec((1, 4, 1, per_res, w), lambda *g: cur_map5(*g) + (0,)),
                   pl.BlockSpec((1, 4, 1, per_res, LANES), lambda *g: cur_map5(*g) + (0,))],
        out_shape=[jax.ShapeDtypeStruct(shp5 + (w,), BF16), jax.ShapeDtypeStruct(shp5 + (LANES,), F32)],
        compiler_params=_compiler_params(("parallel", "parallel", "arbitrary")),
        name="dilated_attention_d4",
    )(qkv5, qkv5, qkv5, qkv5, qkv5)

    run, hist = 32, 16
    runs = per_res // run
    cur_map1 = lambda b, s, c: (b * nsb + s, 0, c)

    def prev_map1(b, s, c):
        g = (b * nsb + s) * (per_res // hist) + c * (run // hist) - 1
        g = jnp.maximum(g, 0)
        return (g // (per_res // hist), 0, g % (per_res // hist))

    o1, l1 = pl.pallas_call(
        _dil1_kernel, grid=(batch, nsb, runs),
        in_specs=in_specs((1, RESIDUES, run, w), cur_map1, (1, RESIDUES, hist, w), prev_map1),
        out_specs=[pl.BlockSpec((1, RESIDUES, run, w), lambda *g: cur_map1(*g) + (0,)),
                   pl.BlockSpec((1, RESIDUES, run, LANES), lambda *g: cur_map1(*g) + (0,))],
        out_shape=[jax.ShapeDtypeStruct(shp + (w,), BF16), jax.ShapeDtypeStruct(shp + (LANES,), F32)],
        compiler_params=_compiler_params(("parallel", "arbitrary", "arbitrary")),
        name="dilated_attention_d1",
    )(qkv4, qkv4, qkv4, qkv4, qkv4)

    for o, l in ((o1, l1), (o4, l4), (o16, l16)):
        outs.append(o.reshape(t, w))
        lses.append(l.reshape(t, LANES))
    return outs, lses


def _softplus(x):
    return jnp.maximum(x, 0.0) + jnp.log1p(jnp.exp(-jnp.abs(x)))


def _silu(x):
    h = 0.5 * x
    return h * jnp.tanh(h) + h


def _expand_heads(v, head_mask, e3):
    n_heads = e3.shape[1] // HEAD_DIM
    v = jnp.where(head_mask, v, 0.0)
    hi = v.astype(BF16).astype(F32)
    r = v - hi
    mid = r.astype(BF16).astype(F32)
    lo = (r - mid).astype(BF16).astype(F32)
    cat = hi + pltpu.roll(mid, n_heads, axis=1) + pltpu.roll(lo, 2 * n_heads, axis=1)
    return jnp.dot(cat.astype(BF16), e3, preferred_element_type=F32)


def _ssd_kernel(xs_ref, bc_ref, z_ref, dt_ref, cw_ref, cb_ref, dtb_ref, a_ref, dskip_ref, gn_ref, tril_ref, e3_ref,
                shift_ref, y_ref, hist_ref, state_ref):
    chunk = SSM_CHUNK
    d_inner = z_ref.shape[1]
    n_heads = d_inner // HEAD_DIM
    group_w = d_inner // SSM_GROUPS
    heads_per_group = n_heads // SSM_GROUPS

    @pl.when(pl.program_id(1) == 0)
    def _():
        hist_ref[...] = jnp.zeros(hist_ref.shape, hist_ref.dtype)
        state_ref[...] = jnp.zeros(state_ref.shape, F32)

    parts = []
    for ref, cols in ((xs_ref, slice(0, d_inner)), (bc_ref, slice(d_inner, hist_ref.shape[1]))):
        cur = ref[...]
        ext = jnp.concatenate([hist_ref[:, cols], cur], axis=0)
        shifted = jnp.dot(shift_ref[...], ext, preferred_element_type=F32)
        conv = cb_ref[:, cols] + cw_ref[SSM_CONV - 1:SSM_CONV, cols] * cur.astype(F32)
        for j in range(SSM_CONV - 1):
            conv = conv + cw_ref[j:j + 1, cols] * shifted[j * chunk:(j + 1) * chunk]
        hist_ref[:, cols] = cur[chunk - CONV_HIST:, :]
        parts.append(_silu(conv))
    xs = parts[0]
    xbc_bc = parts[1]

    lane = _iota((1, LANES), 1)
    head_mask = lane < n_heads
    lane_lo = lane < HEAD_DIM
    e3 = e3_ref[...]

    dt = jnp.where(head_mask, _softplus(dt_ref[...] + dtb_ref[...]), 0.0)
    cum = jnp.dot(tril_ref[...], dt * a_ref[...], precision=lax.Precision.HIGHEST,
                  preferred_element_type=F32)
    cum_t = cum.T
    ecum_e = _expand_heads(jnp.exp(cum), head_mask, e3)
    dte_e = _expand_heads(jnp.exp(cum[chunk - 1:chunk, :] - cum), head_mask, e3)
    xdt = xs * _expand_heads(dt, head_mask, e3)
    xend = xdt * dte_e

    causal = _iota((chunk, chunk), 0) >= _iota((chunk, chunk), 1)

    for g in range(SSM_GROUPS):
        gcols = slice(g * group_w, (g + 1) * group_w)
        b_f32 = xbc_bc[:, g * SSM_STATE: (g + 1) * SSM_STATE]
        c_off = SSM_GROUPS * SSM_STATE
        c_bf = xbc_bc[:, c_off + g * SSM_STATE: c_off + (g + 1) * SSM_STATE].astype(BF16)
        cb = lax.dot_general(c_bf, b_f32.astype(BF16), (((1,), (1,)), ((), ())), preferred_element_type=F32)
        st = state_ref[g]
        y_off = jnp.dot(c_bf, st.astype(BF16), preferred_element_type=F32) * ecum_e[:, gcols]
        state_ref[g] = st * ecum_e[chunk - 1:chunk, gcols] + jnp.dot(
            b_f32.T.astype(BF16), xend[:, gcols].astype(BF16), preferred_element_type=F32)

        y_pairs = []
        for pair in range(heads_per_group // 2):
            h0 = g * heads_per_group + 2 * pair
            ms = []
            for h in (h0, h0 + 1):
                diff = cum[:, h:h + 1] - cum_t[h:h + 1, :]
                seg = jnp.exp(jnp.where(causal, diff, -jnp.inf))
                ms.append((cb * seg).astype(BF16))
            xp = xdt[:, h0 * HEAD_DIM: h0 * HEAD_DIM + LANES]
            x_heads = jnp.concatenate([jnp.where(lane_lo, xp, 0.0), jnp.where(lane_lo, 0.0, xp)], axis=0)
            y_pairs.append(jnp.dot(jnp.concatenate(ms, axis=1), x_heads.astype(BF16),
                                   preferred_element_type=F32))
        y = jnp.concatenate(y_pairs, axis=1) + y_off + dskip_ref[:, gcols] * xs[:, gcols]
        gy = y * _silu(z_ref[:, gcols].astype(F32))
        ms_g = jnp.mean(gy * gy, axis=-1, keepdims=True)
        y_ref[:, gcols] = (gy * lax.rsqrt(ms_g + NORM_EPS) * gn_ref[:, gcols]).astype(y_ref.dtype)


def _ssd_mixer(proj, dt_raw, conv_w, conv_b, dt_bias, a_log, d_skip, gate_norm, batch, seq):
    t = proj.shape[0]
    d_inner = gate_norm.shape[0]
    n_heads = d_inner // HEAD_DIM
    conv_dim = conv_w.shape[1]
    bc_w = conv_dim - d_inner
    nc = seq // SSM_CHUNK
    assert d_inner % bc_w == 0

    def pad_heads(v):
        return jnp.zeros((1, LANES), F32).at[0, :n_heads].set(v.astype(F32))

    tril = (_iota((SSM_CHUNK, SSM_CHUNK), 0) >= _iota((SSM_CHUNK, SSM_CHUNK), 1)).astype(F32)
    row = _iota((LANES, d_inner), 0)
    col = _iota((LANES, d_inner), 1)
    e3 = ((row < 3 * n_heads) & ((row % n_heads) == (col // HEAD_DIM))).astype(BF16)
    srow = _iota(((SSM_CONV - 1) * SSM_CHUNK, CONV_HIST + SSM_CHUNK), 0)
    scol = _iota(((SSM_CONV - 1) * SSM_CHUNK, CONV_HIST + SSM_CHUNK), 1)
    shift = (scol == CONV_HIST + srow % SSM_CHUNK - (SSM_CONV - 1 - srow // SSM_CHUNK)).astype(BF16)
    const = lambda shape: pl.BlockSpec(shape, lambda b, c: (0, 0))
    return pl.pallas_call(
        _ssd_kernel,
        grid=(batch, nc),
        in_specs=[
            pl.BlockSpec((SSM_CHUNK, d_inner), lambda b, c: (b * nc + c, 1)),
            pl.BlockSpec((SSM_CHUNK, bc_w), lambda b, c: (b * nc + c, 2 * d_inner // bc_w)),
            pl.BlockSpec((SSM_CHUNK, d_inner), lambda b, c: (b * nc + c, 0)),
            pl.BlockSpec((SSM_CHUNK, LANES), lambda b, c: (b * nc + c, 0)),
            const((SSM_CONV, conv_dim)), const((1, conv_dim)), const((1, LANES)), const((1, LANES)),
            const((1, d_inner)), const((1, d_inner)), const((SSM_CHUNK, SSM_CHUNK)), const((LANES, d_inner)),
            const(shift.shape),
        ],
        out_specs=pl.BlockSpec((SSM_CHUNK, d_inner), lambda b, c: (b * nc + c, 0)),
        out_shape=jax.ShapeDtypeStruct((t, d_inner), BF16),
        scratch_shapes=[pltpu.VMEM((CONV_HIST, conv_dim), BF16),
                        pltpu.VMEM((SSM_GROUPS, SSM_STATE, d_inner // SSM_GROUPS), F32)],
        compiler_params=_compiler_params(("parallel", "arbitrary")),
        name="ssd_mixer",
    )(proj, proj, proj, dt_raw, conv_w.astype(F32), conv_b.reshape(1, conv_dim).astype(F32),
      pad_heads(dt_bias), pad_heads(-jnp.exp(a_log.astype(F32))),
      jnp.repeat(d_skip.astype(F32), HEAD_DIM).reshape(1, d_inner), gate_norm.reshape(1, d_inner).astype(F32),
      tril, e3, shift)


def _outproj_swa_ssd_kernel(x_ref, a_ref, y_ref, wa_ref, wy_ref, o_ref):
    o_ref[...] = (x_ref[...]
                  + jnp.dot(a_ref[...], wa_ref[...], preferred_element_type=F32)
                  + jnp.dot(y_ref[...], wy_ref[...], preferred_element_type=F32))


def _outproj_swa_ssd(x, attn, y, w_attn, w_y, tm, tn):
    t, d = x.shape
    tm = min(tm, t)
    return pl.pallas_call(
        _outproj_swa_ssd_kernel,
        grid=(t // tm, d // tn),
        in_specs=[
            pl.BlockSpec((tm, tn), lambda i, j: (i, j)),
            pl.BlockSpec((tm, attn.shape[1]), lambda i, j: (i, 0)),
            pl.BlockSpec((tm, y.shape[1]), lambda i, j: (i, 0)),
            _col_tile_spec(attn.shape[1], tn),
            _col_tile_spec(y.shape[1], tn),
        ],
        out_specs=pl.BlockSpec((tm, tn), lambda i, j: (i, j)),
        out_shape=jax.ShapeDtypeStruct((t, d), F32),
        compiler_params=_compiler_params(("parallel", "arbitrary")),
        name="outproj_swa_ssd",
    )(x, attn, y, _col_tiles(w_attn, tn), _col_tiles(w_y, tn))


def _outproj_dilated_kernel(x_ref, o1_ref, o2_ref, o3_ref, l1_ref, l2_ref, l3_ref, e2_ref, w_ref, out_ref,
                            om_ref, scr_ref):
    @pl.when(pl.program_id(1) == 0)
    def _():
        tm, width = om_ref.shape
        n_res = o1_ref.shape[1]
        lane = _iota((1, LANES), 1)
        head_mask = lane % STAT_GROUP == 0
        stats = [l_ref[...].reshape(tm, LANES) for l_ref in (l1_ref, l2_ref, l3_ref)]
        maxes = [pltpu.roll(st, LANES - STAT_GROUP // 2, axis=1) for st in stats]
        m = jnp.maximum(jnp.maximum(maxes[0], maxes[1]), maxes[2])
        es = [jnp.exp2(mx - m) for mx in maxes]
        den = sum(e * st for e, st in zip(es, stats))
        inv = 1.0 / jnp.where(head_mask, den, 1.0)
        acc = None
        for e, o_ref in zip(es, (o1_ref, o2_ref, o3_ref)):
            w = jnp.where(head_mask, e * inv, 0.0)
            hi = w.astype(BF16).astype(F32)
            lo = (w - hi).astype(BF16).astype(F32)
            cat = (hi + pltpu.roll(lo, 1, axis=1)).astype(BF16)
            term = jnp.dot(cat, e2_ref[...], preferred_element_type=F32) * o_ref[...].reshape(tm, width).astype(F32)
            acc = term if acc is None else acc + term
        rows = tm // n_res
        pitch = scr_ref.shape[1] // n_res
        for c in range(width // LANES):
            for r in range(n_res):
                scr_ref[c, r * pitch:r * pitch + rows, :] = acc[r * rows:(r + 1) * rows, c * LANES:(c + 1) * LANES]
        for n in range(rows):
            om_ref[n * n_res:(n + 1) * n_res, :] = jnp.concatenate(
                [scr_ref[c, pl.ds(n, n_res, stride=pitch), :] for c in range(width // LANES)], axis=1).astype(BF16)

    out_ref[...] = x_ref[...] + jnp.dot(om_ref[...], w_ref[...], preferred_element_type=F32)


def _outproj_dilated(x, outs, lses, w_o, tm, tn):
    t, d = x.shape
    width = w_o.shape[0]
    n_heads = width // HEAD_DIM
    tm = min(tm, t)
    rows = tm // RESIDUES
    per_super = SUPERBLOCK // tm
    row = _iota((LANES, width), 0)
    col = _iota((LANES, width), 1)
    stat_head = 2 * ((row % HEAD_DIM) // STAT_GROUP) + row // HEAD_DIM
    e2 = ((row % STAT_GROUP < 2) & (stat_head == col // HEAD_DIM)).astype(BF16)
    assert n_heads * STAT_GROUP // 2 <= HEAD_DIM
    shp = (t // SUPERBLOCK, RESIDUES, SUPERBLOCK // RESIDUES)
    o_spec = pl.BlockSpec((1, RESIDUES, rows, width), lambda i, j: (i // per_super, 0, i % per_super, 0))
    l_spec = pl.BlockSpec((1, RESIDUES, rows, LANES), lambda i, j: (i // per_super, 0, i % per_super, 0))
    return pl.pallas_call(
        _outproj_dilated_kernel,
        grid=(t // tm, d // tn),
        in_specs=[pl.BlockSpec((tm, tn), lambda i, j: (i, j)), o_spec, o_spec, o_spec, l_spec, l_spec, l_spec,
                  pl.BlockSpec((LANES, width), lambda i, j: (0, 0)),
                  pl.BlockSpec((width, tn), lambda i, j: (0, j),
                               pipeline_mode=pl.Buffered(1) if tn == d else None)],
        out_specs=pl.BlockSpec((tm, tn), lambda i, j: (i, j)),
        out_shape=jax.ShapeDtypeStruct((t, d), F32),
        scratch_shapes=[pltpu.VMEM((tm, width), BF16),
                        pltpu.VMEM((width // LANES, RESIDUES * (rows + STAGE_PITCH - RESIDUES), LANES), F32)],
        compiler_params=_compiler_params(("parallel", "arbitrary")),
        name="outproj_dilated",
    )(x, *[o.reshape(shp + (width,)) for o in outs], *[l.reshape(shp + (LANES,)) for l in lses], e2, w_o)


def _mlp_kernel(x_ref, g_ref, wu_ref, wd_ref, o_ref, h_ref):
    @pl.when(pl.program_id(1) == 0)
    def _():
        h_ref[...] = _rms_norm_rows(x_ref[...], g_ref[...]).astype(BF16)
        o_ref[...] = x_ref[...]

    u = jnp.dot(h_ref[...], wu_ref[...], preferred_element_type=F32)
    a = jnp.square(jnp.maximum(u, 0.0)).astype(BF16)
    o_ref[...] += jnp.dot(a, wd_ref[...], preferred_element_type=F32)


def _mlp(x, gain, w_up, w_down, tm, tf):
    t, d = x.shape
    ff = w_up.shape[1]
    tm = min(tm, t)
    return pl.pallas_call(
        _mlp_kernel,
        grid=(t // tm, ff // tf),
        in_specs=[
            pl.BlockSpec((tm, d), lambda i, j: (i, 0)),
            pl.BlockSpec((1, d), lambda i, j: (0, 0)),
            _col_tile_spec(d, tf),
            pl.BlockSpec((tf, d), lambda i, j: (j, 0)),
        ],
        out_specs=pl.BlockSpec((tm, d), lambda i, j: (i, 0)),
        out_shape=jax.ShapeDtypeStruct((t, d), F32),
        scratch_shapes=[pltpu.VMEM((tm, d), BF16)],
        compiler_params=_compiler_params(("parallel", "arbitrary")),
        name="sqrelu_mlp",
    )(x, gain.reshape(1, d).astype(F32), _col_tiles(w_up, tf), w_down)


def _swa_ssd_layer(x, gain, w_in, q_norm, k_norm, sinks, conv_w, conv_b, dt_bias, a_log, d_skip, gate_norm,
                   w_out, batch, seq):
    d_inner = gate_norm.shape[0]
    n_ssm_heads = dt_bias.shape[0]
    n_q_heads = sinks.shape[0]
    q_w = n_q_heads * HEAD_DIM
    kv_w = SWA_KV_HEADS * HEAD_DIM
    conv_dim = conv_w.shape[1]
    splits = (q_w, q_w + kv_w, q_w + 2 * kv_w, q_w + 2 * kv_w + d_inner, q_w + 2 * kv_w + d_inner + conv_dim)
    wq, wk, wv, wz, wxbc, wdt = jnp.split(w_in, splits, axis=1)
    w_qkv_dt = jnp.concatenate([wq, wk, wv, jnp.pad(wdt, ((0, 0), (0, LANES - n_ssm_heads)))], axis=1).astype(BF16)
    w_ssm = jnp.concatenate([wz, wxbc], axis=1).astype(BF16)
    qkv_w = q_w + 2 * kv_w
    head_gain = jnp.concatenate([jnp.tile(q_norm.astype(F32) * Q_SCALE, n_q_heads),
                                 jnp.tile(k_norm.astype(F32), SWA_KV_HEADS), jnp.ones((kv_w,), F32)])
    head_mask = jnp.concatenate([jnp.ones((q_w + kv_w,), F32), jnp.zeros((kv_w,), F32)])

    qkv, dt_raw = _norm_matmul_headnorm(x, gain, w_qkv_dt, head_gain, head_mask, tm=1024, tn=qkv_w,
                                        name="norm_qkv_dt_proj", n_extra=LANES)
    proj = _norm_matmul(x, gain, w_ssm, tm=1024, tn=w_ssm.shape[1] // 4, name="norm_ssm_proj")
    attn = _swa_attention(qkv, sinks, batch, seq, q_w)
    y = _ssd_mixer(proj, dt_raw, conv_w, conv_b, dt_bias, a_log, d_skip, gate_norm, batch, seq)
    w_out = w_out.astype(BF16)
    return _outproj_swa_ssd(x, attn, y, w_out[:q_w], w_out[q_w:], tm=1024, tn=1024)


def _dilated_layer(x, gain, w_qkv, q_norm, k_norm, w_o, batch, seq):
    width = w_o.shape[0]
    n_heads = width // HEAD_DIM
    w_qkv = w_qkv.astype(BF16)
    head_gain = jnp.concatenate([jnp.tile(q_norm.astype(F32) * Q_SCALE, n_heads),
                                 jnp.tile(k_norm.astype(F32), n_heads), jnp.ones((width,), F32)])
    head_mask = jnp.concatenate([jnp.ones((2 * width,), F32), jnp.zeros((width,), F32)])
    tn = 1024
    qkv = _norm_matmul_headnorm(x, gain, w_qkv, head_gain, head_mask, tm=1024, tn=tn, name="norm_qkv_proj",
                                interleave=RESIDUES, norm_tiles=2 * width // tn)
    outs, lses = _dilated_branches(qkv, batch, seq)
    return _outproj_dilated(x, outs, lses, w_o.astype(BF16), tm=512, tn=w_o.shape[1])


def kernel(x, norm_mix, norm_ffn, w_up, w_down, ab_w_in, ab_q_norm, ab_k_norm, ab_sinks, ab_conv_w, ab_conv_b,
           ab_dt_bias, ab_a_log, ab_d_skip, ab_gate_norm, ab_w_out, c_w_qkv, c_q_norm, c_k_norm, c_w_o):
    batch, seq, d_model = x.shape
    depth = norm_mix.shape[0]
    assert seq % SUPERBLOCK == 0 and seq % SSM_CHUNK == 0
    h = x.reshape(batch * seq, d_model).astype(F32)
    for layer in range(depth):
        i = layer // 2
        if layer % 2 == 0:
            h = _swa_ssd_layer(h, norm_mix[layer], ab_w_in[i], ab_q_norm[i], ab_k_norm[i], ab_sinks[i],
                               ab_conv_w[i], ab_conv_b[i], ab_dt_bias[i], ab_a_log[i], ab_d_skip[i],
                               ab_gate_norm[i], ab_w_out[i], batch, seq)
        else:
            h = _dilated_layer(h, norm_mix[layer], c_w_qkv[i], c_q_norm[i], c_k_norm[i], c_w_o[i], batch, seq)
        h = _mlp(h, norm_ffn[layer], w_up[layer].astype(BF16), w_down[layer].astype(BF16), tm=512, tf=1024)
    return h.reshape(batch, seq, d_model).astype(x.dtype)
```

```python
import functools
import math

import jax
import jax.numpy as jnp
from jax import lax
from jax.experimental import pallas as pl
from jax.experimental.pallas import tpu as pltpu

F32 = jnp.float32
BF16 = jnp.bfloat16

NORM_EPS = 1e-5
HEAD_DIM = 64
ATTN_BLOCK = 128
SWA_KV_HEADS = 2
SWA_WINDOW = 128
SSM_GROUPS = 4
SSM_STATE = 128
SSM_CONV = 4
SSM_CHUNK = 128
DIL_BRANCHES = ((128, 1), (512, 4), (2048, 16))
RESIDUES = 16
SUPERBLOCK = RESIDUES * ATTN_BLOCK
STAGE_PITCH = 20

LOG2E = math.log2(math.e)
Q_SCALE = HEAD_DIM ** -0.5 * LOG2E

LANES = 128
MXU_WIDTH = 256
CONV_HIST = 16
VMEM_LIMIT_BYTES = 56 * 1024 * 1024


def _compiler_params(semantics):
    return pltpu.CompilerParams(dimension_semantics=semantics, vmem_limit_bytes=VMEM_LIMIT_BYTES)


def _rms_norm_rows(x, gain):
    ms = jnp.mean(x * x, axis=-1, keepdims=True)
    return x * lax.rsqrt(ms + NORM_EPS) * gain


def _norm_rows_into(h_ref, x_ref, g_ref, interleave, xs_ref):
    if interleave is None:
        h_ref[...] = _rms_norm_rows(x_ref[...], g_ref[...]).astype(BF16)
        return
    n_chunks = xs_ref.shape[0]
    rows = x_ref.shape[0] // interleave
    pitch = xs_ref.shape[1] // rows
    for c in range(n_chunks):
        for k in range(rows):
            xs_ref[c, k * pitch:k * pitch + interleave, :] = x_ref[k * interleave:(k + 1) * interleave,
                                                                   c * LANES:(c + 1) * LANES]
    for r in range(interleave):
        xr = jnp.concatenate([xs_ref[c, pl.ds(r, rows, stride=pitch), :] for c in range(n_chunks)], axis=1)
        h_ref[r * rows:(r + 1) * rows, :] = _rms_norm_rows(xr, g_ref[...]).astype(BF16)


def _norm_matmul_kernel(x_ref, g_ref, w_ref, o_ref, h_ref, xs_ref=None, *, interleave):
    @pl.when(pl.program_id(1) == 0)
    def _():
        _norm_rows_into(h_ref, x_ref, g_ref, interleave, xs_ref)

    acc = jnp.dot(h_ref[...], w_ref[...], preferred_element_type=F32)
    o_ref[...] = acc.astype(o_ref.dtype).reshape(o_ref.shape)


def _norm_matmul_headnorm_kernel(x_ref, g_ref, w_ref, hg_ref, hm_ref, ones_ref, *rest, interleave, n_extra,
                                 norm_tiles):
    rest = list(rest)
    o_ref = rest.pop(0)
    extra_ref = rest.pop(0) if n_extra else None
    h_ref = rest.pop(0)
    xs_ref = rest.pop(0) if interleave is not None else None
    j = pl.program_id(1)

    @pl.when(j == 0)
    def _():
        _norm_rows_into(h_ref, x_ref, g_ref, interleave, xs_ref)

    acc = jnp.dot(h_ref[...], w_ref[...], preferred_element_type=F32)
    n_main = acc.shape[1] - n_extra

    def head_norm_store():
        chunks = []
        for c in range(n_main // MXU_WIDTH):
            cols = slice(c * MXU_WIDTH, (c + 1) * MXU_WIDTH)
            a = acc[:, cols]
            ss = jnp.dot((a * a).astype(BF16), ones_ref[...], preferred_element_type=F32)
            r = lax.rsqrt(ss * (1.0 / HEAD_DIM) + NORM_EPS)
            chunks.append((a * jnp.where(hm_ref[:, cols] > 0.0, r, 1.0) * hg_ref[:, cols]).astype(o_ref.dtype))
        o_ref[...] = jnp.concatenate(chunks, axis=1).reshape(o_ref.shape)

    if norm_tiles is None:
        head_norm_store()
    else:
        pl.when(j < norm_tiles)(head_norm_store)

        @pl.when(j >= norm_tiles)
        def _():
            o_ref[...] = acc.astype(o_ref.dtype).reshape(o_ref.shape)
    if n_extra:
        extra_ref[...] = acc[:, n_main:]


def _proj_scratch(tm, d, interleave):
    scratch = [pltpu.VMEM((tm, d), BF16)]
    if interleave is not None:
        scratch.append(pltpu.VMEM((d // LANES, tm // interleave * STAGE_PITCH, LANES), F32))
    return scratch


def _row_specs(t, tm, tn, interleave):
    if interleave is None:
        return (lambda n: jax.ShapeDtypeStruct((t, n), BF16)), pl.BlockSpec((tm, tn), lambda i, j: (i, j))
    per_super = SUPERBLOCK // tm
    rows = tm // interleave
    shape = lambda n: jax.ShapeDtypeStruct((t // SUPERBLOCK, interleave, SUPERBLOCK // interleave, n), BF16)
    spec = pl.BlockSpec((1, interleave, rows, tn), lambda i, j: (i // per_super, 0, i % per_super, j))
    return shape, spec


def _norm_matmul(x, gain, w, tm, tn, name, interleave=None):
    t, d = x.shape
    n = w.shape[1]
    tm = min(tm, t)
    shape, out_spec = _row_specs(t, tm, tn, interleave)
    out = pl.pallas_call(
        functools.partial(_norm_matmul_kernel, interleave=interleave),
        grid=(t // tm, n // tn),
        in_specs=[
            pl.BlockSpec((tm, d), lambda i, j: (i, 0)),
            pl.BlockSpec((1, d), lambda i, j: (0, 0)),
            pl.BlockSpec((d, tn), lambda i, j: (0, j)),
        ],
        out_specs=out_spec,
        out_shape=shape(n),
        scratch_shapes=_proj_scratch(tm, d, interleave),
        compiler_params=_compiler_params(("parallel", "arbitrary")),
        name=name,
    )(x, gain.reshape(1, d).astype(F32), w)
    return out.reshape(t, n)


def _head_ones():
    r = lax.broadcasted_iota(jnp.int32, (MXU_WIDTH, MXU_WIDTH), 0)
    c = lax.broadcasted_iota(jnp.int32, (MXU_WIDTH, MXU_WIDTH), 1)
    return ((r // HEAD_DIM) == (c // HEAD_DIM)).astype(BF16)


def _norm_matmul_headnorm(x, gain, w, head_gain, head_mask, tm, tn, name, interleave=None, n_extra=0,
                          norm_tiles=None):
    t, d = x.shape
    n_main = w.shape[1] - n_extra
    tm = min(tm, t)
    assert n_extra == 0 or tn == n_main
    shape, out_spec = _row_specs(t, tm, tn, interleave)
    out_shapes = [shape(n_main)]
    out_specs = [out_spec]
    if n_extra:
        out_shapes.append(jax.ShapeDtypeStruct((t, n_extra), F32))
        out_specs.append(pl.BlockSpec((tm, n_extra), lambda i, j: (i, 0)))
    outs = pl.pallas_call(
        functools.partial(_norm_matmul_headnorm_kernel, interleave=interleave, n_extra=n_extra,
                          norm_tiles=norm_tiles),
        grid=(t // tm, n_main // tn),
        in_specs=[
            pl.BlockSpec((tm, d), lambda i, j: (i, 0)),
            pl.BlockSpec((1, d), lambda i, j: (0, 0)),
            pl.BlockSpec((d, tn + n_extra), lambda i, j: (0, j)),
            pl.BlockSpec((1, tn), lambda i, j: (0, j)),
            pl.BlockSpec((1, tn), lambda i, j: (0, j)),
            pl.BlockSpec((MXU_WIDTH, MXU_WIDTH), lambda i, j: (0, 0)),
        ],
        out_specs=out_specs,
        out_shape=out_shapes,
        scratch_shapes=_proj_scratch(tm, d, interleave),
        compiler_params=_compiler_params(("parallel", "arbitrary")),
        name=name,
    )(x, gain.reshape(1, d).astype(F32), w, head_gain.reshape(1, n_main).astype(F32),
      head_mask.reshape(1, n_main).astype(F32), _head_ones())
    if n_extra:
        return outs[0].reshape(t, n_main), outs[1]
    return outs[0].reshape(t, n_main)


def _lane_lo():
    return lax.broadcasted_iota(jnp.int32, (1, LANES), 1) < HEAD_DIM


def _attn_tiles(qs, ks, vs, masks, lane_lo, sinks=None, normalize=False):
    sel_rows = _iota((4 * ATTN_BLOCK, LANES), 0) < 2 * ATTN_BLOCK
    row_sum = (sel_rows == (_iota((4 * ATTN_BLOCK, LANES), 1) < HEAD_DIM)).astype(BF16)
    results = []
    for i, (q, k, v, mask) in enumerate(zip(qs, ks, vs, masks)):
        zq = jnp.zeros_like(q)
        q_heads = jnp.concatenate([jnp.where(lane_lo, q, zq), jnp.where(lane_lo, zq, q)], axis=0)
        s2 = lax.dot_general(q_heads, k, (((1,), (1,)), ((), ())), preferred_element_type=F32)
        ps, ms = [], []
        for idx in range(2):
            s = jnp.where(mask, s2[idx * ATTN_BLOCK:(idx + 1) * ATTN_BLOCK], -jnp.inf)
            m = jnp.max(s, axis=-1, keepdims=True)
            if sinks is not None:
                m = jnp.maximum(m, sinks[i][idx])
            ps.append(jnp.exp2(s - m).astype(BF16))
            ms.append(m)
        zv = jnp.zeros_like(v)
        v_heads = jnp.concatenate([jnp.where(lane_lo, v, zv), jnp.where(lane_lo, zv, v)], axis=0)
        res = jnp.dot(jnp.concatenate(ps, axis=1), jnp.concatenate([v_heads, row_sum], axis=1),
                      preferred_element_type=F32)
        o, l = res[:, :LANES], res[:, LANES:]
        if sinks is not None:
            l = l + jnp.where(lane_lo, jnp.exp2(sinks[i][0] - ms[0]), jnp.exp2(sinks[i][1] - ms[1]))
        if normalize:
            o = o * (1.0 / l)
        results.append((o, ms, l))
    return results


def _dist_mask(q_pos, k_pos, max_dist, key_exists=None):
    dist = q_pos - k_pos
    mask = (dist >= 0) & (dist <= max_dist)
    if key_exists is not None:
        mask = mask & key_exists
    return mask


def _iota(shape, dim):
    return lax.broadcasted_iota(jnp.int32, shape, dim)


SWA_BLOCKS_PER_STEP = 4


def _swa_kernel(sinks_ref, q_ref, kvp_ref, kvc_ref, o_ref):
    first = pl.program_id(1) == 0
    lane_lo = _lane_lo()
    kj = _iota((1, 2 * ATTN_BLOCK), 1)
    mask = _dist_mask(_iota((ATTN_BLOCK, 1), 0) + ATTN_BLOCK, kj, SWA_WINDOW - 1)
    mask_first = mask & ((kj >= ATTN_BLOCK) | jnp.logical_not(first))
    kv = jnp.concatenate([kvp_ref[...], kvc_ref[...]], axis=0).astype(F32)

    def dup_heads(x):
        xr = pltpu.roll(x, HEAD_DIM, axis=1)
        return jnp.where(lane_lo, x, xr).astype(BF16), jnp.where(lane_lo, xr, x).astype(BF16)

    k_dup = dup_heads(kv[:, :LANES])
    v_dup = dup_heads(kv[:, LANES:])
    n_blocks = q_ref.shape[0] // ATTN_BLOCK
    n_pairs = q_ref.shape[1] // LANES
    pairs_per_kv = n_pairs // SWA_KV_HEADS
    qs, ks, vs, masks, sinks = [], [], [], [], []
    for blk in range(n_blocks):
        rows = slice(blk * ATTN_BLOCK, (blk + 1) * ATTN_BLOCK)
        keys = slice(blk * ATTN_BLOCK, (blk + 2) * ATTN_BLOCK)
        for j in range(n_pairs):
            kh = j // pairs_per_kv
            qs.append(q_ref[rows, j * LANES:(j + 1) * LANES])
            ks.append(k_dup[kh][keys])
            vs.append(v_dup[kh][keys])
            masks.append(mask_first if blk == 0 else mask)
            sinks.append((sinks_ref[2 * j], sinks_ref[2 * j + 1]))
    for idx, (o, _, _) in enumerate(_attn_tiles(qs, ks, vs, masks, lane_lo, sinks, normalize=True)):
        blk, j = divmod(idx, n_pairs)
        o_ref[blk * ATTN_BLOCK:(blk + 1) * ATTN_BLOCK, j * LANES:(j + 1) * LANES] = o.astype(o_ref.dtype)


def _swa_attention(qkv, sinks, batch, seq, q_width):
    t = qkv.shape[0]
    rows = SWA_BLOCKS_PER_STEP * ATTN_BLOCK
    nb = seq // rows
    kv_block = q_width // (2 * LANES)
    prev_block = lambda b, n: jnp.maximum((b * nb + n) * SWA_BLOCKS_PER_STEP - 1, 0)
    return pl.pallas_call(
        _swa_kernel,
        grid=(batch, nb),
        in_specs=[
            pl.BlockSpec(memory_space=pltpu.SMEM),
            pl.BlockSpec((rows, q_width), lambda b, n: (b * nb + n, 0)),
            pl.BlockSpec((ATTN_BLOCK, 2 * LANES), lambda b, n: (prev_block(b, n), kv_block)),
            pl.BlockSpec((rows, 2 * LANES), lambda b, n: (b * nb + n, kv_block)),
        ],
        out_specs=pl.BlockSpec((rows, q_width), lambda b, n: (b * nb + n, 0)),
        out_shape=jax.ShapeDtypeStruct((t, q_width), BF16),
        compiler_params=_compiler_params(("parallel", "arbitrary")),
        name="swa_attention",
    )(sinks.astype(F32) * LOG2E, qkv, qkv, qkv)


STAT_GROUP = 4


def _write_stats(acc, lane, pair, ms, l):
    lane_lo = lane < HEAD_DIM
    stat = jnp.where(lane % STAT_GROUP < STAT_GROUP // 2, l, jnp.where(lane_lo, ms[0], ms[1]))
    return jnp.where((lane % HEAD_DIM) // STAT_GROUP == pair, stat, acc)


def _dil16_kernel(q_ref, kp_ref, kc_ref, vp_ref, vc_ref, o_ref, lse_ref):
    first = pl.program_id(2) == 0
    n_res = q_ref.shape[1]
    n_pairs = q_ref.shape[3] // LANES
    lane = _iota((1, LANES), 1)
    lane_lo = lane < HEAD_DIM
    kj = _iota((1, 2 * ATTN_BLOCK), 1)
    mask = _dist_mask(_iota((ATTN_BLOCK, 1), 0) + ATTN_BLOCK, kj, ATTN_BLOCK,
                      (kj >= ATTN_BLOCK) | jnp.logical_not(first))

    def body(j, lse_accs):
        cols = pl.ds(pl.multiple_of(j * LANES, LANES), LANES)
        qs = [q_ref[0, r, :, cols] for r in range(n_res)]
        ks = [jnp.concatenate([kp_ref[0, r, :, cols], kc_ref[0, r, :, cols]], axis=0) for r in range(n_res)]
        vs = [jnp.concatenate([vp_ref[0, r, :, cols], vc_ref[0, r, :, cols]], axis=0) for r in range(n_res)]
        out = []
        for r, (o, ms, ls) in enumerate(_attn_tiles(qs, ks, vs, [mask] * n_res, lane_lo)):
            o_ref[0, r, :, cols] = o.astype(o_ref.dtype)
            out.append(_write_stats(lse_accs[r], lane, j, ms, ls))
        return tuple(out)

    zeros = tuple(jnp.zeros((ATTN_BLOCK, LANES), F32) for _ in range(n_res))
    for r, acc in enumerate(lax.fori_loop(0, n_pairs, body, zeros, unroll=True)):
        lse_ref[0, r] = acc


def _dil4_kernel(q_ref, kp_ref, kc_ref, vp_ref, vc_ref, o_ref, lse_ref):
    first = pl.program_id(2) == 0
    n_sub = q_ref.shape[1]
    rows = ATTN_BLOCK // n_sub
    n_tiles = q_ref.shape[3] // rows
    n_pairs = q_ref.shape[4] // LANES
    lane = _iota((1, LANES), 1)
    lane_lo = lane < HEAD_DIM
    qi = _iota((ATTN_BLOCK, 1), 0)
    kj = _iota((1, 2 * ATTN_BLOCK), 1)
    q_pos = n_sub * (rows + qi % rows) + qi // rows
    k_pos = n_sub * (kj % (2 * rows)) + kj // (2 * rows)
    mask = _dist_mask(q_pos, k_pos, ATTN_BLOCK)
    mask_first = mask & ((kj % (2 * rows) >= rows) | jnp.logical_not(first))

    def gather(cur_ref, prev_ref, tile, cols):
        if tile == 0:
            parts = [jnp.concatenate([prev_ref[0, a, 0, :, cols], cur_ref[0, a, 0, 0:rows, cols]], axis=0)
                     for a in range(n_sub)]
        else:
            parts = [cur_ref[0, a, 0, (tile - 1) * rows:(tile + 1) * rows, cols] for a in range(n_sub)]
        return jnp.concatenate(parts, axis=0)

    def body(j, lse_accs):
        cols = pl.ds(pl.multiple_of(j * LANES, LANES), LANES)
        qs = [jnp.concatenate([q_ref[0, a, 0, t * rows:(t + 1) * rows, cols] for a in range(n_sub)], axis=0)
              for t in range(n_tiles)]
        ks = [gather(kc_ref, kp_ref, t, cols) for t in range(n_tiles)]
        vs = [gather(vc_ref, vp_ref, t, cols) for t in range(n_tiles)]
        masks = [mask_first] + [mask] * (n_tiles - 1)
        out = []
        for t, (o, ms, ls) in enumerate(_attn_tiles(qs, ks, vs, masks, lane_lo)):
            ob = o.astype(o_ref.dtype)
            for a in range(n_sub):
                o_ref[0, a, 0, t * rows:(t + 1) * rows, cols] = ob[a * rows:(a + 1) * rows, :]
            out.append(_write_stats(lse_accs[t], lane, j, ms, ls))
        return tuple(out)

    zeros = tuple(jnp.zeros((ATTN_BLOCK, LANES), F32) for _ in range(n_tiles))
    for t, acc in enumerate(lax.fori_loop(0, n_pairs, body, zeros, unroll=True)):
        for a in range(n_sub):
            lse_ref[0, a, 0, t * rows:(t + 1) * rows, :] = acc[a * rows:(a + 1) * rows, :]


def _dil1_kernel(q_ref, kp_ref, kc_ref, vp_ref, vc_ref, o_ref, lse_ref):
    first = (pl.program_id(1) == 0) & (pl.program_id(2) == 0)
    n_res = q_ref.shape[1]
    sub = ATTN_BLOCK // n_res
    n_tiles = q_ref.shape[2] // sub
    n_pairs = q_ref.shape[3] // LANES
    prev_rows = kp_ref.shape[2]
    lane = _iota((1, LANES), 1)
    lane_lo = lane < HEAD_DIM
    qi = _iota((ATTN_BLOCK, 1), 0)
    kj = _iota((1, 2 * ATTN_BLOCK), 1)
    q_pos = n_res * (sub + qi % sub) + qi // sub
    k_pos = n_res * (kj % (2 * sub)) + kj // (2 * sub)
    mask = _dist_mask(q_pos, k_pos, ATTN_BLOCK)
    mask_first = mask & ((kj % (2 * sub) >= sub) | jnp.logical_not(first))

    def tile_rows(x, lo, hi):
        return x[:, lo:hi, :].reshape(n_res * (hi - lo), LANES)

    def body(j, lse_accs):
        cols = pl.ds(pl.multiple_of(j * LANES, LANES), LANES)
        q = q_ref[0, :, :, cols].astype(F32)
        qs = [tile_rows(q, t * sub, (t + 1) * sub).astype(BF16) for t in range(n_tiles)]

        def gather(cur_ref, prev_ref):
            cur = cur_ref[0, :, :, cols].astype(F32)
            prev = prev_ref[0, :, :, cols].astype(F32)
            tiles = []
            for t in range(n_tiles):
                if t == 0:
                    x = jnp.concatenate([prev[:, prev_rows - sub:, :], cur[:, 0:sub, :]], axis=1)
                else:
                    x = cur[:, (t - 1) * sub:(t + 1) * sub, :]
                tiles.append(x.reshape(n_res * 2 * sub, LANES).astype(BF16))
            return tiles

        ks = gather(kc_ref, kp_ref)
        vs = gather(vc_ref, vp_ref)
        masks = [mask_first] + [mask] * (n_tiles - 1)
        res = _attn_tiles(qs, ks, vs, masks, lane_lo)
        o_full = jnp.concatenate([o.reshape(n_res, sub, LANES) for o, _, _ in res], axis=1)
        o_ref[0, :, :, cols] = o_full.astype(o_ref.dtype)
        return tuple(_write_stats(lse_accs[t], lane, j, ms, ls) for t, (_, ms, ls) in enumerate(res))

    zeros = tuple(jnp.zeros((ATTN_BLOCK, LANES), F32) for _ in range(n_tiles))
    accs = lax.fori_loop(0, n_pairs, body, zeros, unroll=True)
    lse_ref[0] = jnp.concatenate([acc.reshape(n_res, sub, LANES) for acc in accs], axis=1)


def _dilated_branches(qkv, batch, seq):
    t, w = qkv.shape[0], qkv.shape[1] // 3
    nsb = seq // SUPERBLOCK
    n_super = t // SUPERBLOCK
    per_res = SUPERBLOCK // RESIDUES
    outs, lses = [], []

    def in_specs(block, cur_map, prev_block, prev_map):
        cur = lambda col: pl.BlockSpec(block, lambda *g: cur_map(*g) + (col,))
        prev = lambda col: pl.BlockSpec(prev_block, lambda *g: prev_map(*g) + (col,))
        return [cur(0), prev(1), cur(1), prev(2), cur(2)]

    r16 = 4
    shp = (n_super, RESIDUES, per_res)
    cur_map = lambda b, r, s: (b * nsb + s, r, 0)
    prev_map = lambda b, r, s: (b * nsb + jnp.maximum(s - 1, 0), r, 0)
    qkv4 = qkv.reshape(shp + (3 * w,))
    o16, l16 = pl.pallas_call(
        _dil16_kernel, grid=(batch, RESIDUES // r16, nsb),
        in_specs=in_specs((1, r16, per_res, w), cur_map, (1, r16, per_res, w), prev_map),
        out_specs=[pl.BlockSpec((1, r16, per_res, w), lambda *g: cur_map(*g) + (0,)),
                   pl.BlockSpec((1, r16, per_res, LANES), lambda *g: cur_map(*g) + (0,))],
        out_shape=[jax.ShapeDtypeStruct(shp + (w,), BF16), jax.ShapeDtypeStruct(shp + (LANES,), F32)],
        compiler_params=_compiler_params(("parallel", "parallel", "arbitrary")),
        name="dilated_attention_d16",
    )(qkv4, qkv4, qkv4, qkv4, qkv4)

    shp5 = (n_super, 4, 4, per_res)
    tail = per_res // 4
    cur_map5 = lambda b, r, s: (b * nsb + s, 0, r, 0)
    prev_map5 = lambda b, r, s: (b * nsb + jnp.maximum(s - 1, 0), 0, r, per_res // tail - 1)
    qkv5 = qkv.reshape(shp5 + (3 * w,))
    o4, l4 = pl.pallas_call(
        _dil4_kernel, grid=(batch, 4, nsb),
        in_specs=in_specs((1, 4, 1, per_res, w), cur_map5, (1, 4, 1, tail, w), prev_map5),
        out_specs=[pl.BlockSpec((1, 4, 1, per_res, w), lambda *g: cur_map5(*g) + (0,)),
                   pl.BlockSpec((1, 4, 1, per_res, LANES), lambda *g: cur_map5(*g) + (0,))],
        out_shape=[jax.ShapeDtypeStruct(shp5 + (w,), BF16), jax.ShapeDtypeStruct(shp5 + (LANES,), F32)],
        compiler_params=_compiler_params(("parallel", "parallel", "arbitrary")),
        name="dilated_attention_d4",
    )(qkv5, qkv5, qkv5, qkv5, qkv5)

    run, hist = 32, 16
    runs = per_res // run
    cur_map1 = lambda b, s, c: (b * nsb + s, 0, c)

    def prev_map1(b, s, c):
        g = (b * nsb + s) * (per_res // hist) + c * (run // hist) - 1
        g = jnp.maximum(g, 0)
        return (g // (per_res // hist), 0, g % (per_res // hist))

    o1, l1 = pl.pallas_call(
        _dil1_kernel, grid=(batch, nsb, runs),
        in_specs=in_specs((1, RESIDUES, run, w), cur_map1, (1, RESIDUES, hist, w), prev_map1),
        out_specs=[pl.BlockSpec((1, RESIDUES, run, w), lambda *g: cur_map1(*g) + (0,)),
                   pl.BlockSpec((1, RESIDUES, run, LANES), lambda *g: cur_map1(*g) + (0,))],
        out_shape=[jax.ShapeDtypeStruct(shp + (w,), BF16), jax.ShapeDtypeStruct(shp + (LANES,), F32)],
        compiler_params=_compiler_params(("parallel", "arbitrary", "arbitrary")),
        name="dilated_attention_d1",
    )(qkv4, qkv4, qkv4, qkv4, qkv4)

    for o, l in ((o1, l1), (o4, l4), (o16, l16)):
        outs.append(o.reshape(t, w))
        lses.append(l.reshape(t, LANES))
    return outs, lses


def _softplus(x):
    return jnp.maximum(x, 0.0) + jnp.log1p(jnp.exp(-jnp.abs(x)))


def _silu(x):
    h = 0.5 * x
    return h * jnp.tanh(h) + h


def _expand_heads(v, head_mask, e3):
    n_heads = e3.shape[1] // HEAD_DIM
    v = jnp.where(head_mask, v, 0.0)
    hi = v.astype(BF16).astype(F32)
    r = v - hi
    mid = r.astype(BF16).astype(F32)
    lo = (r - mid).astype(BF16).astype(F32)
    cat = hi + pltpu.roll(mid, n_heads, axis=1) + pltpu.roll(lo, 2 * n_heads, axis=1)
    return jnp.dot(cat.astype(BF16), e3, preferred_element_type=F32)


def _ssd_kernel(xs_ref, bc_ref, z_ref, dt_ref, cw_ref, cb_ref, dtb_ref, a_ref, dskip_ref, gn_ref, tril_ref, e3_ref,
                shift_ref, y_ref, hist_ref, state_ref):
    chunk = SSM_CHUNK
    d_inner = z_ref.shape[1]
    n_heads = d_inner // HEAD_DIM
    group_w = d_inner // SSM_GROUPS
    heads_per_group = n_heads // SSM_GROUPS

    @pl.when(pl.program_id(1) == 0)
    def _():
        hist_ref[...] = jnp.zeros(hist_ref.shape, hist_ref.dtype)
        state_ref[...] = jnp.zeros(state_ref.shape, F32)

    parts = []
    for ref, cols in ((xs_ref, slice(0, d_inner)), (bc_ref, slice(d_inner, hist_ref.shape[1]))):
        cur = ref[...]
        ext = jnp.concatenate([hist_ref[:, cols], cur], axis=0)
        shifted = jnp.dot(shift_ref[...], ext, preferred_element_type=F32)
        conv = cb_ref[:, cols] + cw_ref[SSM_CONV - 1:SSM_CONV, cols] * cur.astype(F32)
        for j in range(SSM_CONV - 1):
            conv = conv + cw_ref[j:j + 1, cols] * shifted[j * chunk:(j + 1) * chunk]
        hist_ref[:, cols] = cur[chunk - CONV_HIST:, :]
        parts.append(_silu(conv))
    xs = parts[0]
    xbc_bc = parts[1]

    lane = _iota((1, LANES), 1)
    head_mask = lane < n_heads
    lane_lo = lane < HEAD_DIM
    e3 = e3_ref[...]

    dt = jnp.where(head_mask, _softplus(dt_ref[...] + dtb_ref[...]), 0.0)
    cum = jnp.dot(tril_ref[...], dt * a_ref[...], precision=lax.Precision.HIGHEST,
                  preferred_element_type=F32)
    cum_t = cum.T
    ecum_e = _expand_heads(jnp.exp(cum), head_mask, e3)
    dte_e = _expand_heads(jnp.exp(cum[chunk - 1:chunk, :] - cum), head_mask, e3)
    xdt = xs * _expand_heads(dt, head_mask, e3)
    xend = xdt * dte_e

    causal = _iota((chunk, chunk), 0) >= _iota((chunk, chunk), 1)

    for g in range(SSM_GROUPS):
        gcols = slice(g * group_w, (g + 1) * group_w)
        b_f32 = xbc_bc[:, g * SSM_STATE: (g + 1) * SSM_STATE]
        c_off = SSM_GROUPS * SSM_STATE
        c_bf = xbc_bc[:, c_off + g * SSM_STATE: c_off + (g + 1) * SSM_STATE].astype(BF16)
        cb = lax.dot_general(c_bf, b_f32.astype(BF16), (((1,), (1,)), ((), ())), preferred_element_type=F32)
        st = state_ref[g]
        y_off = jnp.dot(c_bf, st.astype(BF16), preferred_element_type=F32) * ecum_e[:, gcols]
        state_ref[g] = st * ecum_e[chunk - 1:chunk, gcols] + jnp.dot(
            b_f32.T.astype(BF16), xend[:, gcols].astype(BF16), preferred_element_type=F32)

        y_pairs = []
        for pair in range(heads_per_group // 2):
            h0 = g * heads_per_group + 2 * pair
            ms = []
            for h in (h0, h0 + 1):
                diff = cum[:, h:h + 1] - cum_t[h:h + 1, :]
                seg = jnp.exp(jnp.where(causal, diff, -jnp.inf))
                ms.append((cb * seg).astype(BF16))
            xp = xdt[:, h0 * HEAD_DIM: h0 * HEAD_DIM + LANES]
            x_heads = jnp.concatenate([jnp.where(lane_lo, xp, 0.0), jnp.where(lane_lo, 0.0, xp)], axis=0)
            y_pairs.append(jnp.dot(jnp.concatenate(ms, axis=1), x_heads.astype(BF16),
                                   preferred_element_type=F32))
        y = jnp.concatenate(y_pairs, axis=1) + y_off + dskip_ref[:, gcols] * xs[:, gcols]
        gy = y * _silu(z_ref[:, gcols].astype(F32))
        ms_g = jnp.mean(gy * gy, axis=-1, keepdims=True)
        y_ref[:, gcols] = (gy * lax.rsqrt(ms_g + NORM_EPS) * gn_ref[:, gcols]).astype(y_ref.dtype)


def _ssd_mixer(proj, dt_raw, conv_w, conv_b, dt_bias, a_log, d_skip, gate_norm, batch, seq):
    t = proj.shape[0]
    d_inner = gate_norm.shape[0]
    n_heads = d_inner // HEAD_DIM
    conv_dim = conv_w.shape[1]
    bc_w = conv_dim - d_inner
    nc = seq // SSM_CHUNK
    assert d_inner % bc_w == 0

    def pad_heads(v):
        return jnp.zeros((1, LANES), F32).at[0, :n_heads].set(v.astype(F32))

    tril = (_iota((SSM_CHUNK, SSM_CHUNK), 0) >= _iota((SSM_CHUNK, SSM_CHUNK), 1)).astype(F32)
    row = _iota((LANES, d_inner), 0)
    col = _iota((LANES, d_inner), 1)
    e3 = ((row < 3 * n_heads) & ((row % n_heads) == (col // HEAD_DIM))).astype(BF16)
    srow = _iota(((SSM_CONV - 1) * SSM_CHUNK, CONV_HIST + SSM_CHUNK), 0)
    scol = _iota(((SSM_CONV - 1) * SSM_CHUNK, CONV_HIST + SSM_CHUNK), 1)
    shift = (scol == CONV_HIST + srow % SSM_CHUNK - (SSM_CONV - 1 - srow // SSM_CHUNK)).astype(BF16)
    const = lambda shape: pl.BlockSpec(shape, lambda b, c: (0, 0))
    return pl.pallas_call(
        _ssd_kernel,
        grid=(batch, nc),
        in_specs=[
            pl.BlockSpec((SSM_CHUNK, d_inner), lambda b, c: (b * nc + c, 1)),
            pl.BlockSpec((SSM_CHUNK, bc_w), lambda b, c: (b * nc + c, 2 * d_inner // bc_w)),
            pl.BlockSpec((SSM_CHUNK, d_inner), lambda b, c: (b * nc + c, 0)),
            pl.BlockSpec((SSM_CHUNK, LANES), lambda b, c: (b * nc + c, 0)),
            const((SSM_CONV, conv_dim)), const((1, conv_dim)), const((1, LANES)), const((1, LANES)),
            const((1, d_inner)), const((1, d_inner)), const((SSM_CHUNK, SSM_CHUNK)), const((LANES, d_inner)),
            const(shift.shape),
        ],
        out_specs=pl.BlockSpec((SSM_CHUNK, d_inner), lambda b, c: (b * nc + c, 0)),
        out_shape=jax.ShapeDtypeStruct((t, d_inner), BF16),
        scratch_shapes=[pltpu.VMEM((CONV_HIST, conv_dim), BF16),
                        pltpu.VMEM((SSM_GROUPS, SSM_STATE, d_inner // SSM_GROUPS), F32)],
        compiler_params=_compiler_params(("parallel", "arbitrary")),
        name="ssd_mixer",
    )(proj, proj, proj, dt_raw, conv_w.astype(F32), conv_b.reshape(1, conv_dim).astype(F32),
      pad_heads(dt_bias), pad_heads(-jnp.exp(a_log.astype(F32))),
      jnp.repeat(d_skip.astype(F32), HEAD_DIM).reshape(1, d_inner), gate_norm.reshape(1, d_inner).astype(F32),
      tril, e3, shift)


def _outproj_swa_ssd_kernel(x_ref, a_ref, y_ref, wa_ref, wy_ref, o_ref):
    o_ref[...] = (x_ref[...]
                  + jnp.dot(a_ref[...], wa_ref[...], preferred_element_type=F32)
                  + jnp.dot(y_ref[...], wy_ref[...], preferred_element_type=F32))


def _outproj_swa_ssd(x, attn, y, w_attn, w_y, tm, tn):
    t, d = x.shape
    tm = min(tm, t)
    return pl.pallas_call(
        _outproj_swa_ssd_kernel,
        grid=(t // tm, d // tn),
        in_specs=[
            pl.BlockSpec((tm, tn), lambda i, j: (i, j)),
            pl.BlockSpec((tm, attn.shape[1]), lambda i, j: (i, 0)),
            pl.BlockSpec((tm, y.shape[1]), lambda i, j: (i, 0)),
            pl.BlockSpec((attn.shape[1], tn), lambda i, j: (0, j)),
            pl.BlockSpec((y.shape[1], tn), lambda i, j: (0, j)),
        ],
        out_specs=pl.BlockSpec((tm, tn), lambda i, j: (i, j)),
        out_shape=jax.ShapeDtypeStruct((t, d), F32),
        compiler_params=_compiler_params(("parallel", "arbitrary")),
        name="outproj_swa_ssd",
    )(x, attn, y, w_attn, w_y)


def _outproj_dilated_kernel(x_ref, o1_ref, o2_ref, o3_ref, l1_ref, l2_ref, l3_ref, e2_ref, w_ref, out_ref,
                            om_ref, scr_ref):
    @pl.when(pl.program_id(1) == 0)
    def _():
        tm, width = om_ref.shape
        n_res = o1_ref.shape[1]
        lane = _iota((1, LANES), 1)
        head_mask = lane % STAT_GROUP == 0
        stats = [l_ref[...].reshape(tm, LANES) for l_ref in (l1_ref, l2_ref, l3_ref)]
        maxes = [pltpu.roll(st, LANES - STAT_GROUP // 2, axis=1) for st in stats]
        m = jnp.maximum(jnp.maximum(maxes[0], maxes[1]), maxes[2])
        es = [jnp.exp2(mx - m) for mx in maxes]
        den = sum(e * st for e, st in zip(es, stats))
        inv = 1.0 / jnp.where(head_mask, den, 1.0)
        acc = None
        for e, o_ref in zip(es, (o1_ref, o2_ref, o3_ref)):
            w = jnp.where(head_mask, e * inv, 0.0)
            hi = w.astype(BF16).astype(F32)
            lo = (w - hi).astype(BF16).astype(F32)
            cat = (hi + pltpu.roll(lo, 1, axis=1)).astype(BF16)
            term = jnp.dot(cat, e2_ref[...], preferred_element_type=F32) * o_ref[...].reshape(tm, width).astype(F32)
            acc = term if acc is None else acc + term
        rows = tm // n_res
        pitch = scr_ref.shape[1] // n_res
        for c in range(width // LANES):
            for r in range(n_res):
                scr_ref[c, r * pitch:r * pitch + rows, :] = acc[r * rows:(r + 1) * rows, c * LANES:(c + 1) * LANES]
        for n in range(rows):
            om_ref[n * n_res:(n + 1) * n_res, :] = jnp.concatenate(
                [scr_ref[c, pl.ds(n, n_res, stride=pitch), :] for c in range(width // LANES)], axis=1).astype(BF16)

    out_ref[...] = x_ref[...] + jnp.dot(om_ref[...], w_ref[...], preferred_element_type=F32)


def _outproj_dilated(x, outs, lses, w_o, tm, tn):
    t, d = x.shape
    width = w_o.shape[0]
    n_heads = width // HEAD_DIM
    tm = min(tm, t)
    rows = tm // RESIDUES
    per_super = SUPERBLOCK // tm
    row = _iota((LANES, width), 0)
    col = _iota((LANES, width), 1)
    stat_head = 2 * ((row % HEAD_DIM) // STAT_GROUP) + row // HEAD_DIM
    e2 = ((row % STAT_GROUP < 2) & (stat_head == col // HEAD_DIM)).astype(BF16)
    assert n_heads * STAT_GROUP // 2 <= HEAD_DIM
    shp = (t // SUPERBLOCK, RESIDUES, SUPERBLOCK // RESIDUES)
    o_spec = pl.BlockSpec((1, RESIDUES, rows, width), lambda i, j: (i // per_super, 0, i % per_super, 0))
    l_spec = pl.BlockSpec((1, RESIDUES, rows, LANES), lambda i, j: (i // per_super, 0, i % per_super, 0))
    return pl.pallas_call(
        _outproj_dilated_kernel,
        grid=(t // tm, d // tn),
        in_specs=[pl.BlockSpec((tm, tn), lambda i, j: (i, j)), o_spec, o_spec, o_spec, l_spec, l_spec, l_spec,
                  pl.BlockSpec((LANES, width), lambda i, j: (0, 0)),
                  pl.BlockSpec((width, tn), lambda i, j: (0, j),
                               pipeline_mode=pl.Buffered(1) if tn == d else None)],
        out_specs=pl.BlockSpec((tm, tn), lambda i, j: (i, j)),
        out_shape=jax.ShapeDtypeStruct((t, d), F32),
        scratch_shapes=[pltpu.VMEM((tm, width), BF16),
                        pltpu.VMEM((width // LANES, RESIDUES * (rows + STAGE_PITCH - RESIDUES), LANES), F32)],
        compiler_params=_compiler_params(("parallel", "arbitrary")),
        name="outproj_dilated",
    )(x, *[o.reshape(shp + (width,)) for o in outs], *[l.reshape(shp + (LANES,)) for l in lses], e2, w_o)


def _mlp_kernel(x_ref, g_ref, wu_ref, wd_ref, o_ref, h_ref):
    @pl.when(pl.program_id(1) == 0)
    def _():
        h_ref[...] = _rms_norm_rows(x_ref[...], g_ref[...]).astype(BF16)
        o_ref[...] = x_ref[...]

    u = jnp.dot(h_ref[...], wu_ref[...], preferred_element_type=F32)
    a = jnp.square(jnp.maximum(u, 0.0)).astype(BF16)
    o_ref[...] += jnp.dot(a, wd_ref[...], preferred_element_type=F32)


def _mlp(x, gain, w_up, w_down, tm, tf):
    t, d = x.shape
    ff = w_up.shape[1]
    tm = min(tm, t)
    return pl.pallas_call(
        _mlp_kernel,
        grid=(t // tm, ff // tf),
        in_specs=[
            pl.BlockSpec((tm, d), lambda i, j: (i, 0)),
            pl.BlockSpec((1, d), lambda i, j: (0, 0)),
            pl.BlockSpec((d, tf), lambda i, j: (0, j)),
            pl.BlockSpec((tf, d), lambda i, j: (j, 0)),
        ],
        out_specs=pl.BlockSpec((tm, d), lambda i, j: (i, 0)),
        out_shape=jax.ShapeDtypeStruct((t, d), F32),
        scratch_shapes=[pltpu.VMEM((tm, d), BF16)],
        compiler_params=_compiler_params(("parallel", "arbitrary")),
        name="sqrelu_mlp",
    )(x, gain.reshape(1, d).astype(F32), w_up, w_down)


def _swa_ssd_layer(x, gain, w_in, q_norm, k_norm, sinks, conv_w, conv_b, dt_bias, a_log, d_skip, gate_norm,
                   w_out, batch, seq):
    d_inner = gate_norm.shape[0]
    n_ssm_heads = dt_bias.shape[0]
    n_q_heads = sinks.shape[0]
    q_w = n_q_heads * HEAD_DIM
    kv_w = SWA_KV_HEADS * HEAD_DIM
    conv_dim = conv_w.shape[1]
    splits = (q_w, q_w + kv_w, q_w + 2 * kv_w, q_w + 2 * kv_w + d_inner, q_w + 2 * kv_w + d_inner + conv_dim)
    wq, wk, wv, wz, wxbc, wdt = jnp.split(w_in, splits, axis=1)
    w_qkv_dt = jnp.concatenate([wq, wk, wv, jnp.pad(wdt, ((0, 0), (0, LANES - n_ssm_heads)))], axis=1).astype(BF16)
    w_ssm = jnp.concatenate([wz, wxbc], axis=1).astype(BF16)
    qkv_w = q_w + 2 * kv_w
    head_gain = jnp.concatenate([jnp.tile(q_norm.astype(F32) * Q_SCALE, n_q_heads),
                                 jnp.tile(k_norm.astype(F32), SWA_KV_HEADS), jnp.ones((kv_w,), F32)])
    head_mask = jnp.concatenate([jnp.ones((q_w + kv_w,), F32), jnp.zeros((kv_w,), F32)])

    qkv, dt_raw = _norm_matmul_headnorm(x, gain, w_qkv_dt, head_gain, head_mask, tm=1024, tn=qkv_w,
                                        name="norm_qkv_dt_proj", n_extra=LANES)
    proj = _norm_matmul(x, gain, w_ssm, tm=1024, tn=w_ssm.shape[1] // 4, name="norm_ssm_proj")
    attn = _swa_attention(qkv, sinks, batch, seq, q_w)
    y = _ssd_mixer(proj, dt_raw, conv_w, conv_b, dt_bias, a_log, d_skip, gate_norm, batch, seq)
    w_out = w_out.astype(BF16)
    return _outproj_swa_ssd(x, attn, y, w_out[:q_w], w_out[q_w:], tm=1024, tn=1024)


def _dilated_layer(x, gain, w_qkv, q_norm, k_norm, w_o, batch, seq):
    width = w_o.shape[0]
    n_heads = width // HEAD_DIM
    w_qkv = w_qkv.astype(BF16)
    head_gain = jnp.concatenate([jnp.tile(q_norm.astype(F32) * Q_SCALE, n_heads),
                                 jnp.tile(k_norm.astype(F32), n_heads), jnp.ones((width,), F32)])
    head_mask = jnp.concatenate([jnp.ones((2 * width,), F32), jnp.zeros((width,), F32)])
    tn = 1024
    qkv = _norm_matmul_headnorm(x, gain, w_qkv, head_gain, head_mask, tm=1024, tn=tn, name="norm_qkv_proj",
                                interleave=RESIDUES, norm_tiles=2 * width // tn)
    outs, lses = _dilated_branches(qkv, batch, seq)
    return _outproj_dilated(x, outs, lses, w_o.astype(BF16), tm=512, tn=w_o.shape[1])


def kernel(x, norm_mix, norm_ffn, w_up, w_down, ab_w_in, ab_q_norm, ab_k_norm, ab_sinks, ab_conv_w, ab_conv_b,
           ab_dt_bias, ab_a_log, ab_d_skip, ab_gate_norm, ab_w_out, c_w_qkv, c_q_norm, c_k_norm, c_w_o):
    batch, seq, d_model = x.shape
    depth = norm_mix.shape[0]
    assert seq % SUPERBLOCK == 0 and seq % SSM_CHUNK == 0
    h = x.reshape(batch * seq, d_model).astype(F32)
    for layer in range(depth):
        i = layer // 2
        if layer % 2 == 0:
            h = _swa_ssd_layer(h, norm_mix[layer], ab_w_in[i], ab_q_norm[i], ab_k_norm[i], ab_sinks[i],
                               ab_conv_w[i], ab_conv_b[i], ab_dt_bias[i], ab_a_log[i], ab_d_skip[i],
                               ab_gate_norm[i], ab_w_out[i], batch, seq)
        else:
            h = _dilated_layer(h, norm_mix[layer], c_w_qkv[i], c_q_norm[i], c_k_norm[i], c_w_o[i], batch, seq)
        h = _mlp(h, norm_ffn[layer], w_up[layer].astype(BF16), w_down[layer].astype(BF16), tm=512, tf=1024)
    return h.reshape(batch, seq, d_model).astype(x.dtype)
```

```python
import functools
import math

import jax
import jax.numpy as jnp
from jax import lax
from jax.experimental import pallas as pl
from jax.experimental.pallas import tpu as pltpu

F32 = jnp.float32
BF16 = jnp.bfloat16

NORM_EPS = 1e-5
HEAD_DIM = 64
ATTN_BLOCK = 128
SWA_KV_HEADS = 2
SWA_WINDOW = 128
SSM_GROUPS = 4
SSM_STATE = 128
SSM_CONV = 4
SSM_CHUNK = 128
DIL_BRANCHES = ((128, 1), (512, 4), (2048, 16))
RESIDUES = 16
SUPERBLOCK = RESIDUES * ATTN_BLOCK
STAGE_PITCH = 20

LOG2E = math.log2(math.e)
Q_SCALE = HEAD_DIM ** -0.5 * LOG2E

LANES = 128
MXU_WIDTH = 256
CONV_HIST = 16
VMEM_LIMIT_BYTES = 56 * 1024 * 1024


def _compiler_params(semantics):
    return pltpu.CompilerParams(dimension_semantics=semantics, vmem_limit_bytes=VMEM_LIMIT_BYTES)


def _rms_norm_rows(x, gain):
    ms = jnp.mean(x * x, axis=-1, keepdims=True)
    return x * lax.rsqrt(ms + NORM_EPS) * gain


def _norm_rows_into(h_ref, x_ref, g_ref, interleave, xs_ref):
    if interleave is None:
        h_ref[...] = _rms_norm_rows(x_ref[...], g_ref[...]).astype(BF16)
        return
    n_chunks = xs_ref.shape[0]
    rows = x_ref.shape[0] // interleave
    pitch = xs_ref.shape[1] // rows
    for c in range(n_chunks):
        for k in range(rows):
            xs_ref[c, k * pitch:k * pitch + interleave, :] = x_ref[k * interleave:(k + 1) * interleave,
                                                                   c * LANES:(c + 1) * LANES]
    for r in range(interleave):
        xr = jnp.concatenate([xs_ref[c, pl.ds(r, rows, stride=pitch), :] for c in range(n_chunks)], axis=1)
        h_ref[r * rows:(r + 1) * rows, :] = _rms_norm_rows(xr, g_ref[...]).astype(BF16)


def _norm_matmul_kernel(x_ref, g_ref, w_ref, o_ref, h_ref, xs_ref=None, *, interleave):
    @pl.when(pl.program_id(1) == 0)
    def _():
        _norm_rows_into(h_ref, x_ref, g_ref, interleave, xs_ref)

    acc = jnp.dot(h_ref[...], w_ref[...], preferred_element_type=F32)
    o_ref[...] = acc.astype(o_ref.dtype).reshape(o_ref.shape)


def _norm_matmul_headnorm_kernel(x_ref, g_ref, w_ref, hg_ref, hm_ref, ones_ref, *rest, interleave, n_extra,
                                 norm_tiles):
    rest = list(rest)
    o_ref = rest.pop(0)
    extra_ref = rest.pop(0) if n_extra else None
    h_ref = rest.pop(0)
    xs_ref = rest.pop(0) if interleave is not None else None
    j = pl.program_id(1)

    @pl.when(j == 0)
    def _():
        _norm_rows_into(h_ref, x_ref, g_ref, interleave, xs_ref)

    acc = jnp.dot(h_ref[...], w_ref[...], preferred_element_type=F32)
    n_main = acc.shape[1] - n_extra

    def head_norm_store():
        chunks = []
        for c in range(n_main // MXU_WIDTH):
            cols = slice(c * MXU_WIDTH, (c + 1) * MXU_WIDTH)
            a = acc[:, cols]
            ms = jnp.dot((a * a).astype(BF16), ones_ref[...], preferred_element_type=F32)
            r = lax.rsqrt(ms + NORM_EPS)
            if norm_tiles is None:
                r = jnp.where(hm_ref[:, cols] > 0.0, r, 1.0)
            chunks.append((a * r * hg_ref[:, cols]).astype(o_ref.dtype))
        o_ref[...] = jnp.concatenate(chunks, axis=1).reshape(o_ref.shape)

    if norm_tiles is None:
        head_norm_store()
    else:
        pl.when(j < norm_tiles)(head_norm_store)

        @pl.when(j >= norm_tiles)
        def _():
            o_ref[...] = acc.astype(o_ref.dtype).reshape(o_ref.shape)
    if n_extra:
        extra_ref[...] = acc[:, n_main:]


def _proj_scratch(tm, d, interleave):
    scratch = [pltpu.VMEM((tm, d), BF16)]
    if interleave is not None:
        scratch.append(pltpu.VMEM((d // LANES, tm // interleave * STAGE_PITCH, LANES), F32))
    return scratch


def _row_specs(t, tm, tn, interleave):
    if interleave is None:
        return (lambda n: jax.ShapeDtypeStruct((t, n), BF16)), pl.BlockSpec((tm, tn), lambda i, j: (i, j))
    per_super = SUPERBLOCK // tm
    rows = tm // interleave
    shape = lambda n: jax.ShapeDtypeStruct((t // SUPERBLOCK, interleave, SUPERBLOCK // interleave, n), BF16)
    spec = pl.BlockSpec((1, interleave, rows, tn), lambda i, j: (i // per_super, 0, i % per_super, j))
    return shape, spec


def _norm_matmul(x, gain, w, tm, tn, name, interleave=None):
    t, d = x.shape
    n = w.shape[1]
    tm = min(tm, t)
    shape, out_spec = _row_specs(t, tm, tn, interleave)
    out = pl.pallas_call(
        functools.partial(_norm_matmul_kernel, interleave=interleave),
        grid=(t // tm, n // tn),
        in_specs=[
            pl.BlockSpec((tm, d), lambda i, j: (i, 0)),
            pl.BlockSpec((1, d), lambda i, j: (0, 0)),
            pl.BlockSpec((d, tn), lambda i, j: (0, j)),
        ],
        out_specs=out_spec,
        out_shape=shape(n),
        scratch_shapes=_proj_scratch(tm, d, interleave),
        compiler_params=_compiler_params(("parallel", "arbitrary")),
        name=name,
    )(x, gain.reshape(1, d).astype(F32), w)
    return out.reshape(t, n)


def _head_ones():
    r = lax.broadcasted_iota(jnp.int32, (MXU_WIDTH, MXU_WIDTH), 0)
    c = lax.broadcasted_iota(jnp.int32, (MXU_WIDTH, MXU_WIDTH), 1)
    return jnp.where((r // HEAD_DIM) == (c // HEAD_DIM), 1.0 / HEAD_DIM, 0.0).astype(BF16)


def _norm_matmul_headnorm(x, gain, w, head_gain, head_mask, tm, tn, name, interleave=None, n_extra=0,
                          norm_tiles=None):
    t, d = x.shape
    n_main = w.shape[1] - n_extra
    tm = min(tm, t)
    assert n_extra == 0 or tn == n_main
    shape, out_spec = _row_specs(t, tm, tn, interleave)
    out_shapes = [shape(n_main)]
    out_specs = [out_spec]
    if n_extra:
        out_shapes.append(jax.ShapeDtypeStruct((t, n_extra), F32))
        out_specs.append(pl.BlockSpec((tm, n_extra), lambda i, j: (i, 0)))
    outs = pl.pallas_call(
        functools.partial(_norm_matmul_headnorm_kernel, interleave=interleave, n_extra=n_extra,
                          norm_tiles=norm_tiles),
        grid=(t // tm, n_main // tn),
        in_specs=[
            pl.BlockSpec((tm, d), lambda i, j: (i, 0)),
            pl.BlockSpec((1, d), lambda i, j: (0, 0)),
            pl.BlockSpec((d, tn + n_extra), lambda i, j: (0, j)),
            pl.BlockSpec((1, tn), lambda i, j: (0, j)),
            pl.BlockSpec((1, tn), lambda i, j: (0, j)),
            pl.BlockSpec((MXU_WIDTH, MXU_WIDTH), lambda i, j: (0, 0)),
        ],
        out_specs=out_specs,
        out_shape=out_shapes,
        scratch_shapes=_proj_scratch(tm, d, interleave),
        compiler_params=_compiler_params(("parallel", "arbitrary")),
        name=name,
    )(x, gain.reshape(1, d).astype(F32), w, head_gain.reshape(1, n_main).astype(F32),
      head_mask.reshape(1, n_main).astype(F32), _head_ones())
    if n_extra:
        return outs[0].reshape(t, n_main), outs[1]
    return outs[0].reshape(t, n_main)


def _lane_lo():
    return lax.broadcasted_iota(jnp.int32, (1, LANES), 1) < HEAD_DIM


def _attn_tiles(qs, ks, vs, masks, lane_lo, sinks=None, normalize=False):
    sel_rows = _iota((4 * ATTN_BLOCK, LANES), 0) < 2 * ATTN_BLOCK
    row_sum = (sel_rows == (_iota((4 * ATTN_BLOCK, LANES), 1) < HEAD_DIM)).astype(BF16)
    results = []
    for i, (q, k, v, mask) in enumerate(zip(qs, ks, vs, masks)):
        zq = jnp.zeros_like(q)
        q_heads = jnp.concatenate([jnp.where(lane_lo, q, zq), jnp.where(lane_lo, zq, q)], axis=0)
        s2 = lax.dot_general(q_heads, k, (((1,), (1,)), ((), ())), preferred_element_type=F32)
        ps, ms = [], []
        for idx in range(2):
            s = jnp.where(mask, s2[idx * ATTN_BLOCK:(idx + 1) * ATTN_BLOCK], -jnp.inf)
            m = jnp.max(s, axis=-1, keepdims=True)
            if sinks is not None:
                m = jnp.maximum(m, sinks[i][idx])
            ps.append(jnp.exp2(s - m).astype(BF16))
            ms.append(m)
        zv = jnp.zeros_like(v)
        v_heads = jnp.concatenate([jnp.where(lane_lo, v, zv), jnp.where(lane_lo, zv, v)], axis=0)
        res = jnp.dot(jnp.concatenate(ps, axis=1), jnp.concatenate([v_heads, row_sum], axis=1),
                      preferred_element_type=F32)
        o, l = res[:, :LANES], res[:, LANES:]
        if sinks is not None:
            l = l + jnp.where(lane_lo, jnp.exp2(sinks[i][0] - ms[0]), jnp.exp2(sinks[i][1] - ms[1]))
        if normalize:
            o = o * (1.0 / l)
        results.append((o, ms, l))
    return results


def _dist_mask(q_pos, k_pos, max_dist, key_exists=None):
    dist = q_pos - k_pos
    mask = (dist >= 0) & (dist <= max_dist)
    if key_exists is not None:
        mask = mask & key_exists
    return mask


def _iota(shape, dim):
    return lax.broadcasted_iota(jnp.int32, shape, dim)


SWA_BLOCKS_PER_STEP = 8


def _swa_kernel(sinks_ref, q_ref, kvp_ref, kvc_ref, o_ref):
    first = pl.program_id(1) == 0
    lane_lo = _lane_lo()
    kj = _iota((1, 2 * ATTN_BLOCK), 1)
    mask = _dist_mask(_iota((ATTN_BLOCK, 1), 0) + ATTN_BLOCK, kj, SWA_WINDOW - 1)
    mask_first = mask & ((kj >= ATTN_BLOCK) | jnp.logical_not(first))
    kv = jnp.concatenate([kvp_ref[...], kvc_ref[...]], axis=0).astype(F32)

    def dup_heads(x):
        xr = pltpu.roll(x, HEAD_DIM, axis=1)
        return jnp.where(lane_lo, x, xr).astype(BF16), jnp.where(lane_lo, xr, x).astype(BF16)

    k_dup = dup_heads(kv[:, :LANES])
    v_dup = dup_heads(kv[:, LANES:])
    n_blocks = q_ref.shape[0] // ATTN_BLOCK
    n_pairs = q_ref.shape[1] // LANES
    pairs_per_kv = n_pairs // SWA_KV_HEADS
    qs, ks, vs, masks, sinks = [], [], [], [], []
    for blk in range(n_blocks):
        rows = slice(blk * ATTN_BLOCK, (blk + 1) * ATTN_BLOCK)
        keys = slice(blk * ATTN_BLOCK, (blk + 2) * ATTN_BLOCK)
        for j in range(n_pairs):
            kh = j // pairs_per_kv
            qs.append(q_ref[rows, j * LANES:(j + 1) * LANES])
            ks.append(k_dup[kh][keys])
            vs.append(v_dup[kh][keys])
            masks.append(mask_first if blk == 0 else mask)
            sinks.append((sinks_ref[2 * j], sinks_ref[2 * j + 1]))
    for idx, (o, _, _) in enumerate(_attn_tiles(qs, ks, vs, masks, lane_lo, sinks, normalize=True)):
        blk, j = divmod(idx, n_pairs)
        o_ref[blk * ATTN_BLOCK:(blk + 1) * ATTN_BLOCK, j * LANES:(j + 1) * LANES] = o.astype(o_ref.dtype)


def _swa_attention(qkv, sinks, batch, seq, q_width):
    t = qkv.shape[0]
    rows = SWA_BLOCKS_PER_STEP * ATTN_BLOCK
    nb = seq // rows
    kv_block = q_width // (2 * LANES)
    prev_block = lambda b, n: jnp.maximum((b * nb + n) * SWA_BLOCKS_PER_STEP - 1, 0)
    return pl.pallas_call(
        _swa_kernel,
        grid=(batch, nb),
        in_specs=[
            pl.BlockSpec(memory_space=pltpu.SMEM),
            pl.BlockSpec((rows, q_width), lambda b, n: (b * nb + n, 0)),
            pl.BlockSpec((ATTN_BLOCK, 2 * LANES), lambda b, n: (prev_block(b, n), kv_block)),
            pl.BlockSpec((rows, 2 * LANES), lambda b, n: (b * nb + n, kv_block)),
        ],
        out_specs=pl.BlockSpec((rows, q_width), lambda b, n: (b * nb + n, 0)),
        out_shape=jax.ShapeDtypeStruct((t, q_width), BF16),
        compiler_params=_compiler_params(("parallel", "arbitrary")),
        name="swa_attention",
    )(sinks.astype(F32) * LOG2E, qkv, qkv, qkv)


STAT_GROUP = 4


def _write_stats(acc, lane, pair, ms, l):
    lane_lo = lane < HEAD_DIM
    stat = jnp.where(lane % STAT_GROUP < STAT_GROUP // 2, l, jnp.where(lane_lo, ms[0], ms[1]))
    return jnp.where((lane % HEAD_DIM) // STAT_GROUP == pair, stat, acc)


def _dil16_kernel(q_ref, kp_ref, kc_ref, vp_ref, vc_ref, o_ref, lse_ref):
    first = pl.program_id(2) == 0
    n_res = q_ref.shape[1]
    n_pairs = q_ref.shape[3] // LANES
    lane = _iota((1, LANES), 1)
    lane_lo = lane < HEAD_DIM
    kj = _iota((1, 2 * ATTN_BLOCK), 1)
    mask = _dist_mask(_iota((ATTN_BLOCK, 1), 0) + ATTN_BLOCK, kj, ATTN_BLOCK,
                      (kj >= ATTN_BLOCK) | jnp.logical_not(first))

    def body(j, lse_accs):
        cols = pl.ds(pl.multiple_of(j * LANES, LANES), LANES)
        qs = [q_ref[0, r, :, cols] for r in range(n_res)]
        ks = [jnp.concatenate([kp_ref[0, r, :, cols], kc_ref[0, r, :, cols]], axis=0) for r in range(n_res)]
        vs = [jnp.concatenate([vp_ref[0, r, :, cols], vc_ref[0, r, :, cols]], axis=0) for r in range(n_res)]
        out = []
        for r, (o, ms, ls) in enumerate(_attn_tiles(qs, ks, vs, [mask] * n_res, lane_lo)):
            o_ref[0, r, :, cols] = o.astype(o_ref.dtype)
            out.append(_write_stats(lse_accs[r], lane, j, ms, ls))
        return tuple(out)

    zeros = tuple(jnp.zeros((ATTN_BLOCK, LANES), F32) for _ in range(n_res))
    for r, acc in enumerate(lax.fori_loop(0, n_pairs, body, zeros, unroll=True)):
        lse_ref[0, r] = acc


def _dil4_kernel(q_ref, kp_ref, kc_ref, vp_ref, vc_ref, o_ref, lse_ref):
    first = pl.program_id(2) == 0
    n_sub = q_ref.shape[1]
    rows = ATTN_BLOCK // n_sub
    n_tiles = q_ref.shape[3] // rows
    n_pairs = q_ref.shape[4] // LANES
    lane = _iota((1, LANES), 1)
    lane_lo = lane < HEAD_DIM
    qi = _iota((ATTN_BLOCK, 1), 0)
    kj = _iota((1, 2 * ATTN_BLOCK), 1)
    q_pos = n_sub * (rows + qi % rows) + qi // rows
    k_pos = n_sub * (kj % (2 * rows)) + kj // (2 * rows)
    mask = _dist_mask(q_pos, k_pos, ATTN_BLOCK)
    mask_first = mask & ((kj % (2 * rows) >= rows) | jnp.logical_not(first))

    def gather(cur_ref, prev_ref, tile, cols):
        if tile == 0:
            parts = [jnp.concatenate([prev_ref[0, a, 0, :, cols], cur_ref[0, a, 0, 0:rows, cols]], axis=0)
                     for a in range(n_sub)]
        else:
            parts = [cur_ref[0, a, 0, (tile - 1) * rows:(tile + 1) * rows, cols] for a in range(n_sub)]
        return jnp.concatenate(parts, axis=0)

    def body(j, lse_accs):
        cols = pl.ds(pl.multiple_of(j * LANES, LANES), LANES)
        qs = [jnp.concatenate([q_ref[0, a, 0, t * rows:(t + 1) * rows, cols] for a in range(n_sub)], axis=0)
              for t in range(n_tiles)]
        ks = [gather(kc_ref, kp_ref, t, cols) for t in range(n_tiles)]
        vs = [gather(vc_ref, vp_ref, t, cols) for t in range(n_tiles)]
        masks = [mask_first] + [mask] * (n_tiles - 1)
        out = []
        for t, (o, ms, ls) in enumerate(_attn_tiles(qs, ks, vs, masks, lane_lo)):
            ob = o.astype(o_ref.dtype)
            for a in range(n_sub):
                o_ref[0, a, 0, t * rows:(t + 1) * rows, cols] = ob[a * rows:(a + 1) * rows, :]
            out.append(_write_stats(lse_accs[t], lane, j, ms, ls))
        return tuple(out)

    zeros = tuple(jnp.zeros((ATTN_BLOCK, LANES), F32) for _ in range(n_tiles))
    for t, acc in enumerate(lax.fori_loop(0, n_pairs, body, zeros, unroll=True)):
        for a in range(n_sub):
            lse_ref[0, a, 0, t * rows:(t + 1) * rows, :] = acc[a * rows:(a + 1) * rows, :]


def _dil1_kernel(q_ref, kp_ref, kc_ref, vp_ref, vc_ref, o_ref, lse_ref):
    first = (pl.program_id(1) == 0) & (pl.program_id(2) == 0)
    n_res = q_ref.shape[1]
    sub = ATTN_BLOCK // n_res
    n_tiles = q_ref.shape[2] // sub
    n_pairs = q_ref.shape[3] // LANES
    prev_rows = kp_ref.shape[2]
    lane = _iota((1, LANES), 1)
    lane_lo = lane < HEAD_DIM
    qi = _iota((ATTN_BLOCK, 1), 0)
    kj = _iota((1, 2 * ATTN_BLOCK), 1)
    q_pos = n_res * (sub + qi % sub) + qi // sub
    k_pos = n_res * (kj % (2 * sub)) + kj // (2 * sub)
    mask = _dist_mask(q_pos, k_pos, ATTN_BLOCK)
    mask_first = mask & ((kj % (2 * sub) >= sub) | jnp.logical_not(first))

    def tile_rows(x, lo, hi):
        return x[:, lo:hi, :].reshape(n_res * (hi - lo), LANES)

    def body(j, lse_accs):
        cols = pl.ds(pl.multiple_of(j * LANES, LANES), LANES)
        q = q_ref[0, :, :, cols].astype(F32)
        qs = [tile_rows(q, t * sub, (t + 1) * sub).astype(BF16) for t in range(n_tiles)]

        def gather(cur_ref, prev_ref):
            cur = cur_ref[0, :, :, cols].astype(F32)
            prev = prev_ref[0, :, :, cols].astype(F32)
            tiles = []
            for t in range(n_tiles):
                if t == 0:
                    x = jnp.concatenate([prev[:, prev_rows - sub:, :], cur[:, 0:sub, :]], axis=1)
                else:
                    x = cur[:, (t - 1) * sub:(t + 1) * sub, :]
                tiles.append(x.reshape(n_res * 2 * sub, LANES).astype(BF16))
            return tiles

        ks = gather(kc_ref, kp_ref)
        vs = gather(vc_ref, vp_ref)
        masks = [mask_first] + [mask] * (n_tiles - 1)
        res = _attn_tiles(qs, ks, vs, masks, lane_lo)
        o_full = jnp.concatenate([o.reshape(n_res, sub, LANES) for o, _, _ in res], axis=1)
        o_ref[0, :, :, cols] = o_full.astype(o_ref.dtype)
        return tuple(_write_stats(lse_accs[t], lane, j, ms, ls) for t, (_, ms, ls) in enumerate(res))

    zeros = tuple(jnp.zeros((ATTN_BLOCK, LANES), F32) for _ in range(n_tiles))
    accs = lax.fori_loop(0, n_pairs, body, zeros, unroll=True)
    lse_ref[0] = jnp.concatenate([acc.reshape(n_res, sub, LANES) for acc in accs], axis=1)


def _dilated_branches(qkv, batch, seq):
    t, w = qkv.shape[0], qkv.shape[1] // 3
    nsb = seq // SUPERBLOCK
    n_super = t // SUPERBLOCK
    per_res = SUPERBLOCK // RESIDUES
    outs, lses = [], []

    def in_specs(block, cur_map, prev_block, prev_map):
        cur = lambda col: pl.BlockSpec(block, lambda *g: cur_map(*g) + (col,))
        prev = lambda col: pl.BlockSpec(prev_block, lambda *g: prev_map(*g) + (col,))
        return [cur(0), prev(1), cur(1), prev(2), cur(2)]

    r16 = 4
    shp = (n_super, RESIDUES, per_res)
    cur_map = lambda b, r, s: (b * nsb + s, r, 0)
    prev_map = lambda b, r, s: (b * nsb + jnp.maximum(s - 1, 0), r, 0)
    qkv4 = qkv.reshape(shp + (3 * w,))
    o16, l16 = pl.pallas_call(
        _dil16_kernel, grid=(batch, RESIDUES // r16, nsb),
        in_specs=in_specs((1, r16, per_res, w), cur_map, (1, r16, per_res, w), prev_map),
        out_specs=[pl.BlockSpec((1, r16, per_res, w), lambda *g: cur_map(*g) + (0,)),
                   pl.BlockSpec((1, r16, per_res, LANES), lambda *g: cur_map(*g) + (0,))],
        out_shape=[jax.ShapeDtypeStruct(shp + (w,), BF16), jax.ShapeDtypeStruct(shp + (LANES,), F32)],
        compiler_params=_compiler_params(("parallel", "parallel", "arbitrary")),
        name="dilated_attention_d16",
    )(qkv4, qkv4, qkv4, qkv4, qkv4)

    shp5 = (n_super, 4, 4, per_res)
    tail = per_res // 4
    cur_map5 = lambda b, r, s: (b * nsb + s, 0, r, 0)
    prev_map5 = lambda b, r, s: (b * nsb + jnp.maximum(s - 1, 0), 0, r, per_res // tail - 1)
    qkv5 = qkv.reshape(shp5 + (3 * w,))
    o4, l4 = pl.pallas_call(
        _dil4_kernel, grid=(batch, 4, nsb),
        in_specs=in_specs((1, 4, 1, per_res, w), cur_map5, (1, 4, 1, tail, w), prev_map5),
        out_specs=[pl.BlockSpec((1, 4, 1, per_res, w), lambda *g: cur_map5(*g) + (0,)),
                   pl.BlockSpec((1, 4, 1, per_res, LANES), lambda *g: cur_map5(*g) + (0,))],
        out_shape=[jax.ShapeDtypeStruct(shp5 + (w,), BF16), jax.ShapeDtypeStruct(shp5 + (LANES,), F32)],
        compiler_params=_compiler_params(("parallel", "parallel", "arbitrary")),
        name="dilated_attention_d4",
    )(qkv5, qkv5, qkv5, qkv5, qkv5)

    run, hist = 32, 16
    runs = per_res // run
    cur_map1 = lambda b, s, c: (b * nsb + s, 0, c)

    def prev_map1(b, s, c):
        g = (b * nsb + s) * (per_res // hist) + c * (run // hist) - 1
        g = jnp.maximum(g, 0)
        return (g // (per_res // hist), 0, g % (per_res // hist))

    o1, l1 = pl.pallas_call(
        _dil1_kernel, grid=(batch, nsb, runs),
        in_specs=in_specs((1, RESIDUES, run, w), cur_map1, (1, RESIDUES, hist, w), prev_map1),
        out_specs=[pl.BlockSpec((1, RESIDUES, run, w), lambda *g: cur_map1(*g) + (0,)),
                   pl.BlockSpec((1, RESIDUES, run, LANES), lambda *g: cur_map1(*g) + (0,))],
        out_shape=[jax.ShapeDtypeStruct(shp + (w,), BF16), jax.ShapeDtypeStruct(shp + (LANES,), F32)],
        compiler_params=_compiler_params(("parallel", "arbitrary", "arbitrary")),
        name="dilated_attention_d1",
    )(qkv4, qkv4, qkv4, qkv4, qkv4)

    for o, l in ((o1, l1), (o4, l4), (o16, l16)):
        outs.append(o.reshape(t, w))
        lses.append(l.reshape(t, LANES))
    return outs, lses


def _softplus(x):
    return jnp.maximum(x, 0.0) + jnp.log1p(jnp.exp(-jnp.abs(x)))


def _silu(x):
    h = 0.5 * x
    return h * jnp.tanh(h) + h


def _expand_heads(v, head_mask, e3):
    n_heads = e3.shape[1] // HEAD_DIM
    v = jnp.where(head_mask, v, 0.0)
    hi = v.astype(BF16).astype(F32)
    r = v - hi
    mid = r.astype(BF16).astype(F32)
    lo = (r - mid).astype(BF16).astype(F32)
    cat = hi + pltpu.roll(mid, n_heads, axis=1) + pltpu.roll(lo, 2 * n_heads, axis=1)
    return jnp.dot(cat.astype(BF16), e3, preferred_element_type=F32)


def _ssd_kernel(xs_ref, bc_ref, z_ref, dt_ref, cw_ref, cb_ref, dtb_ref, a_ref, dskip_ref, gn_ref, tril_ref, e3_ref,
                shift_ref, y_ref, hist_ref, state_ref):
    chunk = SSM_CHUNK
    d_inner = z_ref.shape[1]
    n_heads = d_inner // HEAD_DIM
    group_w = d_inner // SSM_GROUPS
    heads_per_group = n_heads // SSM_GROUPS

    @pl.when(pl.program_id(1) == 0)
    def _():
        hist_ref[...] = jnp.zeros(hist_ref.shape, hist_ref.dtype)
        state_ref[...] = jnp.zeros(state_ref.shape, F32)

    parts = []
    for ref, cols in ((xs_ref, slice(0, d_inner)), (bc_ref, slice(d_inner, hist_ref.shape[1]))):
        cur = ref[...]
        ext = jnp.concatenate([hist_ref[:, cols], cur], axis=0)
        shifted = jnp.dot(shift_ref[...], ext, preferred_element_type=F32)
        conv = cb_ref[:, cols] + cw_ref[SSM_CONV - 1:SSM_CONV, cols] * cur.astype(F32)
        for j in range(SSM_CONV - 1):
            conv = conv + cw_ref[j:j + 1, cols] * shifted[j * chunk:(j + 1) * chunk]
        hist_ref[:, cols] = cur[chunk - CONV_HIST:, :]
        parts.append(_silu(conv))
    xs = parts[0]
    xbc_bc = parts[1]

    lane = _iota((1, LANES), 1)
    head_mask = lane < n_heads
    lane_lo = lane < HEAD_DIM
    e3 = e3_ref[...]

    dt = jnp.where(head_mask, _softplus(dt_ref[...] + dtb_ref[...]), 0.0)
    cum = jnp.dot(tril_ref[...], dt * a_ref[...], precision=lax.Precision.HIGHEST,
                  preferred_element_type=F32)
    cum_t = cum.T
    ecum_e = _expand_heads(jnp.exp(cum), head_mask, e3)
    dte_e = _expand_heads(jnp.exp(cum[chunk - 1:chunk, :] - cum), head_mask, e3)
    xdt = xs * _expand_heads(dt, head_mask, e3)
    xend = xdt * dte_e

    causal = _iota((chunk, chunk), 0) >= _iota((chunk, chunk), 1)

    for g in range(SSM_GROUPS):
        gcols = slice(g * group_w, (g + 1) * group_w)
        b_f32 = xbc_bc[:, g * SSM_STATE: (g + 1) * SSM_STATE]
        c_off = SSM_GROUPS * SSM_STATE
        c_bf = xbc_bc[:, c_off + g * SSM_STATE: c_off + (g + 1) * SSM_STATE].astype(BF16)
        cb = lax.dot_general(c_bf, b_f32.astype(BF16), (((1,), (1,)), ((), ())), preferred_element_type=F32)
        st = state_ref[g]
        y_off = jnp.dot(c_bf, st.astype(BF16), preferred_element_type=F32) * ecum_e[:, gcols]
        state_ref[g] = st * ecum_e[chunk - 1:chunk, gcols] + jnp.dot(
            b_f32.T.astype(BF16), xend[:, gcols].astype(BF16), preferred_element_type=F32)

        y_pairs = []
        for pair in range(heads_per_group // 2):
            h0 = g * heads_per_group + 2 * pair
            ms = []
            for h in (h0, h0 + 1):
                diff = cum[:, h:h + 1] - cum_t[h:h + 1, :]
                seg = jnp.exp(jnp.where(causal, diff, -jnp.inf))
                ms.append((cb * seg).astype(BF16))
            xp = xdt[:, h0 * HEAD_DIM: h0 * HEAD_DIM + LANES]
            x_heads = jnp.concatenate([jnp.where(lane_lo, xp, 0.0), jnp.where(lane_lo, 0.0, xp)], axis=0)
            y_pairs.append(jnp.dot(jnp.concatenate(ms, axis=1), x_heads.astype(BF16),
                                   preferred_element_type=F32))
        y = jnp.concatenate(y_pairs, axis=1) + y_off + dskip_ref[:, gcols] * xs[:, gcols]
        gy = y * _silu(z_ref[:, gcols].astype(F32))
        ms_g = jnp.mean(gy * gy, axis=-1, keepdims=True)
        y_ref[:, gcols] = (gy * lax.rsqrt(ms_g + NORM_EPS) * gn_ref[:, gcols]).astype(y_ref.dtype)


def _ssd_mixer(proj, dt_raw, conv_w, conv_b, dt_bias, a_log, d_skip, gate_norm, batch, seq):
    t = proj.shape[0]
    d_inner = gate_norm.shape[0]
    n_heads = d_inner // HEAD_DIM
    conv_dim = conv_w.shape[1]
    bc_w = conv_dim - d_inner
    nc = seq // SSM_CHUNK
    assert d_inner % bc_w == 0

    def pad_heads(v):
        return jnp.zeros((1, LANES), F32).at[0, :n_heads].set(v.astype(F32))

    tril = (_iota((SSM_CHUNK, SSM_CHUNK), 0) >= _iota((SSM_CHUNK, SSM_CHUNK), 1)).astype(F32)
    row = _iota((LANES, d_inner), 0)
    col = _iota((LANES, d_inner), 1)
    e3 = ((row < 3 * n_heads) & ((row % n_heads) == (col // HEAD_DIM))).astype(BF16)
    srow = _iota(((SSM_CONV - 1) * SSM_CHUNK, CONV_HIST + SSM_CHUNK), 0)
    scol = _iota(((SSM_CONV - 1) * SSM_CHUNK, CONV_HIST + SSM_CHUNK), 1)
    shift = (scol == CONV_HIST + srow % SSM_CHUNK - (SSM_CONV - 1 - srow // SSM_CHUNK)).astype(BF16)
    const = lambda shape: pl.BlockSpec(shape, lambda b, c: (0, 0))
    return pl.pallas_call(
        _ssd_kernel,
        grid=(batch, nc),
        in_specs=[
            pl.BlockSpec((SSM_CHUNK, d_inner), lambda b, c: (b * nc + c, 1)),
            pl.BlockSpec((SSM_CHUNK, bc_w), lambda b, c: (b * nc + c, 2 * d_inner // bc_w)),
            pl.BlockSpec((SSM_CHUNK, d_inner), lambda b, c: (b * nc + c, 0)),
            pl.BlockSpec((SSM_CHUNK, LANES), lambda b, c: (b * nc + c, 0)),
            const((SSM_CONV, conv_dim)), const((1, conv_dim)), const((1, LANES)), const((1, LANES)),
            const((1, d_inner)), const((1, d_inner)), const((SSM_CHUNK, SSM_CHUNK)), const((LANES, d_inner)),
            const(shift.shape),
        ],
        out_specs=pl.BlockSpec((SSM_CHUNK, d_inner), lambda b, c: (b * nc + c, 0)),
        out_shape=jax.ShapeDtypeStruct((t, d_inner), BF16),
        scratch_shapes=[pltpu.VMEM((CONV_HIST, conv_dim), BF16),
                        pltpu.VMEM((SSM_GROUPS, SSM_STATE, d_inner // SSM_GROUPS), F32)],
        compiler_params=_compiler_params(("parallel", "arbitrary")),
        name="ssd_mixer",
    )(proj, proj, proj, dt_raw, conv_w.astype(F32), conv_b.reshape(1, conv_dim).astype(F32),
      pad_heads(dt_bias), pad_heads(-jnp.exp(a_log.astype(F32))),
      jnp.repeat(d_skip.astype(F32), HEAD_DIM).reshape(1, d_inner), gate_norm.reshape(1, d_inner).astype(F32),
      tril, e3, shift)


def _outproj_swa_ssd_kernel(x_ref, a_ref, y_ref, wa_ref, wy_ref, o_ref):
    o_ref[...] = (x_ref[...]
                  + jnp.dot(a_ref[...], wa_ref[...], preferred_element_type=F32)
                  + jnp.dot(y_ref[...], wy_ref[...], preferred_element_type=F32))


def _outproj_swa_ssd(x, attn, y, w_attn, w_y, tm, tn):
    t, d = x.shape
    tm = min(tm, t)
    return pl.pallas_call(
        _outproj_swa_ssd_kernel,
        grid=(t // tm, d // tn),
        in_specs=[
            pl.BlockSpec((tm, tn), lambda i, j: (i, j)),
            pl.BlockSpec((tm, attn.shape[1]), lambda i, j: (i, 0)),
            pl.BlockSpec((tm, y.shape[1]), lambda i, j: (i, 0)),
            pl.BlockSpec((attn.shape[1], tn), lambda i, j: (0, j)),
            pl.BlockSpec((y.shape[1], tn), lambda i, j: (0, j)),
        ],
        out_specs=pl.BlockSpec((tm, tn), lambda i, j: (i, j)),
        out_shape=jax.ShapeDtypeStruct((t, d), F32),
        compiler_params=_compiler_params(("parallel", "arbitrary")),
        name="outproj_swa_ssd",
    )(x, attn, y, w_attn, w_y)


def _outproj_dilated_kernel(x_ref, o1_ref, o2_ref, o3_ref, l1_ref, l2_ref, l3_ref, e2_ref, w_ref, out_ref,
                            om_ref, scr_ref):
    @pl.when(pl.program_id(1) == 0)
    def _():
        tm, width = om_ref.shape
        n_res = o1_ref.shape[1]
        lane = _iota((1, LANES), 1)
        head_mask = lane % STAT_GROUP == 0
        stats = [l_ref[...].reshape(tm, LANES) for l_ref in (l1_ref, l2_ref, l3_ref)]
        maxes = [pltpu.roll(st, LANES - STAT_GROUP // 2, axis=1) for st in stats]
        m = jnp.maximum(jnp.maximum(maxes[0], maxes[1]), maxes[2])
        es = [jnp.exp2(mx - m) for mx in maxes]
        den = sum(e * st for e, st in zip(es, stats))
        inv = 1.0 / jnp.where(head_mask, den, 1.0)
        acc = None
        for e, o_ref in zip(es, (o1_ref, o2_ref, o3_ref)):
            w = jnp.where(head_mask, e * inv, 0.0)
            hi = w.astype(BF16).astype(F32)
            lo = (w - hi).astype(BF16).astype(F32)
            cat = (hi + pltpu.roll(lo, 1, axis=1)).astype(BF16)
            term = jnp.dot(cat, e2_ref[...], preferred_element_type=F32) * o_ref[...].reshape(tm, width).astype(F32)
            acc = term if acc is None else acc + term
        rows = tm // n_res
        pitch = scr_ref.shape[1] // n_res
        for c in range(width // LANES):
            for r in range(n_res):
                scr_ref[c, r * pitch:r * pitch + rows, :] = acc[r * rows:(r + 1) * rows, c * LANES:(c + 1) * LANES]
        for n in range(rows):
            om_ref[n * n_res:(n + 1) * n_res, :] = jnp.concatenate(
                [scr_ref[c, pl.ds(n, n_res, stride=pitch), :] for c in range(width // LANES)], axis=1).astype(BF16)

    out_ref[...] = x_ref[...] + jnp.dot(om_ref[...], w_ref[...], preferred_element_type=F32)


def _outproj_dilated(x, outs, lses, w_o, tm, tn):
    t, d = x.shape
    width = w_o.shape[0]
    n_heads = width // HEAD_DIM
    tm = min(tm, t)
    rows = tm // RESIDUES
    per_super = SUPERBLOCK // tm
    row = _iota((LANES, width), 0)
    col = _iota((LANES, width), 1)
    stat_head = 2 * ((row % HEAD_DIM) // STAT_GROUP) + row // HEAD_DIM
    e2 = ((row % STAT_GROUP < 2) & (stat_head == col // HEAD_DIM)).astype(BF16)
    assert n_heads * STAT_GROUP // 2 <= HEAD_DIM
    shp = (t // SUPERBLOCK, RESIDUES, SUPERBLOCK // RESIDUES)
    o_spec = pl.BlockSpec((1, RESIDUES, rows, width), lambda i, j: (i // per_super, 0, i % per_super, 0))
    l_spec = pl.BlockSpec((1, RESIDUES, rows, LANES), lambda i, j: (i // per_super, 0, i % per_super, 0))
    return pl.pallas_call(
        _outproj_dilated_kernel,
        grid=(t // tm, d // tn),
        in_specs=[pl.BlockSpec((tm, tn), lambda i, j: (i, j)), o_spec, o_spec, o_spec, l_spec, l_spec, l_spec,
                  pl.BlockSpec((LANES, width), lambda i, j: (0, 0)),
                  pl.BlockSpec((width, tn), lambda i, j: (0, j),
                               pipeline_mode=pl.Buffered(1) if tn == d else None)],
        out_specs=pl.BlockSpec((tm, tn), lambda i, j: (i, j)),
        out_shape=jax.ShapeDtypeStruct((t, d), F32),
        scratch_shapes=[pltpu.VMEM((tm, width), BF16),
                        pltpu.VMEM((width // LANES, RESIDUES * (rows + STAGE_PITCH - RESIDUES), LANES), F32)],
        compiler_params=_compiler_params(("parallel", "arbitrary")),
        name="outproj_dilated",
    )(x, *[o.reshape(shp + (width,)) for o in outs], *[l.reshape(shp + (LANES,)) for l in lses], e2, w_o)


def _mlp_kernel(x_ref, g_ref, wu_ref, wd_ref, o_ref, h_ref):
    @pl.when(pl.program_id(1) == 0)
    def _():
        h_ref[...] = _rms_norm_rows(x_ref[...], g_ref[...]).astype(BF16)
        o_ref[...] = x_ref[...]

    u = jnp.dot(h_ref[...], wu_ref[...], preferred_element_type=F32)
    a = jnp.square(jnp.maximum(u, 0.0)).astype(BF16)
    o_ref[...] += jnp.dot(a, wd_ref[...], preferred_element_type=F32)


def _mlp(x, gain, w_up, w_down, tm, tf):
    t, d = x.shape
    ff = w_up.shape[1]
    tm = min(tm, t)
    return pl.pallas_call(
        _mlp_kernel,
        grid=(t // tm, ff // tf),
        in_specs=[
            pl.BlockSpec((tm, d), lambda i, j: (i, 0)),
            pl.BlockSpec((1, d), lambda i, j: (0, 0)),
            pl.BlockSpec((d, tf), lambda i, j: (0, j)),
            pl.BlockSpec((tf, d), lambda i, j: (j, 0)),
        ],
        out_specs=pl.BlockSpec((tm, d), lambda i, j: (i, 0)),
        out_shape=jax.ShapeDtypeStruct((t, d), F32),
        scratch_shapes=[pltpu.VMEM((tm, d), BF16)],
        compiler_params=_compiler_params(("parallel", "arbitrary")),
        name="sqrelu_mlp",
    )(x, gain.reshape(1, d).astype(F32), w_up, w_down)


def _swa_ssd_layer(x, gain, w_in, q_norm, k_norm, sinks, conv_w, conv_b, dt_bias, a_log, d_skip, gate_norm,
                   w_out, batch, seq):
    d_inner = gate_norm.shape[0]
    n_ssm_heads = dt_bias.shape[0]
    n_q_heads = sinks.shape[0]
    q_w = n_q_heads * HEAD_DIM
    kv_w = SWA_KV_HEADS * HEAD_DIM
    conv_dim = conv_w.shape[1]
    splits = (q_w, q_w + kv_w, q_w + 2 * kv_w, q_w + 2 * kv_w + d_inner, q_w + 2 * kv_w + d_inner + conv_dim)
    wq, wk, wv, wz, wxbc, wdt = jnp.split(w_in, splits, axis=1)
    w_qkv_dt = jnp.concatenate([wq, wk, wv, jnp.pad(wdt, ((0, 0), (0, LANES - n_ssm_heads)))], axis=1).astype(BF16)
    w_ssm = jnp.concatenate([wz, wxbc], axis=1).astype(BF16)
    qkv_w = q_w + 2 * kv_w
    head_gain = jnp.concatenate([jnp.tile(q_norm.astype(F32) * Q_SCALE, n_q_heads),
                                 jnp.tile(k_norm.astype(F32), SWA_KV_HEADS), jnp.ones((kv_w,), F32)])
    head_mask = jnp.concatenate([jnp.ones((q_w + kv_w,), F32), jnp.zeros((kv_w,), F32)])

    qkv, dt_raw = _norm_matmul_headnorm(x, gain, w_qkv_dt, head_gain, head_mask, tm=1024, tn=qkv_w,
                                        name="norm_qkv_dt_proj", n_extra=LANES)
    proj = _norm_matmul(x, gain, w_ssm, tm=1024, tn=w_ssm.shape[1] // 4, name="norm_ssm_proj")
    attn = _swa_attention(qkv, sinks, batch, seq, q_w)
    y = _ssd_mixer(proj, dt_raw, conv_w, conv_b, dt_bias, a_log, d_skip, gate_norm, batch, seq)
    w_out = w_out.astype(BF16)
    return _outproj_swa_ssd(x, attn, y, w_out[:q_w], w_out[q_w:], tm=1024, tn=1024)


def _dilated_layer(x, gain, w_qkv, q_norm, k_norm, w_o, batch, seq):
    width = w_o.shape[0]
    n_heads = width // HEAD_DIM
    w_qkv = w_qkv.astype(BF16)
    head_gain = jnp.concatenate([jnp.tile(q_norm.astype(F32) * Q_SCALE, n_heads),
                                 jnp.tile(k_norm.astype(F32), n_heads), jnp.ones((width,), F32)])
    head_mask = jnp.concatenate([jnp.ones((2 * width,), F32), jnp.zeros((width,), F32)])
    tn = 1024
    qkv = _norm_matmul_headnorm(x, gain, w_qkv, head_gain, head_mask, tm=1024, tn=tn, name="norm_qkv_proj",
                                interleave=RESIDUES, norm_tiles=2 * width // tn)
    outs, lses = _dilated_branches(qkv, batch, seq)
    return _outproj_dilated(x, outs, lses, w_o.astype(BF16), tm=512, tn=w_o.shape[1])


def kernel(x, norm_mix, norm_ffn, w_up, w_down, ab_w_in, ab_q_norm, ab_k_norm, ab_sinks, ab_conv_w, ab_conv_b,
           ab_dt_bias, ab_a_log, ab_d_skip, ab_gate_norm, ab_w_out, c_w_qkv, c_q_norm, c_k_norm, c_w_o):
    batch, seq, d_model = x.shape
    depth = norm_mix.shape[0]
    assert seq % SUPERBLOCK == 0 and seq % SSM_CHUNK == 0
    h = x.reshape(batch * seq, d_model).astype(F32)
    for layer in range(depth):
        i = layer // 2
        if layer % 2 == 0:
            h = _swa_ssd_layer(h, norm_mix[layer], ab_w_in[i], ab_q_norm[i], ab_k_norm[i], ab_sinks[i],
                               ab_conv_w[i], ab_conv_b[i], ab_dt_bias[i], ab_a_log[i], ab_d_skip[i],
                               ab_gate_norm[i], ab_w_out[i], batch, seq)
        else:
            h = _dilated_layer(h, norm_mix[layer], c_w_qkv[i], c_q_norm[i], c_k_norm[i], c_w_o[i], batch, seq)
        h = _mlp(h, norm_ffn[layer], w_up[layer].astype(BF16), w_down[layer].astype(BF16), tm=512, tf=1024)
    return h.reshape(batch, seq, d_model).astype(x.dtype)
```

```python
import functools
import math

import jax
import jax.numpy as jnp
from jax import lax
from jax.experimental import pallas as pl
from jax.experimental.pallas import tpu as pltpu

F32 = jnp.float32
BF16 = jnp.bfloat16

NORM_EPS = 1e-5
HEAD_DIM = 64
ATTN_BLOCK = 128
SWA_KV_HEADS = 2
SWA_WINDOW = 128
SSM_GROUPS = 4
SSM_STATE = 128
SSM_CONV = 4
SSM_CHUNK = 128
DIL_BRANCHES = ((128, 1), (512, 4), (2048, 16))
RESIDUES = 16
SUPERBLOCK = RESIDUES * ATTN_BLOCK
STAGE_PITCH = 20

LOG2E = math.log2(math.e)
Q_SCALE = HEAD_DIM ** -0.5 * LOG2E

LANES = 128
MXU_WIDTH = 256
CONV_HIST = 16
VMEM_LIMIT_BYTES = 56 * 1024 * 1024


def _compiler_params(semantics):
    return pltpu.CompilerParams(dimension_semantics=semantics, vmem_limit_bytes=VMEM_LIMIT_BYTES)


def _rms_norm_rows(x, gain):
    ms = jnp.mean(x * x, axis=-1, keepdims=True)
    return x * lax.rsqrt(ms + NORM_EPS) * gain


def _norm_rows_into(h_ref, x_ref, g_ref, interleave, xs_ref):
    if interleave is None:
        h_ref[...] = _rms_norm_rows(x_ref[...], g_ref[...]).astype(BF16)
        return
    n_chunks = xs_ref.shape[0]
    rows = x_ref.shape[0] // interleave
    pitch = xs_ref.shape[1] // rows
    for c in range(n_chunks):
        for k in range(rows):
            xs_ref[c, k * pitch:k * pitch + interleave, :] = x_ref[k * interleave:(k + 1) * interleave,
                                                                   c * LANES:(c + 1) * LANES]
    for r in range(interleave):
        xr = jnp.concatenate([xs_ref[c, pl.ds(r, rows, stride=pitch), :] for c in range(n_chunks)], axis=1)
        h_ref[r * rows:(r + 1) * rows, :] = _rms_norm_rows(xr, g_ref[...]).astype(BF16)


def _norm_matmul_kernel(x_ref, g_ref, w_ref, o_ref, h_ref, xs_ref=None, *, interleave):
    @pl.when(pl.program_id(1) == 0)
    def _():
        _norm_rows_into(h_ref, x_ref, g_ref, interleave, xs_ref)

    acc = jnp.dot(h_ref[...], w_ref[...], preferred_element_type=F32)
    o_ref[...] = acc.astype(o_ref.dtype).reshape(o_ref.shape)


def _norm_matmul_headnorm_kernel(x_ref, g_ref, w_ref, hg_ref, hm_ref, ones_ref, *rest, interleave, n_extra,
                                 norm_tiles):
    rest = list(rest)
    o_ref = rest.pop(0)
    extra_ref = rest.pop(0) if n_extra else None
    h_ref = rest.pop(0)
    xs_ref = rest.pop(0) if interleave is not None else None
    j = pl.program_id(1)

    @pl.when(j == 0)
    def _():
        _norm_rows_into(h_ref, x_ref, g_ref, interleave, xs_ref)

    n_main = w_ref.shape[1] - n_extra

    def head_norm_store():
        acc = jnp.dot(h_ref[...], w_ref[...], preferred_element_type=F32)
        if n_extra:
            extra_ref[...] = acc[:, n_main:]
        chunks = []
        for c in range(n_main // MXU_WIDTH):
            cols = slice(c * MXU_WIDTH, (c + 1) * MXU_WIDTH)
            a = acc[:, cols]
            ms = jnp.dot((a * a).astype(BF16), ones_ref[...], preferred_element_type=F32)
            r = lax.rsqrt(ms + NORM_EPS)
            if norm_tiles is None:
                r = jnp.where(hm_ref[:, cols] > 0.0, r, 1.0)
            chunks.append((a * r * hg_ref[:, cols]).astype(o_ref.dtype))
        o_ref[...] = jnp.concatenate(chunks, axis=1).reshape(o_ref.shape)

    if norm_tiles is None:
        head_norm_store()
    else:
        pl.when(j < norm_tiles)(head_norm_store)

        @pl.when(j >= norm_tiles)
        def _():
            acc = jnp.dot(h_ref[...], w_ref[...], preferred_element_type=F32)
            o_ref[...] = acc.astype(o_ref.dtype).reshape(o_ref.shape)


def _proj_scratch(tm, d, interleave):
    scratch = [pltpu.VMEM((tm, d), BF16)]
    if interleave is not None:
        scratch.append(pltpu.VMEM((d // LANES, tm // interleave * STAGE_PITCH, LANES), F32))
    return scratch


def _row_specs(t, tm, tn, interleave):
    if interleave is None:
        return (lambda n: jax.ShapeDtypeStruct((t, n), BF16)), pl.BlockSpec((tm, tn), lambda i, j: (i, j))
    per_super = SUPERBLOCK // tm
    rows = tm // interleave
    shape = lambda n: jax.ShapeDtypeStruct((t // SUPERBLOCK, interleave, SUPERBLOCK // interleave, n), BF16)
    spec = pl.BlockSpec((1, interleave, rows, tn), lambda i, j: (i // per_super, 0, i % per_super, j))
    return shape, spec


def _norm_matmul(x, gain, w, tm, tn, name, interleave=None):
    t, d = x.shape
    n = w.shape[1]
    tm = min(tm, t)
    shape, out_spec = _row_specs(t, tm, tn, interleave)
    out = pl.pallas_call(
        functools.partial(_norm_matmul_kernel, interleave=interleave),
        grid=(t // tm, n // tn),
        in_specs=[
            pl.BlockSpec((tm, d), lambda i, j: (i, 0)),
            pl.BlockSpec((1, d), lambda i, j: (0, 0)),
            pl.BlockSpec((d, tn), lambda i, j: (0, j)),
        ],
        out_specs=out_spec,
        out_shape=shape(n),
        scratch_shapes=_proj_scratch(tm, d, interleave),
        compiler_params=_compiler_params(("parallel", "arbitrary")),
        name=name,
    )(x, gain.reshape(1, d).astype(F32), w)
    return out.reshape(t, n)


def _head_ones():
    r = lax.broadcasted_iota(jnp.int32, (MXU_WIDTH, MXU_WIDTH), 0)
    c = lax.broadcasted_iota(jnp.int32, (MXU_WIDTH, MXU_WIDTH), 1)
    return jnp.where((r // HEAD_DIM) == (c // HEAD_DIM), 1.0 / HEAD_DIM, 0.0).astype(BF16)


def _norm_matmul_headnorm(x, gain, w, head_gain, head_mask, tm, tn, name, interleave=None, n_extra=0,
                          norm_tiles=None):
    t, d = x.shape
    n_main = w.shape[1] - n_extra
    tm = min(tm, t)
    assert n_extra == 0 or tn == n_main
    shape, out_spec = _row_specs(t, tm, tn, interleave)
    out_shapes = [shape(n_main)]
    out_specs = [out_spec]
    if n_extra:
        out_shapes.append(jax.ShapeDtypeStruct((t, n_extra), F32))
        out_specs.append(pl.BlockSpec((tm, n_extra), lambda i, j: (i, 0)))
    outs = pl.pallas_call(
        functools.partial(_norm_matmul_headnorm_kernel, interleave=interleave, n_extra=n_extra,
                          norm_tiles=norm_tiles),
        grid=(t // tm, n_main // tn),
        in_specs=[
            pl.BlockSpec((tm, d), lambda i, j: (i, 0)),
            pl.BlockSpec((1, d), lambda i, j: (0, 0)),
            pl.BlockSpec((d, tn + n_extra), lambda i, j: (0, j)),
            pl.BlockSpec((1, tn), lambda i, j: (0, j)),
            pl.BlockSpec((1, tn), lambda i, j: (0, j)),
            pl.BlockSpec((MXU_WIDTH, MXU_WIDTH), lambda i, j: (0, 0)),
        ],
        out_specs=out_specs,
        out_shape=out_shapes,
        scratch_shapes=_proj_scratch(tm, d, interleave),
        compiler_params=_compiler_params(("parallel", "arbitrary")),
        name=name,
    )(x, gain.reshape(1, d).astype(F32), w, head_gain.reshape(1, n_main).astype(F32),
      head_mask.reshape(1, n_main).astype(F32), _head_ones())
    if n_extra:
        return outs[0].reshape(t, n_main), outs[1]
    return outs[0].reshape(t, n_main)


def _lane_lo():
    return lax.broadcasted_iota(jnp.int32, (1, LANES), 1) < HEAD_DIM


def _attn_tiles(qs, ks, vs, masks, lane_lo, sinks=None, normalize=False):
    sel_rows = _iota((4 * ATTN_BLOCK, LANES), 0) < 2 * ATTN_BLOCK
    row_sum = (sel_rows == (_iota((4 * ATTN_BLOCK, LANES), 1) < HEAD_DIM)).astype(BF16)
    results = []
    for i, (q, k, v, mask) in enumerate(zip(qs, ks, vs, masks)):
        zq = jnp.zeros_like(q)
        q_heads = jnp.concatenate([jnp.where(lane_lo, q, zq), jnp.where(lane_lo, zq, q)], axis=0)
        s2 = lax.dot_general(q_heads, k, (((1,), (1,)), ((), ())), preferred_element_type=F32)
        ps, ms = [], []
        for idx in range(2):
            s = jnp.where(mask, s2[idx * ATTN_BLOCK:(idx + 1) * ATTN_BLOCK], -jnp.inf)
            m = jnp.max(s, axis=-1, keepdims=True)
            if sinks is not None:
                m = jnp.maximum(m, sinks[i][idx])
            ps.append(jnp.exp2(s - m).astype(BF16))
            ms.append(m)
        zv = jnp.zeros_like(v)
        v_heads = jnp.concatenate([jnp.where(lane_lo, v, zv), jnp.where(lane_lo, zv, v)], axis=0)
        res = jnp.dot(jnp.concatenate(ps, axis=1), jnp.concatenate([v_heads, row_sum], axis=1),
                      preferred_element_type=F32)
        o, l = res[:, :LANES], res[:, LANES:]
        if sinks is not None:
            l = l + jnp.where(lane_lo, jnp.exp2(sinks[i][0] - ms[0]), jnp.exp2(sinks[i][1] - ms[1]))
        if normalize:
            o = o * (1.0 / l)
        results.append((o, ms, l))
    return results


def _dist_mask(q_pos, k_pos, max_dist, key_exists=None):
    dist = q_pos - k_pos
    mask = (dist >= 0) & (dist <= max_dist)
    if key_exists is not None:
        mask = mask & key_exists
    return mask


def _iota(shape, dim):
    return lax.broadcasted_iota(jnp.int32, shape, dim)


SWA_BLOCKS_PER_STEP = 8


def _swa_kernel(sinks_ref, q_ref, kvp_ref, kvc_ref, o_ref):
    first = pl.program_id(1) == 0
    lane_lo = _lane_lo()
    kj = _iota((1, 2 * ATTN_BLOCK), 1)
    mask = _dist_mask(_iota((ATTN_BLOCK, 1), 0) + ATTN_BLOCK, kj, SWA_WINDOW - 1)
    mask_first = mask & ((kj >= ATTN_BLOCK) | jnp.logical_not(first))
    kv = jnp.concatenate([kvp_ref[...], kvc_ref[...]], axis=0).astype(F32)

    def dup_heads(x):
        xr = pltpu.roll(x, HEAD_DIM, axis=1)
        return jnp.where(lane_lo, x, xr).astype(BF16), jnp.where(lane_lo, xr, x).astype(BF16)

    k_dup = dup_heads(kv[:, :LANES])
    v_dup = dup_heads(kv[:, LANES:])
    n_blocks = q_ref.shape[0] // ATTN_BLOCK
    n_pairs = q_ref.shape[1] // LANES
    pairs_per_kv = n_pairs // SWA_KV_HEADS
    qs, ks, vs, masks, sinks = [], [], [], [], []
    for blk in range(n_blocks):
        rows = slice(blk * ATTN_BLOCK, (blk + 1) * ATTN_BLOCK)
        keys = slice(blk * ATTN_BLOCK, (blk + 2) * ATTN_BLOCK)
        for j in range(n_pairs):
            kh = j // pairs_per_kv
            qs.append(q_ref[rows, j * LANES:(j + 1) * LANES])
            ks.append(k_dup[kh][keys])
            vs.append(v_dup[kh][keys])
            masks.append(mask_first if blk == 0 else mask)
            sinks.append((sinks_ref[2 * j], sinks_ref[2 * j + 1]))
    for idx, (o, _, _) in enumerate(_attn_tiles(qs, ks, vs, masks, lane_lo, sinks, normalize=True)):
        blk, j = divmod(idx, n_pairs)
        o_ref[blk * ATTN_BLOCK:(blk + 1) * ATTN_BLOCK, j * LANES:(j + 1) * LANES] = o.astype(o_ref.dtype)


def _swa_attention(qkv, sinks, batch, seq, q_width):
    t = qkv.shape[0]
    rows = SWA_BLOCKS_PER_STEP * ATTN_BLOCK
    nb = seq // rows
    kv_block = q_width // (2 * LANES)
    prev_block = lambda b, n: jnp.maximum((b * nb + n) * SWA_BLOCKS_PER_STEP - 1, 0)
    return pl.pallas_call(
        _swa_kernel,
        grid=(batch, nb),
        in_specs=[
            pl.BlockSpec(memory_space=pltpu.SMEM),
            pl.BlockSpec((rows, q_width), lambda b, n: (b * nb + n, 0)),
            pl.BlockSpec((ATTN_BLOCK, 2 * LANES), lambda b, n: (prev_block(b, n), kv_block)),
            pl.BlockSpec((rows, 2 * LANES), lambda b, n: (b * nb + n, kv_block)),
        ],
        out_specs=pl.BlockSpec((rows, q_width), lambda b, n: (b * nb + n, 0)),
        out_shape=jax.ShapeDtypeStruct((t, q_width), BF16),
        compiler_params=_compiler_params(("parallel", "arbitrary")),
        name="swa_attention",
    )(sinks.astype(F32) * LOG2E, qkv, qkv, qkv)


STAT_GROUP = 4


def _write_stats(acc, lane, pair, ms, l):
    lane_lo = lane < HEAD_DIM
    stat = jnp.where(lane % STAT_GROUP < STAT_GROUP // 2, l, jnp.where(lane_lo, ms[0], ms[1]))
    return jnp.where((lane % HEAD_DIM) // STAT_GROUP == pair, stat, acc)


def _dil16_kernel(q_ref, kp_ref, kc_ref, vp_ref, vc_ref, o_ref, lse_ref):
    first = pl.program_id(2) == 0
    n_res = q_ref.shape[1]
    n_pairs = q_ref.shape[3] // LANES
    lane = _iota((1, LANES), 1)
    lane_lo = lane < HEAD_DIM
    kj = _iota((1, 2 * ATTN_BLOCK), 1)
    mask = _dist_mask(_iota((ATTN_BLOCK, 1), 0) + ATTN_BLOCK, kj, ATTN_BLOCK,
                      (kj >= ATTN_BLOCK) | jnp.logical_not(first))

    def body(j, lse_accs):
        cols = pl.ds(pl.multiple_of(j * LANES, LANES), LANES)
        qs = [q_ref[0, r, :, cols] for r in range(n_res)]
        ks = [jnp.concatenate([kp_ref[0, r, :, cols], kc_ref[0, r, :, cols]], axis=0) for r in range(n_res)]
        vs = [jnp.concatenate([vp_ref[0, r, :, cols], vc_ref[0, r, :, cols]], axis=0) for r in range(n_res)]
        out = []
        for r, (o, ms, ls) in enumerate(_attn_tiles(qs, ks, vs, [mask] * n_res, lane_lo)):
            o_ref[0, r, :, cols] = o.astype(o_ref.dtype)
            out.append(_write_stats(lse_accs[r], lane, j, ms, ls))
        return tuple(out)

    zeros = tuple(jnp.zeros((ATTN_BLOCK, LANES), F32) for _ in range(n_res))
    for r, acc in enumerate(lax.fori_loop(0, n_pairs, body, zeros, unroll=True)):
        lse_ref[0, r] = acc


def _dil4_kernel(q_ref, kp_ref, kc_ref, vp_ref, vc_ref, o_ref, lse_ref):
    first = pl.program_id(2) == 0
    n_sub = q_ref.shape[1]
    rows = ATTN_BLOCK // n_sub
    n_tiles = q_ref.shape[3] // rows
    n_pairs = q_ref.shape[4] // LANES
    lane = _iota((1, LANES), 1)
    lane_lo = lane < HEAD_DIM
    qi = _iota((ATTN_BLOCK, 1), 0)
    kj = _iota((1, 2 * ATTN_BLOCK), 1)
    q_pos = n_sub * (rows + qi % rows) + qi // rows
    k_pos = n_sub * (kj % (2 * rows)) + kj // (2 * rows)
    mask = _dist_mask(q_pos, k_pos, ATTN_BLOCK)
    mask_first = mask & ((kj % (2 * rows) >= rows) | jnp.logical_not(first))

    def gather(cur_ref, prev_ref, tile, cols):
        if tile == 0:
            parts = [jnp.concatenate([prev_ref[0, a, 0, :, cols], cur_ref[0, a, 0, 0:rows, cols]], axis=0)
                     for a in range(n_sub)]
        else:
            parts = [cur_ref[0, a, 0, (tile - 1) * rows:(tile + 1) * rows, cols] for a in range(n_sub)]
        return jnp.concatenate(parts, axis=0)

    def body(j, lse_accs):
        cols = pl.ds(pl.multiple_of(j * LANES, LANES), LANES)
        qs = [jnp.concatenate([q_ref[0, a, 0, t * rows:(t + 1) * rows, cols] for a in range(n_sub)], axis=0)
              for t in range(n_tiles)]
        ks = [gather(kc_ref, kp_ref, t, cols) for t in range(n_tiles)]
        vs = [gather(vc_ref, vp_ref, t, cols) for t in range(n_tiles)]
        masks = [mask_first] + [mask] * (n_tiles - 1)
        out = []
        for t, (o, ms, ls) in enumerate(_attn_tiles(qs, ks, vs, masks, lane_lo)):
            ob = o.astype(o_ref.dtype)
            for a in range(n_sub):
                o_ref[0, a, 0, t * rows:(t + 1) * rows, cols] = ob[a * rows:(a + 1) * rows, :]
            out.append(_write_stats(lse_accs[t], lane, j, ms, ls))
        return tuple(out)

    zeros = tuple(jnp.zeros((ATTN_BLOCK, LANES), F32) for _ in range(n_tiles))
    for t, acc in enumerate(lax.fori_loop(0, n_pairs, body, zeros, unroll=True)):
        for a in range(n_sub):
            lse_ref[0, a, 0, t * rows:(t + 1) * rows, :] = acc[a * rows:(a + 1) * rows, :]


def _dil1_kernel(q_ref, kp_ref, kc_ref, vp_ref, vc_ref, o_ref, lse_ref):
    first = (pl.program_id(1) == 0) & (pl.program_id(2) == 0)
    n_res = q_ref.shape[1]
    sub = ATTN_BLOCK // n_res
    n_tiles = q_ref.shape[2] // sub
    n_pairs = q_ref.shape[3] // LANES
    prev_rows = kp_ref.shape[2]
    lane = _iota((1, LANES), 1)
    lane_lo = lane < HEAD_DIM
    qi = _iota((ATTN_BLOCK, 1), 0)
    kj = _iota((1, 2 * ATTN_BLOCK), 1)
    q_pos = n_res * (sub + qi % sub) + qi // sub
    k_pos = n_res * (kj % (2 * sub)) + kj // (2 * sub)
    mask = _dist_mask(q_pos, k_pos, ATTN_BLOCK)
    mask_first = mask & ((kj % (2 * sub) >= sub) | jnp.logical_not(first))

    def tile_rows(x, lo, hi):
        return x[:, lo:hi, :].reshape(n_res * (hi - lo), LANES)

    def body(j, lse_accs):
        cols = pl.ds(pl.multiple_of(j * LANES, LANES), LANES)
        q = q_ref[0, :, :, cols].astype(F32)
        qs = [tile_rows(q, t * sub, (t + 1) * sub).astype(BF16) for t in range(n_tiles)]

        def gather(cur_ref, prev_ref):
            cur = cur_ref[0, :, :, cols].astype(F32)
            prev = prev_ref[0, :, :, cols].astype(F32)
            tiles = []
            for t in range(n_tiles):
                if t == 0:
                    x = jnp.concatenate([prev[:, prev_rows - sub:, :], cur[:, 0:sub, :]], axis=1)
                else:
                    x = cur[:, (t - 1) * sub:(t + 1) * sub, :]
                tiles.append(x.reshape(n_res * 2 * sub, LANES).astype(BF16))
            return tiles

        ks = gather(kc_ref, kp_ref)
        vs = gather(vc_ref, vp_ref)
        masks = [mask_first] + [mask] * (n_tiles - 1)
        res = _attn_tiles(qs, ks, vs, masks, lane_lo)
        o_full = jnp.concatenate([o.reshape(n_res, sub, LANES) for o, _, _ in res], axis=1)
        o_ref[0, :, :, cols] = o_full.astype(o_ref.dtype)
        return tuple(_write_stats(lse_accs[t], lane, j, ms, ls) for t, (_, ms, ls) in enumerate(res))

    zeros = tuple(jnp.zeros((ATTN_BLOCK, LANES), F32) for _ in range(n_tiles))
    accs = lax.fori_loop(0, n_pairs, body, zeros, unroll=True)
    lse_ref[0] = jnp.concatenate([acc.reshape(n_res, sub, LANES) for acc in accs], axis=1)


def _dilated_branches(qkv, batch, seq):
    t, w = qkv.shape[0], qkv.shape[1] // 3
    nsb = seq // SUPERBLOCK
    n_super = t // SUPERBLOCK
    per_res = SUPERBLOCK // RESIDUES
    outs, lses = [], []

    def in_specs(block, cur_map, prev_block, prev_map):
        cur = lambda col: pl.BlockSpec(block, lambda *g: cur_map(*g) + (col,))
        prev = lambda col: pl.BlockSpec(prev_block, lambda *g: prev_map(*g) + (col,))
        return [cur(0), prev(1), cur(1), prev(2), cur(2)]

    r16 = 4
    shp = (n_super, RESIDUES, per_res)
    cur_map = lambda b, r, s: (b * nsb + s, r, 0)
    prev_map = lambda b, r, s: (b * nsb + jnp.maximum(s - 1, 0), r, 0)
    qkv4 = qkv.reshape(shp + (3 * w,))
    o16, l16 = pl.pallas_call(
        _dil16_kernel, grid=(batch, RESIDUES // r16, nsb),
        in_specs=in_specs((1, r16, per_res, w), cur_map, (1, r16, per_res, w), prev_map),
        out_specs=[pl.BlockSpec((1, r16, per_res, w), lambda *g: cur_map(*g) + (0,)),
                   pl.BlockSpec((1, r16, per_res, LANES), lambda *g: cur_map(*g) + (0,))],
        out_shape=[jax.ShapeDtypeStruct(shp + (w,), BF16), jax.ShapeDtypeStruct(shp + (LANES,), F32)],
        compiler_params=_compiler_params(("parallel", "parallel", "arbitrary")),
        name="dilated_attention_d16",
    )(qkv4, qkv4, qkv4, qkv4, qkv4)

    shp5 = (n_super, 4, 4, per_res)
    tail = per_res // 4
    cur_map5 = lambda b, r, s: (b * nsb + s, 0, r, 0)
    prev_map5 = lambda b, r, s: (b * nsb + jnp.maximum(s - 1, 0), 0, r, per_res // tail - 1)
    qkv5 = qkv.reshape(shp5 + (3 * w,))
    o4, l4 = pl.pallas_call(
        _dil4_kernel, grid=(batch, 4, nsb),
        in_specs=in_specs((1, 4, 1, per_res, w), cur_map5, (1, 4, 1, tail, w), prev_map5),
        out_specs=[pl.BlockSpec((1, 4, 1, per_res, w), lambda *g: cur_map5(*g) + (0,)),
                   pl.BlockSpec((1, 4, 1, per_res, LANES), lambda *g: cur_map5(*g) + (0,))],
        out_shape=[jax.ShapeDtypeStruct(shp5 + (w,), BF16), jax.ShapeDtypeStruct(shp5 + (LANES,), F32)],
        compiler_params=_compiler_params(("parallel", "parallel", "arbitrary")),
        name="dilated_attention_d4",
    )(qkv5, qkv5, qkv5, qkv5, qkv5)

    run, hist = 32, 16
    runs = per_res // run
    cur_map1 = lambda b, s, c: (b * nsb + s, 0, c)

    def prev_map1(b, s, c):
        g = (b * nsb + s) * (per_res // hist) + c * (run // hist) - 1
        g = jnp.maximum(g, 0)
        return (g // (per_res // hist), 0, g % (per_res // hist))

    o1, l1 = pl.pallas_call(
        _dil1_kernel, grid=(batch, nsb, runs),
        in_specs=in_specs((1, RESIDUES, run, w), cur_map1, (1, RESIDUES, hist, w), prev_map1),
        out_specs=[pl.BlockSpec((1, RESIDUES, run, w), lambda *g: cur_map1(*g) + (0,)),
                   pl.BlockSpec((1, RESIDUES, run, LANES), lambda *g: cur_map1(*g) + (0,))],
        out_shape=[jax.ShapeDtypeStruct(shp + (w,), BF16), jax.ShapeDtypeStruct(shp + (LANES,), F32)],
        compiler_params=_compiler_params(("parallel", "arbitrary", "arbitrary")),
        name="dilated_attention_d1",
    )(qkv4, qkv4, qkv4, qkv4, qkv4)

    for o, l in ((o1, l1), (o4, l4), (o16, l16)):
        outs.append(o.reshape(t, w))
        lses.append(l.reshape(t, LANES))
    return outs, lses


def _softplus(x):
    return jnp.maximum(x, 0.0) + jnp.log1p(jnp.exp(-jnp.abs(x)))


def _silu(x):
    h = 0.5 * x
    return h * jnp.tanh(h) + h


def _expand_heads(v, head_mask, e3):
    n_heads = e3.shape[1] // HEAD_DIM
    v = jnp.where(head_mask, v, 0.0)
    hi = v.astype(BF16).astype(F32)
    r = v - hi
    mid = r.astype(BF16).astype(F32)
    lo = (r - mid).astype(BF16).astype(F32)
    cat = hi + pltpu.roll(mid, n_heads, axis=1) + pltpu.roll(lo, 2 * n_heads, axis=1)
    return jnp.dot(cat.astype(BF16), e3, preferred_element_type=F32)


def _ssd_kernel(xs_ref, bc_ref, z_ref, dt_ref, cw_ref, cb_ref, dtb_ref, a_ref, dskip_ref, gn_ref, tril_ref, e3_ref,
                shift_ref, y_ref, hist_ref, state_ref):
    chunk = SSM_CHUNK
    d_inner = z_ref.shape[1]
    n_heads = d_inner // HEAD_DIM
    group_w = d_inner // SSM_GROUPS
    heads_per_group = n_heads // SSM_GROUPS

    @pl.when(pl.program_id(1) == 0)
    def _():
        hist_ref[...] = jnp.zeros(hist_ref.shape, hist_ref.dtype)
        state_ref[...] = jnp.zeros(state_ref.shape, F32)

    parts = []
    for ref, cols in ((xs_ref, slice(0, d_inner)), (bc_ref, slice(d_inner, hist_ref.shape[1]))):
        cur = ref[...]
        ext = jnp.concatenate([hist_ref[:, cols], cur], axis=0)
        shifted = jnp.dot(shift_ref[...], ext, preferred_element_type=F32)
        conv = cb_ref[:, cols] + cw_ref[SSM_CONV - 1:SSM_CONV, cols] * cur.astype(F32)
        for j in range(SSM_CONV - 1):
            conv = conv + cw_ref[j:j + 1, cols] * shifted[j * chunk:(j + 1) * chunk]
        hist_ref[:, cols] = cur[chunk - CONV_HIST:, :]
        parts.append(_silu(conv))
    xs = parts[0]
    xbc_bc = parts[1]

    lane = _iota((1, LANES), 1)
    head_mask = lane < n_heads
    lane_lo = lane < HEAD_DIM
    e3 = e3_ref[...]

    dt = jnp.where(head_mask, _softplus(dt_ref[...] + dtb_ref[...]), 0.0)
    cum = jnp.dot(tril_ref[...], dt * a_ref[...], precision=lax.Precision.HIGHEST,
                  preferred_element_type=F32)
    cum_t = cum.T
    ecum_e = _expand_heads(jnp.exp(cum), head_mask, e3)
    dte_e = _expand_heads(jnp.exp(cum[chunk - 1:chunk, :] - cum), head_mask, e3)
    xdt = xs * _expand_heads(dt, head_mask, e3)
    xend = xdt * dte_e

    causal = _iota((chunk, chunk), 0) >= _iota((chunk, chunk), 1)

    for g in range(SSM_GROUPS):
        gcols = slice(g * group_w, (g + 1) * group_w)
        b_f32 = xbc_bc[:, g * SSM_STATE: (g + 1) * SSM_STATE]
        c_off = SSM_GROUPS * SSM_STATE
        c_bf = xbc_bc[:, c_off + g * SSM_STATE: c_off + (g + 1) * SSM_STATE].astype(BF16)
        cb = lax.dot_general(c_bf, b_f32.astype(BF16), (((1,), (1,)), ((), ())), preferred_element_type=F32)
        st = state_ref[g]
        y_off = jnp.dot(c_bf, st.astype(BF16), preferred_element_type=F32) * ecum_e[:, gcols]
        state_ref[g] = st * ecum_e[chunk - 1:chunk, gcols] + jnp.dot(
            b_f32.T.astype(BF16), xend[:, gcols].astype(BF16), preferred_element_type=F32)

        y_pairs = []
        for pair in range(heads_per_group // 2):
            h0 = g * heads_per_group + 2 * pair
            ms = []
            for h in (h0, h0 + 1):
                diff = cum[:, h:h + 1] - cum_t[h:h + 1, :]
                seg = jnp.exp(jnp.where(causal, diff, -jnp.inf))
                ms.append((cb * seg).astype(BF16))
            xp = xdt[:, h0 * HEAD_DIM: h0 * HEAD_DIM + LANES]
            x_heads = jnp.concatenate([jnp.where(lane_lo, xp, 0.0), jnp.where(lane_lo, 0.0, xp)], axis=0)
            y_pairs.append(jnp.dot(jnp.concatenate(ms, axis=1), x_heads.astype(BF16),
                                   preferred_element_type=F32))
        y = jnp.concatenate(y_pairs, axis=1) + y_off + dskip_ref[:, gcols] * xs[:, gcols]
        gy = y * _silu(z_ref[:, gcols].astype(F32))
        ms_g = jnp.mean(gy * gy, axis=-1, keepdims=True)
        y_ref[:, gcols] = (gy * lax.rsqrt(ms_g + NORM_EPS) * gn_ref[:, gcols]).astype(y_ref.dtype)


def _ssd_mixer(proj, dt_raw, conv_w, conv_b, dt_bias, a_log, d_skip, gate_norm, batch, seq):
    t = proj.shape[0]
    d_inner = gate_norm.shape[0]
    n_heads = d_inner // HEAD_DIM
    conv_dim = conv_w.shape[1]
    bc_w = conv_dim - d_inner
    nc = seq // SSM_CHUNK
    assert d_inner % bc_w == 0

    def pad_heads(v):
        return jnp.zeros((1, LANES), F32).at[0, :n_heads].set(v.astype(F32))

    tril = (_iota((SSM_CHUNK, SSM_CHUNK), 0) >= _iota((SSM_CHUNK, SSM_CHUNK), 1)).astype(F32)
    row = _iota((LANES, d_inner), 0)
    col = _iota((LANES, d_inner), 1)
    e3 = ((row < 3 * n_heads) & ((row % n_heads) == (col // HEAD_DIM))).astype(BF16)
    srow = _iota(((SSM_CONV - 1) * SSM_CHUNK, CONV_HIST + SSM_CHUNK), 0)
    scol = _iota(((SSM_CONV - 1) * SSM_CHUNK, CONV_HIST + SSM_CHUNK), 1)
    shift = (scol == CONV_HIST + srow % SSM_CHUNK - (SSM_CONV - 1 - srow // SSM_CHUNK)).astype(BF16)
    const = lambda shape: pl.BlockSpec(shape, lambda b, c: (0, 0))
    return pl.pallas_call(
        _ssd_kernel,
        grid=(batch, nc),
        in_specs=[
            pl.BlockSpec((SSM_CHUNK, d_inner), lambda b, c: (b * nc + c, 1)),
            pl.BlockSpec((SSM_CHUNK, bc_w), lambda b, c: (b * nc + c, 2 * d_inner // bc_w)),
            pl.BlockSpec((SSM_CHUNK, d_inner), lambda b, c: (b * nc + c, 0)),
            pl.BlockSpec((SSM_CHUNK, LANES), lambda b, c: (b * nc + c, 0)),
            const((SSM_CONV, conv_dim)), const((1, conv_dim)), const((1, LANES)), const((1, LANES)),
            const((1, d_inner)), const((1, d_inner)), const((SSM_CHUNK, SSM_CHUNK)), const((LANES, d_inner)),
            const(shift.shape),
        ],
        out_specs=pl.BlockSpec((SSM_CHUNK, d_inner), lambda b, c: (b * nc + c, 0)),
        out_shape=jax.ShapeDtypeStruct((t, d_inner), BF16),
        scratch_shapes=[pltpu.VMEM((CONV_HIST, conv_dim), BF16),
                        pltpu.VMEM((SSM_GROUPS, SSM_STATE, d_inner // SSM_GROUPS), F32)],
        compiler_params=_compiler_params(("parallel", "arbitrary")),
        name="ssd_mixer",
    )(proj, proj, proj, dt_raw, conv_w.astype(F32), conv_b.reshape(1, conv_dim).astype(F32),
      pad_heads(dt_bias), pad_heads(-jnp.exp(a_log.astype(F32))),
      jnp.repeat(d_skip.astype(F32), HEAD_DIM).reshape(1, d_inner), gate_norm.reshape(1, d_inner).astype(F32),
      tril, e3, shift)


def _outproj_swa_ssd_kernel(x_ref, a_ref, y_ref, wa_ref, wy_ref, o_ref):
    o_ref[...] = (x_ref[...]
                  + jnp.dot(a_ref[...], wa_ref[...], preferred_element_type=F32)
                  + jnp.dot(y_ref[...], wy_ref[...], preferred_element_type=F32))


def _outproj_swa_ssd(x, attn, y, w_attn, w_y, tm, tn):
    t, d = x.shape
    tm = min(tm, t)
    return pl.pallas_call(
        _outproj_swa_ssd_kernel,
        grid=(t // tm, d // tn),
        in_specs=[
            pl.BlockSpec((tm, tn), lambda i, j: (i, j)),
            pl.BlockSpec((tm, attn.shape[1]), lambda i, j: (i, 0)),
            pl.BlockSpec((tm, y.shape[1]), lambda i, j: (i, 0)),
            pl.BlockSpec((attn.shape[1], tn), lambda i, j: (0, j)),
            pl.BlockSpec((y.shape[1], tn), lambda i, j: (0, j)),
        ],
        out_specs=pl.BlockSpec((tm, tn), lambda i, j: (i, j)),
        out_shape=jax.ShapeDtypeStruct((t, d), F32),
        compiler_params=_compiler_params(("parallel", "arbitrary")),
        name="outproj_swa_ssd",
    )(x, attn, y, w_attn, w_y)


def _outproj_dilated_kernel(x_ref, o1_ref, o2_ref, o3_ref, l1_ref, l2_ref, l3_ref, e2_ref, w_ref, out_ref,
                            om_ref, scr_ref):
    @pl.when(pl.program_id(1) == 0)
    def _():
        tm, width = om_ref.shape
        n_res = o1_ref.shape[1]
        lane = _iota((1, LANES), 1)
        head_mask = lane % STAT_GROUP == 0
        stats = [l_ref[...].reshape(tm, LANES) for l_ref in (l1_ref, l2_ref, l3_ref)]
        maxes = [pltpu.roll(st, LANES - STAT_GROUP // 2, axis=1) for st in stats]
        m = jnp.maximum(jnp.maximum(maxes[0], maxes[1]), maxes[2])
        es = [jnp.exp2(mx - m) for mx in maxes]
        den = sum(e * st for e, st in zip(es, stats))
        inv = 1.0 / jnp.where(head_mask, den, 1.0)
        acc = None
        for e, o_ref in zip(es, (o1_ref, o2_ref, o3_ref)):
            w = jnp.where(head_mask, e * inv, 0.0)
            hi = w.astype(BF16).astype(F32)
            lo = (w - hi).astype(BF16).astype(F32)
            cat = (hi + pltpu.roll(lo, 1, axis=1)).astype(BF16)
            term = jnp.dot(cat, e2_ref[...], preferred_element_type=F32) * o_ref[...].reshape(tm, width).astype(F32)
            acc = term if acc is None else acc + term
        rows = tm // n_res
        pitch = scr_ref.shape[1] // n_res
        for c in range(width // LANES):
            for r in range(n_res):
                scr_ref[c, r * pitch:r * pitch + rows, :] = acc[r * rows:(r + 1) * rows, c * LANES:(c + 1) * LANES]
        for n in range(rows):
            om_ref[n * n_res:(n + 1) * n_res, :] = jnp.concatenate(
                [scr_ref[c, pl.ds(n, n_res, stride=pitch), :] for c in range(width // LANES)], axis=1).astype(BF16)

    out_ref[...] = x_ref[...] + jnp.dot(om_ref[...], w_ref[...], preferred_element_type=F32)


def _outproj_dilated(x, outs, lses, w_o, tm, tn):
    t, d = x.shape
    width = w_o.shape[0]
    n_heads = width // HEAD_DIM
    tm = min(tm, t)
    rows = tm // RESIDUES
    per_super = SUPERBLOCK // tm
    row = _iota((LANES, width), 0)
    col = _iota((LANES, width), 1)
    stat_head = 2 * ((row % HEAD_DIM) // STAT_GROUP) + row // HEAD_DIM
    e2 = ((row % STAT_GROUP < 2) & (stat_head == col // HEAD_DIM)).astype(BF16)
    assert n_heads * STAT_GROUP // 2 <= HEAD_DIM
    shp = (t // SUPERBLOCK, RESIDUES, SUPERBLOCK // RESIDUES)
    o_spec = pl.BlockSpec((1, RESIDUES, rows, width), lambda i, j: (i // per_super, 0, i % per_super, 0))
    l_spec = pl.BlockSpec((1, RESIDUES, rows, LANES), lambda i, j: (i // per_super, 0, i % per_super, 0))
    return pl.pallas_call(
        _outproj_dilated_kernel,
        grid=(t // tm, d // tn),
        in_specs=[pl.BlockSpec((tm, tn), lambda i, j: (i, j)), o_spec, o_spec, o_spec, l_spec, l_spec, l_spec,
                  pl.BlockSpec((LANES, width), lambda i, j: (0, 0)),
                  pl.BlockSpec((width, tn), lambda i, j: (0, j),
                               pipeline_mode=pl.Buffered(1) if tn == d else None)],
        out_specs=pl.BlockSpec((tm, tn), lambda i, j: (i, j)),
        out_shape=jax.ShapeDtypeStruct((t, d), F32),
        scratch_shapes=[pltpu.VMEM((tm, width), BF16),
                        pltpu.VMEM((width // LANES, RESIDUES * (rows + STAGE_PITCH - RESIDUES), LANES), F32)],
        compiler_params=_compiler_params(("parallel", "arbitrary")),
        name="outproj_dilated",
    )(x, *[o.reshape(shp + (width,)) for o in outs], *[l.reshape(shp + (LANES,)) for l in lses], e2, w_o)


def _mlp_kernel(x_ref, g_ref, wu_ref, wd_ref, o_ref, h_ref):
    @pl.when(pl.program_id(1) == 0)
    def _():
        h_ref[...] = _rms_norm_rows(x_ref[...], g_ref[...]).astype(BF16)
        o_ref[...] = x_ref[...]

    u = jnp.dot(h_ref[...], wu_ref[...], preferred_element_type=F32)
    a = jnp.square(jnp.maximum(u, 0.0)).astype(BF16)
    o_ref[...] += jnp.dot(a, wd_ref[...], preferred_element_type=F32)


def _mlp(x, gain, w_up, w_down, tm, tf):
    t, d = x.shape
    ff = w_up.shape[1]
    tm = min(tm, t)
    return pl.pallas_call(
        _mlp_kernel,
        grid=(t // tm, ff // tf),
        in_specs=[
            pl.BlockSpec((tm, d), lambda i, j: (i, 0)),
            pl.BlockSpec((1, d), lambda i, j: (0, 0)),
            pl.BlockSpec((d, tf), lambda i, j: (0, j)),
            pl.BlockSpec((tf, d), lambda i, j: (j, 0)),
        ],
        out_specs=pl.BlockSpec((tm, d), lambda i, j: (i, 0)),
        out_shape=jax.ShapeDtypeStruct((t, d), F32),
        scratch_shapes=[pltpu.VMEM((tm, d), BF16)],
        compiler_params=_compiler_params(("parallel", "arbitrary")),
        name="sqrelu_mlp",
    )(x, gain.reshape(1, d).astype(F32), w_up, w_down)


def _swa_ssd_layer(x, gain, w_in, q_norm, k_norm, sinks, conv_w, conv_b, dt_bias, a_log, d_skip, gate_norm,
                   w_out, batch, seq):
    d_inner = gate_norm.shape[0]
    n_ssm_heads = dt_bias.shape[0]
    n_q_heads = sinks.shape[0]
    q_w = n_q_heads * HEAD_DIM
    kv_w = SWA_KV_HEADS * HEAD_DIM
    conv_dim = conv_w.shape[1]
    splits = (q_w, q_w + kv_w, q_w + 2 * kv_w, q_w + 2 * kv_w + d_inner, q_w + 2 * kv_w + d_inner + conv_dim)
    wq, wk, wv, wz, wxbc, wdt = jnp.split(w_in, splits, axis=1)
    w_qkv_dt = jnp.concatenate([wq, wk, wv, jnp.pad(wdt, ((0, 0), (0, LANES - n_ssm_heads)))], axis=1).astype(BF16)
    w_ssm = jnp.concatenate([wz, wxbc], axis=1).astype(BF16)
    qkv_w = q_w + 2 * kv_w
    head_gain = jnp.concatenate([jnp.tile(q_norm.astype(F32) * Q_SCALE, n_q_heads),
                                 jnp.tile(k_norm.astype(F32), SWA_KV_HEADS), jnp.ones((kv_w,), F32)])
    head_mask = jnp.concatenate([jnp.ones((q_w + kv_w,), F32), jnp.zeros((kv_w,), F32)])

    qkv, dt_raw = _norm_matmul_headnorm(x, gain, w_qkv_dt, head_gain, head_mask, tm=1024, tn=qkv_w,
                                        name="norm_qkv_dt_proj", n_extra=LANES)
    proj = _norm_matmul(x, gain, w_ssm, tm=1024, tn=w_ssm.shape[1] // 4, name="norm_ssm_proj")
    attn = _swa_attention(qkv, sinks, batch, seq, q_w)
    y = _ssd_mixer(proj, dt_raw, conv_w, conv_b, dt_bias, a_log, d_skip, gate_norm, batch, seq)
    w_out = w_out.astype(BF16)
    return _outproj_swa_ssd(x, attn, y, w_out[:q_w], w_out[q_w:], tm=1024, tn=1024)


def _dilated_layer(x, gain, w_qkv, q_norm, k_norm, w_o, batch, seq):
    width = w_o.shape[0]
    n_heads = width // HEAD_DIM
    w_qkv = w_qkv.astype(BF16)
    head_gain = jnp.concatenate([jnp.tile(q_norm.astype(F32) * Q_SCALE, n_heads),
                                 jnp.tile(k_norm.astype(F32), n_heads), jnp.ones((width,), F32)])
    head_mask = jnp.concatenate([jnp.ones((2 * width,), F32), jnp.zeros((width,), F32)])
    tn = 1024
    qkv = _norm_matmul_headnorm(x, gain, w_qkv, head_gain, head_mask, tm=1024, tn=tn, name="norm_qkv_proj",
                                interleave=RESIDUES, norm_tiles=2 * width // tn)
    outs, lses = _dilated_branches(qkv, batch, seq)
    return _outproj_dilated(x, outs, lses, w_o.astype(BF16), tm=512, tn=w_o.shape[1])


def kernel(x, norm_mix, norm_ffn, w_up, w_down, ab_w_in, ab_q_norm, ab_k_norm, ab_sinks, ab_conv_w, ab_conv_b,
           ab_dt_bias, ab_a_log, ab_d_skip, ab_gate_norm, ab_w_out, c_w_qkv, c_q_norm, c_k_norm, c_w_o):
    batch, seq, d_model = x.shape
    depth = norm_mix.shape[0]
    assert seq % SUPERBLOCK == 0 and seq % SSM_CHUNK == 0
    h = x.reshape(batch * seq, d_model).astype(F32)
    for layer in range(depth):
        i = layer // 2
        if layer % 2 == 0:
            h = _swa_ssd_layer(h, norm_mix[layer], ab_w_in[i], ab_q_norm[i], ab_k_norm[i], ab_sinks[i],
                               ab_conv_w[i], ab_conv_b[i], ab_dt_bias[i], ab_a_log[i], ab_d_skip[i],
                               ab_gate_norm[i], ab_w_out[i], batch, seq)
        else:
            h = _dilated_layer(h, norm_mix[layer], c_w_qkv[i], c_q_norm[i], c_k_norm[i], c_w_o[i], batch, seq)
        h = _mlp(h, norm_ffn[layer], w_up[layer].astype(BF16), w_down[layer].astype(BF16), tm=512, tf=1024)
    return h.reshape(batch, seq, d_model).astype(x.dtype)
```

```python
import functools
import math

import jax
import jax.numpy as jnp
from jax import lax
from jax.experimental import pallas as pl
from jax.experimental.pallas import tpu as pltpu

F32 = jnp.float32
BF16 = jnp.bfloat16

NORM_EPS = 1e-5
HEAD_DIM = 64
ATTN_BLOCK = 128
SWA_KV_HEADS = 2
SWA_WINDOW = 128
SSM_GROUPS = 4
SSM_STATE = 128
SSM_CONV = 4
SSM_CHUNK = 128
DIL_BRANCHES = ((128, 1), (512, 4), (2048, 16))
RESIDUES = 16
SUPERBLOCK = RESIDUES * ATTN_BLOCK
STAGE_PITCH = 20

LOG2E = math.log2(math.e)
Q_SCALE = HEAD_DIM ** -0.5 * LOG2E

LANES = 128
MXU_WIDTH = 256
CONV_HIST = 16
VMEM_LIMIT_BYTES = 56 * 1024 * 1024


def _compiler_params(semantics):
    return pltpu.CompilerParams(dimension_semantics=semantics, vmem_limit_bytes=VMEM_LIMIT_BYTES)


def _rms_norm_rows(x, gain):
    ms = jnp.mean(x * x, axis=-1, keepdims=True)
    return x * lax.rsqrt(ms + NORM_EPS) * gain


def _norm_rows_into(h_ref, x_ref, g_ref, interleave, xs_ref):
    if interleave is None:
        h_ref[...] = _rms_norm_rows(x_ref[...], g_ref[...]).astype(BF16)
        return
    n_chunks = xs_ref.shape[0]
    rows = x_ref.shape[0] // interleave
    pitch = xs_ref.shape[1] // rows
    for c in range(n_chunks):
        for k in range(rows):
            xs_ref[c, k * pitch:k * pitch + interleave, :] = x_ref[k * interleave:(k + 1) * interleave,
                                                                   c * LANES:(c + 1) * LANES]
    for r in range(interleave):
        xr = jnp.concatenate([xs_ref[c, pl.ds(r, rows, stride=pitch), :] for c in range(n_chunks)], axis=1)
        h_ref[r * rows:(r + 1) * rows, :] = _rms_norm_rows(xr, g_ref[...]).astype(BF16)


def _norm_matmul_kernel(x_ref, g_ref, w_ref, o_ref, h_ref, xs_ref=None, *, interleave):
    @pl.when(pl.program_id(1) == 0)
    def _():
        _norm_rows_into(h_ref, x_ref, g_ref, interleave, xs_ref)

    acc = jnp.dot(h_ref[...], w_ref[...], preferred_element_type=F32)
    o_ref[...] = acc.astype(o_ref.dtype).reshape(o_ref.shape)


def _norm_matmul_headnorm_kernel(x_ref, g_ref, w_ref, hg_ref, hm_ref, ones_ref, *rest, interleave, n_extra,
                                 norm_tiles):
    rest = list(rest)
    o_ref = rest.pop(0)
    extra_ref = rest.pop(0) if n_extra else None
    h_ref = rest.pop(0)
    xs_ref = rest.pop(0) if interleave is not None else None
    j = pl.program_id(1)

    @pl.when(j == 0)
    def _():
        _norm_rows_into(h_ref, x_ref, g_ref, interleave, xs_ref)

    n_main = w_ref.shape[1] - n_extra

    def head_norm_store():
        acc = jnp.dot(h_ref[...], w_ref[...], preferred_element_type=F32)
        if n_extra:
            extra_ref[...] = acc[:, n_main:]
        chunks = []
        for c in range(n_main // MXU_WIDTH):
            cols = slice(c * MXU_WIDTH, (c + 1) * MXU_WIDTH)
            a = acc[:, cols]
            ms = jnp.dot((a * a).astype(BF16), ones_ref[...], preferred_element_type=F32)
            r = lax.rsqrt(ms + NORM_EPS)
            if norm_tiles is None:
                r = jnp.where(hm_ref[:, cols] > 0.0, r, 1.0)
            chunks.append((a * r * hg_ref[:, cols]).astype(o_ref.dtype))
        o_ref[...] = jnp.concatenate(chunks, axis=1).reshape(o_ref.shape)

    if norm_tiles is None:
        head_norm_store()
    else:
        pl.when(j < norm_tiles)(head_norm_store)

        @pl.when(j >= norm_tiles)
        def _():
            acc = jnp.dot(h_ref[...], w_ref[...], preferred_element_type=F32)
            o_ref[...] = acc.astype(o_ref.dtype).reshape(o_ref.shape)


def _proj_scratch(tm, d, interleave):
    scratch = [pltpu.VMEM((tm, d), BF16)]
    if interleave is not None:
        scratch.append(pltpu.VMEM((d // LANES, tm // interleave * STAGE_PITCH, LANES), F32))
    return scratch


def _row_specs(t, tm, tn, interleave):
    if interleave is None:
        return (lambda n: jax.ShapeDtypeStruct((t, n), BF16)), pl.BlockSpec((tm, tn), lambda i, j: (i, j))
    per_super = SUPERBLOCK // tm
    rows = tm // interleave
    shape = lambda n: jax.ShapeDtypeStruct((t // SUPERBLOCK, interleave, SUPERBLOCK // interleave, n), BF16)
    spec = pl.BlockSpec((1, interleave, rows, tn), lambda i, j: (i // per_super, 0, i % per_super, j))
    return shape, spec


def _norm_matmul(x, gain, w, tm, tn, name, interleave=None):
    t, d = x.shape
    n = w.shape[1]
    tm = min(tm, t)
    shape, out_spec = _row_specs(t, tm, tn, interleave)
    out = pl.pallas_call(
        functools.partial(_norm_matmul_kernel, interleave=interleave),
        grid=(t // tm, n // tn),
        in_specs=[
            pl.BlockSpec((tm, d), lambda i, j: (i, 0)),
            pl.BlockSpec((1, d), lambda i, j: (0, 0)),
            pl.BlockSpec((d, tn), lambda i, j: (0, j)),
        ],
        out_specs=out_spec,
        out_shape=shape(n),
        scratch_shapes=_proj_scratch(tm, d, interleave),
        compiler_params=_compiler_params(("parallel", "arbitrary")),
        name=name,
    )(x, gain.reshape(1, d).astype(F32), w)
    return out.reshape(t, n)


def _head_ones():
    r = lax.broadcasted_iota(jnp.int32, (MXU_WIDTH, MXU_WIDTH), 0)
    c = lax.broadcasted_iota(jnp.int32, (MXU_WIDTH, MXU_WIDTH), 1)
    return jnp.where((r // HEAD_DIM) == (c // HEAD_DIM), 1.0 / HEAD_DIM, 0.0).astype(BF16)


def _norm_matmul_headnorm(x, gain, w, head_gain, head_mask, tm, tn, name, interleave=None, n_extra=0,
                          norm_tiles=None):
    t, d = x.shape
    n_main = w.shape[1] - n_extra
    tm = min(tm, t)
    assert n_extra == 0 or tn == n_main
    shape, out_spec = _row_specs(t, tm, tn, interleave)
    out_shapes = [shape(n_main)]
    out_specs = [out_spec]
    if n_extra:
        out_shapes.append(jax.ShapeDtypeStruct((t, n_extra), F32))
        out_specs.append(pl.BlockSpec((tm, n_extra), lambda i, j: (i, 0)))
    outs = pl.pallas_call(
        functools.partial(_norm_matmul_headnorm_kernel, interleave=interleave, n_extra=n_extra,
                          norm_tiles=norm_tiles),
        grid=(t // tm, n_main // tn),
        in_specs=[
            pl.BlockSpec((tm, d), lambda i, j: (i, 0)),
            pl.BlockSpec((1, d), lambda i, j: (0, 0)),
            pl.BlockSpec((d, tn + n_extra), lambda i, j: (0, j)),
            pl.BlockSpec((1, tn), lambda i, j: (0, j)),
            pl.BlockSpec((1, tn), lambda i, j: (0, j)),
            pl.BlockSpec((MXU_WIDTH, MXU_WIDTH), lambda i, j: (0, 0)),
        ],
        out_specs=out_specs,
        out_shape=out_shapes,
        scratch_shapes=_proj_scratch(tm, d, interleave),
        compiler_params=_compiler_params(("parallel", "arbitrary")),
        name=name,
    )(x, gain.reshape(1, d).astype(F32), w, head_gain.reshape(1, n_main).astype(F32),
      head_mask.reshape(1, n_main).astype(F32), _head_ones())
    if n_extra:
        return outs[0].reshape(t, n_main), outs[1]
    return outs[0].reshape(t, n_main)


def _lane_lo():
    return lax.broadcasted_iota(jnp.int32, (1, LANES), 1) < HEAD_DIM


def _attn_tiles(qs, ks, vs, masks, lane_lo, sinks=None, normalize=False):
    sel_rows = _iota((4 * ATTN_BLOCK, LANES), 0) < 2 * ATTN_BLOCK
    row_sum = (sel_rows == (_iota((4 * ATTN_BLOCK, LANES), 1) < HEAD_DIM)).astype(BF16)
    results = []
    for i, (q, k, v, mask) in enumerate(zip(qs, ks, vs, masks)):
        zq = jnp.zeros_like(q)
        q_heads = jnp.concatenate([jnp.where(lane_lo, q, zq), jnp.where(lane_lo, zq, q)], axis=0)
        s2 = lax.dot_general(q_heads, k, (((1,), (1,)), ((), ())), preferred_element_type=F32)
        ps, ms = [], []
        for idx in range(2):
            s = jnp.where(mask, s2[idx * ATTN_BLOCK:(idx + 1) * ATTN_BLOCK], -jnp.inf)
            m = jnp.max(s, axis=-1, keepdims=True)
            if sinks is not None:
                m = jnp.maximum(m, sinks[i][idx])
            ps.append(jnp.exp2(s - m).astype(BF16))
            ms.append(m)
        zv = jnp.zeros_like(v)
        v_heads = jnp.concatenate([jnp.where(lane_lo, v, zv), jnp.where(lane_lo, zv, v)], axis=0)
        res = jnp.dot(jnp.concatenate(ps, axis=1), jnp.concatenate([v_heads, row_sum], axis=1),
                      preferred_element_type=F32)
        o, l = res[:, :LANES], res[:, LANES:]
        if sinks is not None:
            l = l + jnp.where(lane_lo, jnp.exp2(sinks[i][0] - ms[0]), jnp.exp2(sinks[i][1] - ms[1]))
        if normalize:
            o = o * (1.0 / l)
        results.append((o, ms, l))
    return results


def _dist_mask(q_pos, k_pos, max_dist, key_exists=None):
    dist = q_pos - k_pos
    mask = (dist >= 0) & (dist <= max_dist)
    if key_exists is not None:
        mask = mask & key_exists
    return mask


def _iota(shape, dim):
    return lax.broadcasted_iota(jnp.int32, shape, dim)


SWA_BLOCKS_PER_STEP = 8


def _swa_kernel(sinks_ref, q_ref, kvp_ref, kvc_ref, o_ref):
    first = pl.program_id(1) == 0
    lane_lo = _lane_lo()
    kj = _iota((1, 2 * ATTN_BLOCK), 1)
    mask = _dist_mask(_iota((ATTN_BLOCK, 1), 0) + ATTN_BLOCK, kj, SWA_WINDOW - 1)
    mask_first = mask & ((kj >= ATTN_BLOCK) | jnp.logical_not(first))
    kv = jnp.concatenate([kvp_ref[...], kvc_ref[...]], axis=0).astype(F32)

    def dup_heads(x):
        xr = pltpu.roll(x, HEAD_DIM, axis=1)
        return jnp.where(lane_lo, x, xr).astype(BF16), jnp.where(lane_lo, xr, x).astype(BF16)

    k_dup = dup_heads(kv[:, :LANES])
    v_dup = dup_heads(kv[:, LANES:])
    n_blocks = q_ref.shape[0] // ATTN_BLOCK
    n_pairs = q_ref.shape[1] // LANES
    pairs_per_kv = n_pairs // SWA_KV_HEADS
    qs, ks, vs, masks, sinks = [], [], [], [], []
    for blk in range(n_blocks):
        rows = slice(blk * ATTN_BLOCK, (blk + 1) * ATTN_BLOCK)
        keys = slice(blk * ATTN_BLOCK, (blk + 2) * ATTN_BLOCK)
        for j in range(n_pairs):
            kh = j // pairs_per_kv
            qs.append(q_ref[rows, j * LANES:(j + 1) * LANES])
            ks.append(k_dup[kh][keys])
            vs.append(v_dup[kh][keys])
            masks.append(mask_first if blk == 0 else mask)
            sinks.append((sinks_ref[2 * j], sinks_ref[2 * j + 1]))
    for idx, (o, _, _) in enumerate(_attn_tiles(qs, ks, vs, masks, lane_lo, sinks, normalize=True)):
        blk, j = divmod(idx, n_pairs)
        o_ref[blk * ATTN_BLOCK:(blk + 1) * ATTN_BLOCK, j * LANES:(j + 1) * LANES] = o.astype(o_ref.dtype)


def _swa_attention(qkv, sinks, batch, seq, q_width):
    t = qkv.shape[0]
    rows = SWA_BLOCKS_PER_STEP * ATTN_BLOCK
    nb = seq // rows
    kv_block = q_width // (2 * LANES)
    prev_block = lambda b, n: jnp.maximum((b * nb + n) * SWA_BLOCKS_PER_STEP - 1, 0)
    return pl.pallas_call(
        _swa_kernel,
        grid=(batch, nb),
        in_specs=[
            pl.BlockSpec(memory_space=pltpu.SMEM),
            pl.BlockSpec((rows, q_width), lambda b, n: (b * nb + n, 0)),
            pl.BlockSpec((ATTN_BLOCK, 2 * LANES), lambda b, n: (prev_block(b, n), kv_block)),
            pl.BlockSpec((rows, 2 * LANES), lambda b, n: (b * nb + n, kv_block)),
        ],
        out_specs=pl.BlockSpec((rows, q_width), lambda b, n: (b * nb + n, 0)),
        out_shape=jax.ShapeDtypeStruct((t, q_width), BF16),
        compiler_params=_compiler_params(("parallel", "arbitrary")),
        name="swa_attention",
    )(sinks.astype(F32) * LOG2E, qkv, qkv, qkv)


STAT_GROUP = 4


def _write_stats(acc, lane, pair, ms, l):
    lane_lo = lane < HEAD_DIM
    stat = jnp.where(lane % STAT_GROUP < STAT_GROUP // 2, l, jnp.where(lane_lo, ms[0], ms[1]))
    return jnp.where((lane % HEAD_DIM) // STAT_GROUP == pair, stat, acc)


def _dil16_kernel(q_ref, kp_ref, kc_ref, vp_ref, vc_ref, o_ref, lse_ref):
    first = pl.program_id(2) == 0
    n_res = q_ref.shape[1]
    n_pairs = q_ref.shape[3] // LANES
    lane = _iota((1, LANES), 1)
    lane_lo = lane < HEAD_DIM
    kj = _iota((1, 2 * ATTN_BLOCK), 1)
    mask = _dist_mask(_iota((ATTN_BLOCK, 1), 0) + ATTN_BLOCK, kj, ATTN_BLOCK,
                      (kj >= ATTN_BLOCK) | jnp.logical_not(first))

    def body(j, lse_accs):
        cols = pl.ds(pl.multiple_of(j * LANES, LANES), LANES)
        qs = [q_ref[0, r, :, cols] for r in range(n_res)]
        ks = [jnp.concatenate([kp_ref[0, r, :, cols], kc_ref[0, r, :, cols]], axis=0) for r in range(n_res)]
        vs = [jnp.concatenate([vp_ref[0, r, :, cols], vc_ref[0, r, :, cols]], axis=0) for r in range(n_res)]
        out = []
        for r, (o, ms, ls) in enumerate(_attn_tiles(qs, ks, vs, [mask] * n_res, lane_lo)):
            o_ref[0, r, :, cols] = o.astype(o_ref.dtype)
            out.append(_write_stats(lse_accs[r], lane, j, ms, ls))
        return tuple(out)

    zeros = tuple(jnp.zeros((ATTN_BLOCK, LANES), F32) for _ in range(n_res))
    for r, acc in enumerate(lax.fori_loop(0, n_pairs, body, zeros, unroll=True)):
        lse_ref[0, r] = acc


def _dil4_kernel(q_ref, kp_ref, kc_ref, vp_ref, vc_ref, o_ref, lse_ref):
    first = pl.program_id(2) == 0
    n_sub = q_ref.shape[1]
    rows = ATTN_BLOCK // n_sub
    n_tiles = q_ref.shape[3] // rows
    n_pairs = q_ref.shape[4] // LANES
    lane = _iota((1, LANES), 1)
    lane_lo = lane < HEAD_DIM
    qi = _iota((ATTN_BLOCK, 1), 0)
    kj = _iota((1, 2 * ATTN_BLOCK), 1)
    q_pos = n_sub * (rows + qi % rows) + qi // rows
    k_pos = n_sub * (kj % (2 * rows)) + kj // (2 * rows)
    mask = _dist_mask(q_pos, k_pos, ATTN_BLOCK)
    mask_first = mask & ((kj % (2 * rows) >= rows) | jnp.logical_not(first))

    def gather(cur_ref, prev_ref, tile, cols):
        if tile == 0:
            parts = [jnp.concatenate([prev_ref[0, a, 0, :, cols], cur_ref[0, a, 0, 0:rows, cols]], axis=0)
                     for a in range(n_sub)]
        else:
            parts = [cur_ref[0, a, 0, (tile - 1) * rows:(tile + 1) * rows, cols] for a in range(n_sub)]
        return jnp.concatenate(parts, axis=0)

    def body(j, lse_accs):
        cols = pl.ds(pl.multiple_of(j * LANES, LANES), LANES)
        qs = [jnp.concatenate([q_ref[0, a, 0, t * rows:(t + 1) * rows, cols] for a in range(n_sub)], axis=0)
              for t in range(n_tiles)]
        ks = [gather(kc_ref, kp_ref, t, cols) for t in range(n_tiles)]
        vs = [gather(vc_ref, vp_ref, t, cols) for t in range(n_tiles)]
        masks = [mask_first] + [mask] * (n_tiles - 1)
        out = []
        for t, (o, ms, ls) in enumerate(_attn_tiles(qs, ks, vs, masks, lane_lo)):
            ob = o.astype(o_ref.dtype)
            for a in range(n_sub):
                o_ref[0, a, 0, t * rows:(t + 1) * rows, cols] = ob[a * rows:(a + 1) * rows, :]
            out.append(_write_stats(lse_accs[t], lane, j, ms, ls))
        return tuple(out)

    zeros = tuple(jnp.zeros((ATTN_BLOCK, LANES), F32) for _ in range(n_tiles))
    for t, acc in enumerate(lax.fori_loop(0, n_pairs, body, zeros, unroll=True)):
        for a in range(n_sub):
            lse_ref[0, a, 0, t * rows:(t + 1) * rows, :] = acc[a * rows:(a + 1) * rows, :]


def _dil1_kernel(q_ref, kp_ref, kc_ref, vp_ref, vc_ref, o_ref, lse_ref):
    first = (pl.program_id(1) == 0) & (pl.program_id(2) == 0)
    n_res = q_ref.shape[1]
    sub = ATTN_BLOCK // n_res
    n_tiles = q_ref.shape[2] // sub
    n_pairs = q_ref.shape[3] // LANES
    prev_rows = kp_ref.shape[2]
    lane = _iota((1, LANES), 1)
    lane_lo = lane < HEAD_DIM
    qi = _iota((ATTN_BLOCK, 1), 0)
    kj = _iota((1, 2 * ATTN_BLOCK), 1)
    q_pos = n_res * (sub + qi % sub) + qi // sub
    k_pos = n_res * (kj % (2 * sub)) + kj // (2 * sub)
    mask = _dist_mask(q_pos, k_pos, ATTN_BLOCK)
    mask_first = mask & ((kj % (2 * sub) >= sub) | jnp.logical_not(first))

    def tile_rows(x, lo, hi):
        return x[:, lo:hi, :].reshape(n_res * (hi - lo), LANES)

    def body(j, lse_accs):
        cols = pl.ds(pl.multiple_of(j * LANES, LANES), LANES)
        q = q_ref[0, :, :, cols].astype(F32)
        qs = [tile_rows(q, t * sub, (t + 1) * sub).astype(BF16) for t in range(n_tiles)]

        def gather(cur_ref, prev_ref):
            cur = cur_ref[0, :, :, cols].astype(F32)
            prev = prev_ref[0, :, :, cols].astype(F32)
            tiles = []
            for t in range(n_tiles):
                if t == 0:
                    x = jnp.concatenate([prev[:, prev_rows - sub:, :], cur[:, 0:sub, :]], axis=1)
                else:
                    x = cur[:, (t - 1) * sub:(t + 1) * sub, :]
                tiles.append(x.reshape(n_res * 2 * sub, LANES).astype(BF16))
            return tiles

        ks = gather(kc_ref, kp_ref)
        vs = gather(vc_ref, vp_ref)
        masks = [mask_first] + [mask] * (n_tiles - 1)
        res = _attn_tiles(qs, ks, vs, masks, lane_lo)
        o_full = jnp.concatenate([o.reshape(n_res, sub, LANES) for o, _, _ in res], axis=1)
        o_ref[0, :, :, cols] = o_full.astype(o_ref.dtype)
        return tuple(_write_stats(lse_accs[t], lane, j, ms, ls) for t, (_, ms, ls) in enumerate(res))

    zeros = tuple(jnp.zeros((ATTN_BLOCK, LANES), F32) for _ in range(n_tiles))
    accs = lax.fori_loop(0, n_pairs, body, zeros, unroll=True)
    lse_ref[0] = jnp.concatenate([acc.reshape(n_res, sub, LANES) for acc in accs], axis=1)


def _dilated_branches(qkv, batch, seq):
    t, w = qkv.shape[0], qkv.shape[1] // 3
    nsb = seq // SUPERBLOCK
    n_super = t // SUPERBLOCK
    per_res = SUPERBLOCK // RESIDUES
    outs, lses = [], []

    def in_specs(block, cur_map, prev_block, prev_map):
        cur = lambda col: pl.BlockSpec(block, lambda *g: cur_map(*g) + (col,))
        prev = lambda col: pl.BlockSpec(prev_block, lambda *g: prev_map(*g) + (col,))
        return [cur(0), prev(1), cur(1), prev(2), cur(2)]

    r16 = 4
    shp = (n_super, RESIDUES, per_res)
    cur_map = lambda b, r, s: (b * nsb + s, r, 0)
    prev_map = lambda b, r, s: (b * nsb + jnp.maximum(s - 1, 0), r, 0)
    qkv4 = qkv.reshape(shp + (3 * w,))
    o16, l16 = pl.pallas_call(
        _dil16_kernel, grid=(batch, RESIDUES // r16, nsb),
        in_specs=in_specs((1, r16, per_res, w), cur_map, (1, r16, per_res, w), prev_map),
        out_specs=[pl.BlockSpec((1, r16, per_res, w), lambda *g: cur_map(*g) + (0,)),
                   pl.BlockSpec((1, r16, per_res, LANES), lambda *g: cur_map(*g) + (0,))],
        out_shape=[jax.ShapeDtypeStruct(shp + (w,), BF16), jax.ShapeDtypeStruct(shp + (LANES,), F32)],
        compiler_params=_compiler_params(("parallel", "parallel", "arbitrary")),
        name="dilated_attention_d16",
    )(qkv4, qkv4, qkv4, qkv4, qkv4)

    shp5 = (n_super, 4, 4, per_res)
    tail = per_res // 4
    cur_map5 = lambda b, r, s: (b * nsb + s, 0, r, 0)
    prev_map5 = lambda b, r, s: (b * nsb + jnp.maximum(s - 1, 0), 0, r, per_res // tail - 1)
    qkv5 = qkv.reshape(shp5 + (3 * w,))
    o4, l4 = pl.pallas_call(
        _dil4_kernel, grid=(batch, 4, nsb),
        in_specs=in_specs((1, 4, 1, per_res, w), cur_map5, (1, 4, 1, tail, w), prev_map5),
        out_specs=[pl.BlockSpec((1, 4, 1, per_res, w), lambda *g: cur_map5(*g) + (0,)),
                   pl.BlockSpec((1, 4, 1, per_res, LANES), lambda *g: cur_map5(*g) + (0,))],
        out_shape=[jax.ShapeDtypeStruct(shp5 + (w,), BF16), jax.ShapeDtypeStruct(shp5 + (LANES,), F32)],
        compiler_params=_compiler_params(("parallel", "parallel", "arbitrary")),
        name="dilated_attention_d4",
    )(qkv5, qkv5, qkv5, qkv5, qkv5)

    run, hist = 32, 16
    runs = per_res // run
    cur_map1 = lambda b, s, c: (b * nsb + s, 0, c)

    def prev_map1(b, s, c):
        g = (b * nsb + s) * (per_res // hist) + c * (run // hist) - 1
        g = jnp.maximum(g, 0)
        return (g // (per_res // hist), 0, g % (per_res // hist))

    o1, l1 = pl.pallas_call(
        _dil1_kernel, grid=(batch, nsb, runs),
        in_specs=in_specs((1, RESIDUES, run, w), cur_map1, (1, RESIDUES, hist, w), prev_map1),
        out_specs=[pl.BlockSpec((1, RESIDUES, run, w), lambda *g: cur_map1(*g) + (0,)),
                   pl.BlockSpec((1, RESIDUES, run, LANES), lambda *g: cur_map1(*g) + (0,))],
        out_shape=[jax.ShapeDtypeStruct(shp + (w,), BF16), jax.ShapeDtypeStruct(shp + (LANES,), F32)],
        compiler_params=_compiler_params(("parallel", "arbitrary", "arbitrary")),
        name="dilated_attention_d1",
    )(qkv4, qkv4, qkv4, qkv4, qkv4)

    for o, l in ((o1, l1), (o4, l4), (o16, l16)):
        outs.append(o.reshape(t, w))
        lses.append(l.reshape(t, LANES))
    return outs, lses


def _softplus(x):
    return jnp.maximum(x, 0.0) + jnp.log1p(jnp.exp(-jnp.abs(x)))


def _silu(x):
    h = 0.5 * x
    return h * jnp.tanh(h) + h


def _expand_heads(v, head_mask, e3):
    n_heads = e3.shape[1] // HEAD_DIM
    v = jnp.where(head_mask, v, 0.0)
    hi = v.astype(BF16).astype(F32)
    r = v - hi
    mid = r.astype(BF16).astype(F32)
    lo = (r - mid).astype(BF16).astype(F32)
    cat = hi + pltpu.roll(mid, n_heads, axis=1) + pltpu.roll(lo, 2 * n_heads, axis=1)
    return jnp.dot(cat.astype(BF16), e3, preferred_element_type=F32)


def _ssd_kernel(xs_ref, bc_ref, z_ref, dt_ref, cw_ref, cb_ref, dtb_ref, a_ref, dskip_ref, gn_ref, tril_ref, e3_ref,
                shift_ref, y_ref, hist_ref, state_ref):
    chunk = SSM_CHUNK
    d_inner = z_ref.shape[1]
    n_heads = d_inner // HEAD_DIM
    group_w = d_inner // SSM_GROUPS
    heads_per_group = n_heads // SSM_GROUPS

    @pl.when(pl.program_id(1) == 0)
    def _():
        hist_ref[...] = jnp.zeros(hist_ref.shape, hist_ref.dtype)
        state_ref[...] = jnp.zeros(state_ref.shape, F32)

    parts = []
    for ref, cols in ((xs_ref, slice(0, d_inner)), (bc_ref, slice(d_inner, hist_ref.shape[1]))):
        cur = ref[...]
        ext = jnp.concatenate([hist_ref[:, cols], cur], axis=0)
        shifted = jnp.dot(shift_ref[...], ext, preferred_element_type=F32)
        conv = cb_ref[:, cols] + cw_ref[SSM_CONV - 1:SSM_CONV, cols] * cur.astype(F32)
        for j in range(SSM_CONV - 1):
            conv = conv + cw_ref[j:j + 1, cols] * shifted[j * chunk:(j + 1) * chunk]
        hist_ref[:, cols] = cur[chunk - CONV_HIST:, :]
        parts.append(_silu(conv))
    xs = parts[0]
    xbc_bc = parts[1]

    lane = _iota((1, LANES), 1)
    head_mask = lane < n_heads
    lane_lo = lane < HEAD_DIM
    e3 = e3_ref[...]

    dt = jnp.where(head_mask, _softplus(dt_ref[...] + dtb_ref[...]), 0.0)
    cum = jnp.dot(tril_ref[...], dt * a_ref[...], precision=lax.Precision.HIGHEST,
                  preferred_element_type=F32)
    cum_t = cum.T
    ecum_e = _expand_heads(jnp.exp(cum), head_mask, e3)
    dte_e = _expand_heads(jnp.exp(cum[chunk - 1:chunk, :] - cum), head_mask, e3)
    xdt = xs * _expand_heads(dt, head_mask, e3)
    xend = xdt * dte_e

    causal = _iota((chunk, chunk), 0) >= _iota((chunk, chunk), 1)

    for g in range(SSM_GROUPS):
        gcols = slice(g * group_w, (g + 1) * group_w)
        b_f32 = xbc_bc[:, g * SSM_STATE: (g + 1) * SSM_STATE]
        c_off = SSM_GROUPS * SSM_STATE
        c_bf = xbc_bc[:, c_off + g * SSM_STATE: c_off + (g + 1) * SSM_STATE].astype(BF16)
        cb = lax.dot_general(c_bf, b_f32.astype(BF16), (((1,), (1,)), ((), ())), preferred_element_type=F32)
        st = state_ref[g]
        y_off = jnp.dot(c_bf, st.astype(BF16), preferred_element_type=F32) * ecum_e[:, gcols]
        state_ref[g] = st * ecum_e[chunk - 1:chunk, gcols] + jnp.dot(
            b_f32.T.astype(BF16), xend[:, gcols].astype(BF16), preferred_element_type=F32)

        y_pairs = []
        for pair in range(heads_per_group // 2):
            h0 = g * heads_per_group + 2 * pair
            ms = []
            for h in (h0, h0 + 1):
                diff = cum[:, h:h + 1] - cum_t[h:h + 1, :]
                seg = jnp.exp(jnp.where(causal, diff, -jnp.inf))
                ms.append((cb * seg).astype(BF16))
            xp = xdt[:, h0 * HEAD_DIM: h0 * HEAD_DIM + LANES]
            x_heads = jnp.concatenate([jnp.where(lane_lo, xp, 0.0), jnp.where(lane_lo, 0.0, xp)], axis=0)
            y_pairs.append(jnp.dot(jnp.concatenate(ms, axis=1), x_heads.astype(BF16),
                                   preferred_element_type=F32))
        y = jnp.concatenate(y_pairs, axis=1) + y_off + dskip_ref[:, gcols] * xs[:, gcols]
        gy = y * _silu(z_ref[:, gcols].astype(F32))
        ms_g = jnp.mean(gy * gy, axis=-1, keepdims=True)
        y_ref[:, gcols] = (gy * lax.rsqrt(ms_g + NORM_EPS) * gn_ref[:, gcols]).astype(y_ref.dtype)


def _ssd_mixer(proj, dt_raw, conv_w, conv_b, dt_bias, a_log, d_skip, gate_norm, batch, seq):
    t = proj.shape[0]
    d_inner = gate_norm.shape[0]
    n_heads = d_inner // HEAD_DIM
    conv_dim = conv_w.shape[1]
    bc_w = conv_dim - d_inner
    nc = seq // SSM_CHUNK
    assert d_inner % bc_w == 0

    def pad_heads(v):
        return jnp.zeros((1, LANES), F32).at[0, :n_heads].set(v.astype(F32))

    tril = (_iota((SSM_CHUNK, SSM_CHUNK), 0) >= _iota((SSM_CHUNK, SSM_CHUNK), 1)).astype(F32)
    row = _iota((LANES, d_inner), 0)
    col = _iota((LANES, d_inner), 1)
    e3 = ((row < 3 * n_heads) & ((row % n_heads) == (col // HEAD_DIM))).astype(BF16)
    srow = _iota(((SSM_CONV - 1) * SSM_CHUNK, CONV_HIST + SSM_CHUNK), 0)
    scol = _iota(((SSM_CONV - 1) * SSM_CHUNK, CONV_HIST + SSM_CHUNK), 1)
    shift = (scol == CONV_HIST + srow % SSM_CHUNK - (SSM_CONV - 1 - srow // SSM_CHUNK)).astype(BF16)
    const = lambda shape: pl.BlockSpec(shape, lambda b, c: (0, 0))
    return pl.pallas_call(
        _ssd_kernel,
        grid=(batch, nc),
        in_specs=[
            pl.BlockSpec((SSM_CHUNK, d_inner), lambda b, c: (b * nc + c, 1)),
            pl.BlockSpec((SSM_CHUNK, bc_w), lambda b, c: (b * nc + c, 2 * d_inner // bc_w)),
            pl.BlockSpec((SSM_CHUNK, d_inner), lambda b, c: (b * nc + c, 0)),
            pl.BlockSpec((SSM_CHUNK, LANES), lambda b, c: (b * nc + c, 0)),
            const((SSM_CONV, conv_dim)), const((1, conv_dim)), const((1, LANES)), const((1, LANES)),
            const((1, d_inner)), const((1, d_inner)), const((SSM_CHUNK, SSM_CHUNK)), const((LANES, d_inner)),
            const(shift.shape),
        ],
        out_specs=pl.BlockSpec((SSM_CHUNK, d_inner), lambda b, c: (b * nc + c, 0)),
        out_shape=jax.ShapeDtypeStruct((t, d_inner), BF16),
        scratch_shapes=[pltpu.VMEM((CONV_HIST, conv_dim), BF16),
                        pltpu.VMEM((SSM_GROUPS, SSM_STATE, d_inner // SSM_GROUPS), F32)],
        compiler_params=_compiler_params(("parallel", "arbitrary")),
        name="ssd_mixer",
    )(proj, proj, proj, dt_raw, conv_w.astype(F32), conv_b.reshape(1, conv_dim).astype(F32),
      pad_heads(dt_bias), pad_heads(-jnp.exp(a_log.astype(F32))),
      jnp.repeat(d_skip.astype(F32), HEAD_DIM).reshape(1, d_inner), gate_norm.reshape(1, d_inner).astype(F32),
      tril, e3, shift)


def _outproj_swa_ssd_kernel(x_ref, a_ref, y_ref, wa_ref, wy_ref, o_ref):
    o_ref[...] = (x_ref[...]
                  + jnp.dot(a_ref[...], wa_ref[...], preferred_element_type=F32)
                  + jnp.dot(y_ref[...], wy_ref[...], preferred_element_type=F32))


def _outproj_swa_ssd(x, attn, y, w_attn, w_y, tm, tn):
    t, d = x.shape
    tm = min(tm, t)
    return pl.pallas_call(
        _outproj_swa_ssd_kernel,
        grid=(t // tm, d // tn),
        in_specs=[
            pl.BlockSpec((tm, tn), lambda i, j: (i, j)),
            pl.BlockSpec((tm, attn.shape[1]), lambda i, j: (i, 0)),
            pl.BlockSpec((tm, y.shape[1]), lambda i, j: (i, 0)),
            pl.BlockSpec((attn.shape[1], tn), lambda i, j: (0, j)),
            pl.BlockSpec((y.shape[1], tn), lambda i, j: (0, j)),
        ],
        out_specs=pl.BlockSpec((tm, tn), lambda i, j: (i, j)),
        out_shape=jax.ShapeDtypeStruct((t, d), F32),
        compiler_params=_compiler_params(("parallel", "arbitrary")),
        name="outproj_swa_ssd",
    )(x, attn, y, w_attn, w_y)


def _outproj_dilated_kernel(x_ref, o1_ref, o2_ref, o3_ref, l1_ref, l2_ref, l3_ref, e2_ref, w_ref, out_ref,
                            om_ref, scr_ref):
    @pl.when(pl.program_id(1) == 0)
    def _():
        tm, width = om_ref.shape
        n_res = o1_ref.shape[1]
        lane = _iota((1, LANES), 1)
        head_mask = lane % STAT_GROUP == 0
        stats = [l_ref[...].reshape(tm, LANES) for l_ref in (l1_ref, l2_ref, l3_ref)]
        maxes = [pltpu.roll(st, LANES - STAT_GROUP // 2, axis=1) for st in stats]
        m = jnp.maximum(jnp.maximum(maxes[0], maxes[1]), maxes[2])
        es = [jnp.exp2(mx - m) for mx in maxes]
        den = sum(e * st for e, st in zip(es, stats))
        inv = 1.0 / jnp.where(head_mask, den, 1.0)
        acc = None
        for e, o_ref in zip(es, (o1_ref, o2_ref, o3_ref)):
            w = jnp.where(head_mask, e * inv, 0.0)
            hi = w.astype(BF16).astype(F32)
            lo = (w - hi).astype(BF16).astype(F32)
            cat = (hi + pltpu.roll(lo, 1, axis=1)).astype(BF16)
            term = jnp.dot(cat, e2_ref[...], preferred_element_type=F32) * o_ref[...].reshape(tm, width).astype(F32)
            acc = term if acc is None else acc + term
        rows = tm // n_res
        pitch = scr_ref.shape[1] // n_res
        for c in range(width // LANES):
            for r in range(n_res):
                scr_ref[c, r * pitch:r * pitch + rows, :] = acc[r * rows:(r + 1) * rows, c * LANES:(c + 1) * LANES]
        for n in range(rows):
            om_ref[n * n_res:(n + 1) * n_res, :] = jnp.concatenate(
                [scr_ref[c, pl.ds(n, n_res, stride=pitch), :] for c in range(width // LANES)], axis=1).astype(BF16)

    out_ref[...] = x_ref[...] + jnp.dot(om_ref[...], w_ref[...], preferred_element_type=F32)


def _outproj_dilated(x, outs, lses, w_o, tm, tn):
    t, d = x.shape
    width = w_o.shape[0]
    n_heads = width // HEAD_DIM
    tm = min(tm, t)
    rows = tm // RESIDUES
    per_super = SUPERBLOCK // tm
    row = _iota((LANES, width), 0)
    col = _iota((LANES, width), 1)
    stat_head = 2 * ((row % HEAD_DIM) // STAT_GROUP) + row // HEAD_DIM
    e2 = ((row % STAT_GROUP < 2) & (stat_head == col // HEAD_DIM)).astype(BF16)
    assert n_heads * STAT_GROUP // 2 <= HEAD_DIM
    shp = (t // SUPERBLOCK, RESIDUES, SUPERBLOCK // RESIDUES)
    o_spec = pl.BlockSpec((1, RESIDUES, rows, width), lambda i, j: (i // per_super, 0, i % per_super, 0))
    l_spec = pl.BlockSpec((1, RESIDUES, rows, LANES), lambda i, j: (i // per_super, 0, i % per_super, 0))
    return pl.pallas_call(
        _outproj_dilated_kernel,
        grid=(t // tm, d // tn),
        in_specs=[pl.BlockSpec((tm, tn), lambda i, j: (i, j)), o_spec, o_spec, o_spec, l_spec, l_spec, l_spec,
                  pl.BlockSpec((LANES, width), lambda i, j: (0, 0)),
                  pl.BlockSpec((width, tn), lambda i, j: (0, j),
                               pipeline_mode=pl.Buffered(1) if tn == d else None)],
        out_specs=pl.BlockSpec((tm, tn), lambda i, j: (i, j)),
        out_shape=jax.ShapeDtypeStruct((t, d), F32),
        scratch_shapes=[pltpu.VMEM((tm, width), BF16),
                        pltpu.VMEM((width // LANES, RESIDUES * (rows + STAGE_PITCH - RESIDUES), LANES), F32)],
        compiler_params=_compiler_params(("parallel", "arbitrary")),
        name="outproj_dilated",
    )(x, *[o.reshape(shp + (width,)) for o in outs], *[l.reshape(shp + (LANES,)) for l in lses], e2, w_o)


def _mlp_kernel(x_ref, g_ref, wu_ref, wd_ref, o_ref, h_ref):
    def ff_tile():
        u = jnp.dot(h_ref[...], wu_ref[...], preferred_element_type=F32)
        a = jnp.square(jnp.maximum(u, 0.0)).astype(BF16)
        return jnp.dot(a, wd_ref[...], preferred_element_type=F32)

    @pl.when(pl.program_id(1) == 0)
    def _():
        h_ref[...] = _rms_norm_rows(x_ref[...], g_ref[...]).astype(BF16)
        o_ref[...] = x_ref[...] + ff_tile()

    @pl.when(pl.program_id(1) > 0)
    def _():
        o_ref[...] += ff_tile()


def _mlp(x, gain, w_up, w_down, tm, tf):
    t, d = x.shape
    ff = w_up.shape[1]
    tm = min(tm, t)
    return pl.pallas_call(
        _mlp_kernel,
        grid=(t // tm, ff // tf),
        in_specs=[
            pl.BlockSpec((tm, d), lambda i, j: (i, 0)),
            pl.BlockSpec((1, d), lambda i, j: (0, 0)),
            pl.BlockSpec((d, tf), lambda i, j: (0, j)),
            pl.BlockSpec((tf, d), lambda i, j: (j, 0)),
        ],
        out_specs=pl.BlockSpec((tm, d), lambda i, j: (i, 0)),
        out_shape=jax.ShapeDtypeStruct((t, d), F32),
        scratch_shapes=[pltpu.VMEM((tm, d), BF16)],
        compiler_params=_compiler_params(("parallel", "arbitrary")),
        name="sqrelu_mlp",
    )(x, gain.reshape(1, d).astype(F32), w_up, w_down)


def _swa_ssd_layer(x, gain, w_in, q_norm, k_norm, sinks, conv_w, conv_b, dt_bias, a_log, d_skip, gate_norm,
                   w_out, batch, seq):
    d_inner = gate_norm.shape[0]
    n_ssm_heads = dt_bias.shape[0]
    n_q_heads = sinks.shape[0]
    q_w = n_q_heads * HEAD_DIM
    kv_w = SWA_KV_HEADS * HEAD_DIM
    conv_dim = conv_w.shape[1]
    splits = (q_w, q_w + kv_w, q_w + 2 * kv_w, q_w + 2 * kv_w + d_inner, q_w + 2 * kv_w + d_inner + conv_dim)
    wq, wk, wv, wz, wxbc, wdt = jnp.split(w_in, splits, axis=1)
    w_qkv_dt = jnp.concatenate([wq, wk, wv, jnp.pad(wdt, ((0, 0), (0, LANES - n_ssm_heads)))], axis=1).astype(BF16)
    w_ssm = jnp.concatenate([wz, wxbc], axis=1).astype(BF16)
    qkv_w = q_w + 2 * kv_w
    head_gain = jnp.concatenate([jnp.tile(q_norm.astype(F32) * Q_SCALE, n_q_heads),
                                 jnp.tile(k_norm.astype(F32), SWA_KV_HEADS), jnp.ones((kv_w,), F32)])
    head_mask = jnp.concatenate([jnp.ones((q_w + kv_w,), F32), jnp.zeros((kv_w,), F32)])

    qkv, dt_raw = _norm_matmul_headnorm(x, gain, w_qkv_dt, head_gain, head_mask, tm=1024, tn=qkv_w,
                                        name="norm_qkv_dt_proj", n_extra=LANES)
    proj = _norm_matmul(x, gain, w_ssm, tm=1024, tn=w_ssm.shape[1] // 4, name="norm_ssm_proj")
    attn = _swa_attention(qkv, sinks, batch, seq, q_w)
    y = _ssd_mixer(proj, dt_raw, conv_w, conv_b, dt_bias, a_log, d_skip, gate_norm, batch, seq)
    w_out = w_out.astype(BF16)
    return _outproj_swa_ssd(x, attn, y, w_out[:q_w], w_out[q_w:], tm=1024, tn=1024)


def _dilated_layer(x, gain, w_qkv, q_norm, k_norm, w_o, batch, seq):
    width = w_o.shape[0]
    n_heads = width // HEAD_DIM
    w_qkv = w_qkv.astype(BF16)
    head_gain = jnp.concatenate([jnp.tile(q_norm.astype(F32) * Q_SCALE, n_heads),
                                 jnp.tile(k_norm.astype(F32), n_heads), jnp.ones((width,), F32)])
    head_mask = jnp.concatenate([jnp.ones((2 * width,), F32), jnp.zeros((width,), F32)])
    tn = 1024
    qkv = _norm_matmul_headnorm(x, gain, w_qkv, head_gain, head_mask, tm=1024, tn=tn, name="norm_qkv_proj",
                                interleave=RESIDUES, norm_tiles=2 * width // tn)
    outs, lses = _dilated_branches(qkv, batch, seq)
    return _outproj_dilated(x, outs, lses, w_o.astype(BF16), tm=512, tn=w_o.shape[1])


def kernel(x, norm_mix, norm_ffn, w_up, w_down, ab_w_in, ab_q_norm, ab_k_norm, ab_sinks, ab_conv_w, ab_conv_b,
           ab_dt_bias, ab_a_log, ab_d_skip, ab_gate_norm, ab_w_out, c_w_qkv, c_q_norm, c_k_norm, c_w_o):
    batch, seq, d_model = x.shape
    depth = norm_mix.shape[0]
    assert seq % SUPERBLOCK == 0 and seq % SSM_CHUNK == 0
    h = x.reshape(batch * seq, d_model).astype(F32)
    for layer in range(depth):
        i = layer // 2
        if layer % 2 == 0:
            h = _swa_ssd_layer(h, norm_mix[layer], ab_w_in[i], ab_q_norm[i], ab_k_norm[i], ab_sinks[i],
                               ab_conv_w[i], ab_conv_b[i], ab_dt_bias[i], ab_a_log[i], ab_d_skip[i],
                               ab_gate_norm[i], ab_w_out[i], batch, seq)
        else:
            h = _dilated_layer(h, norm_mix[layer], c_w_qkv[i], c_q_norm[i], c_k_norm[i], c_w_o[i], batch, seq)
        h = _mlp(h, norm_ffn[layer], w_up[layer].astype(BF16), w_down[layer].astype(BF16), tm=512, tf=1024)
    return h.reshape(batch, seq, d_model).astype(x.dtype)
```
